```python
import jax, jax.numpy as jnp
from jax import lax
import numpy as np

D_MODEL = 1024
BATCH = 2
SEQ = 8192
DEPTH = 2

N_A_LAYERS = DEPTH // 2
N_B_LAYERS = DEPTH - N_A_LAYERS

D_FF = ((8 * D_MODEL // 3 + 255) // 256) * 256
N_FFN_PER_LAYER = 2

POOL_WINDOWS = (2, 4, 8, 16)
POOL_GROUPS = len(POOL_WINDOWS)
POOL_GROUP_DIM = D_MODEL // POOL_GROUPS

ATTN_GROUPS = ((128, 1), (512, 4), (2048, 16))
N_ATTN_GROUPS = len(ATTN_GROUPS)
HEAD_DIM = 64
HEADS_PER_GROUP = D_MODEL // (2 * HEAD_DIM)
QKV_WIDTH = N_ATTN_GROUPS * HEADS_PER_GROUP * HEAD_DIM
O_WIDTH = HEADS_PER_GROUP * HEAD_DIM
ROPE_THETA = 10000.0

N_NORMS_PER_LAYER = 6
RMS_EPS = 1e-6
NEG_BIG = -1e30

kernel_name = "yoco_pool_dilated_attn_macaron"


def rms_norm(x, gain):
    x32 = x.astype(jnp.float32)
    y = x32 * lax.rsqrt(jnp.mean(x32 * x32, axis=-1, keepdims=True) + RMS_EPS) * gain.astype(jnp.float32)
    return y.astype(x.dtype)


def swiglu(h, w_gate, w_up, w_down):
    return (jax.nn.silu(h @ w_gate) * (h @ w_up)) @ w_down


def rope_tables(positions):
    inv_freq = ROPE_THETA ** (-jnp.arange(0, HEAD_DIM, 2, dtype=jnp.float32) / HEAD_DIM)
    ang = positions.astype(jnp.float32)[..., None] * inv_freq
    return jnp.cos(ang), jnp.sin(ang)


def apply_rope(x, cos, sin):
    extra = x.ndim - 3
    shape = cos.shape[:2] + (1,) * extra + cos.shape[-1:]
    c = cos.reshape(shape)
    s = sin.reshape(shape)
    x32 = x.astype(jnp.float32)
    x1, x2 = jnp.split(x32, 2, axis=-1)
    out = jnp.concatenate([x1 * c - x2 * s, x2 * c + x1 * s], axis=-1)
    return out.astype(x.dtype)


def pool_mixer(h, w_in, w_group, scale, w_out):
    B, S, _ = h.shape
    u = h @ w_in
    u32 = u.astype(jnp.float32)
    csum = jnp.cumsum(u32, axis=1)
    t = jnp.arange(S)
    pooled = []
    for g, w in enumerate(POOL_WINDOWS):
        lo, hi = g * POOL_GROUP_DIM, (g + 1) * POOL_GROUP_DIM
        c = csum[..., lo:hi]
        c_prev = jnp.pad(c, ((0, 0), (w, 0), (0, 0)))[:, :S]
        count = jnp.minimum(t + 1, w).astype(jnp.float32)[None, :, None]
        pooled.append((c - c_prev) / count - u32[..., lo:hi])
    p = jnp.stack(pooled, axis=2).astype(u.dtype)
    y = jnp.einsum('bsgc,gcd->bsgd', p, w_group).reshape(B, S, D_MODEL) * scale
    return y @ w_out


def dilated_window_attention(q, k, v, window, dilation):
    B, S, H, Dh = q.shape
    blk = window // dilation
    span = blk * dilation
    s_pad = -(-S // span) * span
    nb = s_pad // span

    def split(a):
        a = jnp.pad(a, ((0, 0), (0, s_pad - S), (0, 0), (0, 0)))
        return a.reshape(B, nb, blk, dilation, H, Dh)

    def with_prev(a):
        prev = jnp.pad(a[:, :-1], ((0, 0), (1, 0), (0, 0), (0, 0), (0, 0), (0, 0)))
        return jnp.concatenate([prev, a], axis=2)

    qb = split(q)
    kc = with_prev(split(k))
    vc = with_prev(split(v))
    s = jnp.einsum('bnqrhd,bnkrhd->bnrhqk', qb, kc, preferred_element_type=jnp.float32) * (Dh ** -0.5)
    qi = jnp.arange(blk)[:, None]
    kj = jnp.arange(2 * blk)[None, :]
    band = (kj >= qi) & (kj <= qi + blk)
    valid_prev = (jnp.arange(nb)[:, None, None] > 0) | (kj[None] >= blk)
    mask = band[None] & valid_prev
    s = jnp.where(mask[None, :, None, None], s, NEG_BIG)
    m = jnp.max(s, axis=-1, keepdims=True)
    p = jnp.exp(s - m)
    den = jnp.sum(p, axis=-1, keepdims=True)
    o = jnp.einsum('bnrhqk,bnkrhd->bnqrhd', (p / den).astype(v.dtype), vc)
    lse = (m + jnp.log(den))[..., 0]
    o = o.reshape(B, s_pad, H, Dh)[:, :S]
    lse = lse.transpose(0, 1, 4, 2, 3).reshape(B, s_pad, H)[:, :S]
    return o, lse


def shared_kv(h, kv_gain, w_k, w_v, cos, sin):
    B, S, _ = h.shape
    hk = rms_norm(h, kv_gain)
    k = (hk @ w_k).reshape(B, S, N_ATTN_GROUPS, HEADS_PER_GROUP, HEAD_DIM)
    v = (hk @ w_v).reshape(B, S, N_ATTN_GROUPS, HEADS_PER_GROUP, HEAD_DIM)
    return apply_rope(k, cos, sin), v


def dilated_mixer(h, cos, sin, w_q, k_sh, v_sh, w_o):
    B, S, _ = h.shape
    q = (h @ w_q).reshape(B, S, N_ATTN_GROUPS, HEADS_PER_GROUP, HEAD_DIM)
    q = apply_rope(q, cos, sin)
    outs, lses = [], []
    for g, (window, dilation) in enumerate(ATTN_GROUPS):
        o, l = dilated_window_attention(q[:, :, g], k_sh[:, :, g], v_sh[:, :, g], window, dilation)
        outs.append(o)
        lses.append(l)
    wts = jax.nn.softmax(jnp.stack(lses, axis=0), axis=0)
    o = jnp.sum(wts[..., None] * jnp.stack(outs, axis=0).astype(jnp.float32), axis=0).astype(h.dtype)
    return o.reshape(B, S, O_WIDTH) @ w_o


def setup_inputs(seed: int = 0) -> dict:
    key = jax.random.key(seed)
    ks = jax.random.split(key, 16)
    f32 = jnp.float32

    def w(k, shape, fan_in):
        return jax.random.normal(k, shape, f32) * (fan_in ** -0.5)

    x = jax.random.normal(ks[0], (BATCH, SEQ, D_MODEL), f32)
    offset = jax.random.randint(ks[1], (BATCH, 1), 0, 1024, dtype=jnp.int32)
    positions = (jnp.arange(SEQ, dtype=jnp.int32)[None, :] + offset).astype(jnp.int32)
    norm_gain = 1.0 + 0.05 * jax.random.normal(ks[2], (DEPTH, N_NORMS_PER_LAYER, D_MODEL), f32)
    ffn_w_gate = w(ks[3], (DEPTH, N_FFN_PER_LAYER, D_MODEL, D_FF), D_MODEL)
    ffn_w_up = w(ks[4], (DEPTH, N_FFN_PER_LAYER, D_MODEL, D_FF), D_MODEL)
    ffn_w_down = w(ks[5], (DEPTH, N_FFN_PER_LAYER, D_FF, D_MODEL), D_FF)
    pool_w_in = w(ks[6], (N_A_LAYERS, D_MODEL, D_MODEL), D_MODEL)
    pool_w_group = w(ks[7], (N_A_LAYERS, POOL_GROUPS, POOL_GROUP_DIM, POOL_GROUP_DIM), POOL_GROUP_DIM)
    pool_scale = 1.0 + 0.1 * jax.random.normal(ks[8], (N_A_LAYERS, D_MODEL), f32)
    pool_w_out = w(ks[9], (N_A_LAYERS, D_MODEL, D_MODEL), D_MODEL)
    kv_norm_gain = 1.0 + 0.05 * jax.random.normal(ks[10], (D_MODEL,), f32)
    w_k = w(ks[11], (D_MODEL, QKV_WIDTH), D_MODEL)
    w_v = w(ks[12], (D_MODEL, QKV_WIDTH), D_MODEL)
    attn_w_q = w(ks[13], (N_B_LAYERS, D_MODEL, QKV_WIDTH), D_MODEL)
    attn_w_o = w(ks[14], (N_B_LAYERS, O_WIDTH, D_MODEL), O_WIDTH)
    return {
        "x": x, "positions": positions, "norm_gain": norm_gain,
        "ffn_w_gate": ffn_w_gate, "ffn_w_up": ffn_w_up, "ffn_w_down": ffn_w_down,
        "pool_w_in": pool_w_in, "pool_w_group": pool_w_group, "pool_scale": pool_scale,
        "pool_w_out": pool_w_out, "kv_norm_gain": kv_norm_gain, "w_k": w_k, "w_v": w_v,
        "attn_w_q": attn_w_q, "attn_w_o": attn_w_o,
    }


def reference(x, positions, norm_gain, ffn_w_gate, ffn_w_up, ffn_w_down, pool_w_in, pool_w_group,
              pool_scale, pool_w_out, kv_norm_gain, w_k, w_v, attn_w_q, attn_w_o):
    cos, sin = rope_tables(positions)
    h = x
    k_sh = None
    v_sh = None
    for layer in range(DEPTH):
        g = norm_gain[layer]
        f = swiglu(rms_norm(h, g[0]), ffn_w_gate[layer, 0], ffn_w_up[layer, 0], ffn_w_down[layer, 0])
        h = h + 0.5 * rms_norm(f, g[1])
        hm = rms_norm(h, g[2])
        if layer < N_A_LAYERS:
            mix = pool_mixer(hm, pool_w_in[layer], pool_w_group[layer], pool_scale[layer], pool_w_out[layer])
        else:
            b = layer - N_A_LAYERS
            mix = dilated_mixer(hm, cos, sin, attn_w_q[b], k_sh, v_sh, attn_w_o[b])
        h = h + rms_norm(mix, g[3])
        f = swiglu(rms_norm(h, g[4]), ffn_w_gate[layer, 1], ffn_w_up[layer, 1], ffn_w_down[layer, 1])
        h = h + 0.5 * rms_norm(f, g[5])
        if layer == N_A_LAYERS - 1:
            k_sh, v_sh = shared_kv(h, kv_norm_gain, w_k, w_v, cos, sin)
    return h
```

```python
import functools

import jax
import jax.numpy as jnp
from jax import lax
from jax.experimental import pallas as pl
from jax.experimental.pallas import tpu as pltpu

D_MODEL = 1024
D_FF = 2816
POOL_WINDOWS = (2, 4, 8, 16)
POOL_GROUP_DIM = D_MODEL // len(POOL_WINDOWS)
POOL_HALO = 16
ATTN_DILATIONS = (1, 4, 16)
ATTN_BLK = 128
HEAD_DIM = 64
HEADS_PER_GROUP = 8
GROUP_WIDTH = HEADS_PER_GROUP * HEAD_DIM
QKV_WIDTH = len(ATTN_DILATIONS) * GROUP_WIDTH
ROPE_THETA = 10000.0
RMS_EPS = 1e-6
NEG_BIG = -1e30

LANES = 128
ROW_TILE = 512
FF_CHUNK = 256
VMEM_LIMIT_BYTES = 56 * 1024 * 1024

_F32 = jnp.float32
_BF16 = jnp.bfloat16


def _rms_norm(x, gain):
    ms = jnp.mean(x * x, axis=-1, keepdims=True)
    return x * lax.rsqrt(ms + RMS_EPS) * gain


def _dot(a, b):
    return jnp.dot(a, b, preferred_element_type=_F32)


def _const_spec(shape):
    return pl.BlockSpec(shape, lambda *_: (0,) * len(shape), pipeline_mode=pl.Buffered(1))


def _params(semantics):
    return pltpu.CompilerParams(dimension_semantics=semantics, vmem_limit_bytes=VMEM_LIMIT_BYTES)


def _ffn_kernel(h_ref, gpre_ref, gpost_ref, wg_ref, wu_ref, wd_ref, o_ref):
    x = h_ref[...]
    xn = _rms_norm(x, gpre_ref[...]).astype(_BF16)
    acc = jnp.zeros(x.shape, _F32)
    for c in range(0, D_FF, FF_CHUNK):
        g = _dot(xn, wg_ref[:, c:c + FF_CHUNK])
        u = _dot(xn, wu_ref[:, c:c + FF_CHUNK])
        a = (g * jax.nn.sigmoid(g) * u).astype(_BF16)
        acc = acc + _dot(a, wd_ref[c:c + FF_CHUNK, :])
    o_ref[...] = x + 0.5 * _rms_norm(acc, gpost_ref[...])


def _ffn(h, gpre, gpost, wg, wu, wd):
    m = h.shape[0]
    row = pl.BlockSpec((ROW_TILE, D_MODEL), lambda i: (i, 0))
    return pl.pallas_call(
        _ffn_kernel,
        grid=(m // ROW_TILE,),
        in_specs=[row, _const_spec((1, D_MODEL)), _const_spec((1, D_MODEL)),
                  _const_spec((D_MODEL, D_FF)), _const_spec((D_MODEL, D_FF)), _const_spec((D_FF, D_MODEL))],
        out_specs=row,
        out_shape=jax.ShapeDtypeStruct((m, D_MODEL), _F32),
        compiler_params=_params(("parallel",)),
        name="ffn",
    )(h, gpre, gpost, wg, wu, wd)


def _pool_kernel(h_ref, gpre_ref, gpost_ref, win_ref, wgrp_ref, scale_ref, wout_ref, o_ref, ubuf_ref):
    j = pl.program_id(1)
    x = h_ref[0]
    tm = x.shape[0]
    hm = _rms_norm(x, gpre_ref[...]).astype(_BF16)
    u = _dot(hm, win_ref[...])

    @pl.when(j == 0)
    def _():
        ubuf_ref[0:POOL_HALO, :] = jnp.zeros((POOL_HALO, D_MODEL), _F32)

    ubuf_ref[POOL_HALO:, :] = u
    t = j * tm + lax.broadcasted_iota(jnp.int32, (tm, 1), 0)
    ys = []
    for g, w in enumerate(POOL_WINDOWS):
        lo = g * POOL_GROUP_DIM
        s = ubuf_ref[:, lo:lo + POOL_GROUP_DIM]
        k = 1
        while k < w:
            s = s + pltpu.roll(s, k, axis=0)
            k *= 2
        count = jnp.minimum(t + 1, w).astype(_F32)
        p = s[POOL_HALO:] / count - u[:, lo:lo + POOL_GROUP_DIM]
        ys.append(_dot(p.astype(_BF16), wgrp_ref[g]))
    y = jnp.concatenate(ys, axis=-1) * scale_ref[...]
    mix = _dot(y.astype(_BF16), wout_ref[...])
    ubuf_ref[0:POOL_HALO, :] = u[tm - POOL_HALO:, :]
    o_ref[0] = x + _rms_norm(mix, gpost_ref[...])


def _pool(h3, gpre, gpost, w_in, w_group, scale, w_out):
    b, s, _ = h3.shape
    row = pl.BlockSpec((1, ROW_TILE, D_MODEL), lambda bi, j: (bi, j, 0))
    n_grp = len(POOL_WINDOWS)
    return pl.pallas_call(
        _pool_kernel,
        grid=(b, s // ROW_TILE),
        in_specs=[row, _const_spec((1, D_MODEL)), _const_spec((1, D_MODEL)),
                  _const_spec((D_MODEL, D_MODEL)), _const_spec((n_grp, POOL_GROUP_DIM, POOL_GROUP_DIM)),
                  _const_spec((1, D_MODEL)), _const_spec((D_MODEL, D_MODEL))],
        out_specs=row,
        out_shape=jax.ShapeDtypeStruct(h3.shape, _F32),
        scratch_shapes=[pltpu.VMEM((POOL_HALO + ROW_TILE, D_MODEL), _F32)],
        compiler_params=_params(("arbitrary", "arbitrary")),
        name="pool_mixer",
    )(h3, gpre, gpost, w_in, w_group, scale, w_out)


def _rope_tables(pos, invf):
    ang = pos * invf
    c = jnp.cos(ang)
    s = jnp.sin(ang)
    lane = lax.broadcasted_iota(jnp.int32, ang.shape, 1)
    upper = (lane % HEAD_DIM) >= HEAD_DIM // 2
    s_from_lower = jnp.where(upper, s, 0.0)
    s_from_upper = jnp.where(upper, 0.0, -s)
    return c, s_from_lower, s_from_upper


def _rope(x, tables):
    c, s_lo, s_up = tables
    half = HEAD_DIM // 2
    outs = []
    for j in range(x.shape[1] // LANES):
        xt = x[:, j * LANES:(j + 1) * LANES]
        outs.append(xt * c + pltpu.roll(xt, half, axis=1) * s_lo + pltpu.roll(xt, LANES - half, axis=1) * s_up)
    return jnp.concatenate(outs, axis=-1)


def _kv_kernel(h_ref, gain_ref, pos_ref, invf_ref, wk_ref, wv_ref, k_ref, v_ref):
    hk = _rms_norm(h_ref[...], gain_ref[...]).astype(_BF16)
    tables = _rope_tables(pos_ref[...], invf_ref[...])
    k_ref[...] = _rope(_dot(hk, wk_ref[...]), tables).astype(_BF16)
    v_ref[...] = _dot(hk, wv_ref[...]).astype(_BF16)


def _shared_kv(h, gain, pos, invf, w_k, w_v):
    m = h.shape[0]
    out = pl.BlockSpec((ROW_TILE, QKV_WIDTH), lambda i: (i, 0))
    return pl.pallas_call(
        _kv_kernel,
        grid=(m // ROW_TILE,),
        in_specs=[pl.BlockSpec((ROW_TILE, D_MODEL), lambda i: (i, 0)), _const_spec((1, D_MODEL)),
                  pl.BlockSpec((ROW_TILE, 1), lambda i: (i, 0)), _const_spec((1, LANES)),
                  _const_spec((D_MODEL, QKV_WIDTH)), _const_spec((D_MODEL, QKV_WIDTH))],
        out_specs=[out, out],
        out_shape=[jax.ShapeDtypeStruct((m, QKV_WIDTH), _BF16)] * 2,
        compiler_params=_params(("parallel",)),
        name="shared_kv",
    )(h, gain, pos, invf, w_k, w_v)


def _q_kernel(h_ref, gain_ref, pos_ref, invf_ref, wq_ref, q_ref):
    hm = _rms_norm(h_ref[...], gain_ref[...]).astype(_BF16)
    tables = _rope_tables(pos_ref[...], invf_ref[...])
    q_ref[...] = (_rope(_dot(hm, wq_ref[...]), tables) * (HEAD_DIM ** -0.5)).astype(_BF16)


def _q_proj(h, gain, pos, invf, w_q):
    m = h.shape[0]
    return pl.pallas_call(
        _q_kernel,
        grid=(m // ROW_TILE,),
        in_specs=[pl.BlockSpec((ROW_TILE, D_MODEL), lambda i: (i, 0)), _const_spec((1, D_MODEL)),
                  pl.BlockSpec((ROW_TILE, 1), lambda i: (i, 0)), _const_spec((1, LANES)),
                  _const_spec((D_MODEL, QKV_WIDTH))],
        out_specs=pl.BlockSpec((ROW_TILE, QKV_WIDTH), lambda i: (i, 0)),
        out_shape=jax.ShapeDtypeStruct((m, QKV_WIDTH), _BF16),
        compiler_params=_params(("parallel",)),
        name="q_proj",
    )(h, gain, pos, invf, w_q)


def _attn_kernel(q_ref, kp_ref, kc_ref, vp_ref, vc_ref, o_ref, lse_ref):
    n = pl.program_id(2)
    qi = lax.broadcasted_iota(jnp.int32, (ATTN_BLK, ATTN_BLK), 0)
    kj = lax.broadcasted_iota(jnp.int32, (ATTN_BLK, ATTN_BLK), 1)
    prev_ok = kj >= qi + jnp.where(n > 0, 0, ATTN_BLK)
    cur_ok = kj <= qi
    nt = (((1,), (1,)), ((), ()))
    for h in range(HEADS_PER_GROUP):
        cols = slice(h * HEAD_DIM, (h + 1) * HEAD_DIM)
        q = q_ref[0, :, cols]
        s_p = lax.dot_general(q, kp_ref[0, :, cols], nt, preferred_element_type=_F32)
        s_c = lax.dot_general(q, kc_ref[0, :, cols], nt, preferred_element_type=_F32)
        s_p = jnp.where(prev_ok, s_p, NEG_BIG)
        s_c = jnp.where(cur_ok, s_c, NEG_BIG)
        m = jnp.maximum(jnp.max(s_p, axis=-1, keepdims=True), jnp.max(s_c, axis=-1, keepdims=True))
        p_p = jnp.exp(s_p - m)
        p_c = jnp.exp(s_c - m)
        den = jnp.sum(p_p, axis=-1, keepdims=True) + jnp.sum(p_c, axis=-1, keepdims=True)
        o = _dot(p_p.astype(_BF16), vp_ref[0, :, cols]) + _dot(p_c.astype(_BF16), vc_ref[0, :, cols])
        o_ref[0, :, cols] = o / den
        lse_ref[0, :, cols] = jnp.broadcast_to(m + jnp.log(den), (ATTN_BLK, HEAD_DIM))


def _attn_group(q, k, v, group, dilation, batch, seq):
    rows = seq // dilation
    shape = (batch, rows, dilation * QKV_WIDTH)
    n_col = QKV_WIDTH // GROUP_WIDTH

    def cur(b, r, n):
        return (b, n, r * n_col + group)

    def prev(b, r, n):
        return (b, jnp.maximum(n - 1, 0), r * n_col + group)

    blk = (1, ATTN_BLK, GROUP_WIDTH)
    out_spec = pl.BlockSpec(blk, lambda b, r, n: (b, n, r))
    out_shape = jax.ShapeDtypeStruct((batch, rows, dilation * GROUP_WIDTH), _F32)
    o, lse = pl.pallas_call(
        _attn_kernel,
        grid=(batch, dilation, rows // ATTN_BLK),
        in_specs=[pl.BlockSpec(blk, cur), pl.BlockSpec(blk, prev), pl.BlockSpec(blk, cur),
                  pl.BlockSpec(blk, prev), pl.BlockSpec(blk, cur)],
        out_specs=[out_spec, out_spec],
        out_shape=[out_shape, out_shape],
        compiler_params=_params(("parallel", "parallel", "arbitrary")),
        name=f"dilated_attn_g{group}",
    )(q.reshape(shape), k.reshape(shape), k.reshape(shape), v.reshape(shape), v.reshape(shape))
    return o.reshape(batch * seq, GROUP_WIDTH), lse.reshape(batch * seq, GROUP_WIDTH)


def _merge_kernel(h_ref, gpost_ref, wo_ref, o0_ref, o1_ref, o2_ref, l0_ref, l1_ref, l2_ref, out_ref):
    l0, l1, l2 = l0_ref[...], l1_ref[...], l2_ref[...]
    top = jnp.maximum(jnp.maximum(l0, l1), l2)
    e0, e1, e2 = jnp.exp(l0 - top), jnp.exp(l1 - top), jnp.exp(l2 - top)
    o = (e0 * o0_ref[...] + e1 * o1_ref[...] + e2 * o2_ref[...]) / (e0 + e1 + e2)
    mix = _dot(o.astype(_BF16), wo_ref[...])
    out_ref[...] = h_ref[...] + _rms_norm(mix, gpost_ref[...])


def _merge(h, gpost, w_o, outs, lses):
    m = h.shape[0]
    row = pl.BlockSpec((ROW_TILE, D_MODEL), lambda i: (i, 0))
    grp = pl.BlockSpec((ROW_TILE, GROUP_WIDTH), lambda i: (i, 0))
    return pl.pallas_call(
        _merge_kernel,
        grid=(m // ROW_TILE,),
        in_specs=[row, _const_spec((1, D_MODEL)), _const_spec((GROUP_WIDTH, D_MODEL))] + [grp] * 6,
        out_specs=row,
        out_shape=jax.ShapeDtypeStruct((m, D_MODEL), _F32),
        compiler_params=_params(("parallel",)),
        name="attn_merge",
    )(h, gpost, w_o, *outs, *lses)


def kernel(x, positions, norm_gain, ffn_w_gate, ffn_w_up, ffn_w_down, pool_w_in, pool_w_group, pool_scale,
           pool_w_out, kv_norm_gain, w_k, w_v, attn_w_q, attn_w_o):
    batch, seq, _ = x.shape
    m = batch * seq
    gains = norm_gain.reshape(2, 6, 1, D_MODEL)
    bf = lambda w: w.astype(_BF16)

    def ffn(h, layer, slot):
        return _ffn(h, gains[layer, 4 * slot], gains[layer, 4 * slot + 1],
                    bf(ffn_w_gate[layer, slot]), bf(ffn_w_up[layer, slot]), bf(ffn_w_down[layer, slot]))

    pos = positions.astype(_F32).reshape(m, 1)
    inv_freq = ROPE_THETA ** (-jnp.arange(0, HEAD_DIM, 2, dtype=_F32) / HEAD_DIM)
    invf = jnp.tile(inv_freq, LANES // (HEAD_DIM // 2)).reshape(1, LANES)

    h = x.reshape(m, D_MODEL)
    h = ffn(h, 0, 0)
    h = _pool(h.reshape(batch, seq, D_MODEL), gains[0, 2], gains[0, 3], bf(pool_w_in[0]), bf(pool_w_group[0]),
              pool_scale[0].reshape(1, D_MODEL), bf(pool_w_out[0])).reshape(m, D_MODEL)
    h = ffn(h, 0, 1)
    k_sh, v_sh = _shared_kv(h, kv_norm_gain.reshape(1, D_MODEL), pos, invf, bf(w_k), bf(w_v))
    h = ffn(h, 1, 0)
    q = _q_proj(h, gains[1, 2], pos, invf, bf(attn_w_q[0]))
    outs, lses = [], []
    for group, dilation in enumerate(ATTN_DILATIONS):
        o, lse = _attn_group(q, k_sh, v_sh, group, dilation, batch, seq)
        outs.append(o)
        lses.append(lse)
    h = _merge(h, gains[1, 3], bf(attn_w_o[0]), outs, lses)
    h = ffn(h, 1, 1)
    return h.reshape(batch, seq, D_MODEL)
```

```python
import functools

import jax
import jax.numpy as jnp
from jax import lax
from jax.experimental import pallas as pl
from jax.experimental.pallas import tpu as pltpu

D_MODEL = 1024
D_FF = 2816
POOL_WINDOWS = (2, 4, 8, 16)
POOL_GROUP_DIM = D_MODEL // len(POOL_WINDOWS)
POOL_HALO = 16
ATTN_DILATIONS = (1, 4, 16)
N_GROUPS = len(ATTN_DILATIONS)
ATTN_BLK = 128
HEAD_DIM = 64
HEADS_PER_GROUP = 8
GROUP_WIDTH = HEADS_PER_GROUP * HEAD_DIM
QKV_WIDTH = N_GROUPS * GROUP_WIDTH
ROPE_THETA = 10000.0
RMS_EPS = 1e-6
NEG_BIG = -1e30

LANES = 128
SLOT_WIDTH = HEADS_PER_GROUP * LANES
PAIRS_PER_GROUP = GROUP_WIDTH // LANES
ROW_TILE = 512
ATTN_Q_TILE = 512
FF_CHUNK = 256
VMEM_LIMIT_BYTES = 56 * 1024 * 1024

_F32 = jnp.float32
_BF16 = jnp.bfloat16


def _rms_norm(x, gain):
    ms = jnp.mean(x * x, axis=-1, keepdims=True)
    return x * lax.rsqrt(ms + RMS_EPS) * gain


def _dot(a, b):
    return jnp.dot(a, b, preferred_element_type=_F32)


def _const_spec(shape):
    return pl.BlockSpec(shape, lambda *_: (0,) * len(shape), pipeline_mode=pl.Buffered(1))


def _params(semantics):
    return pltpu.CompilerParams(dimension_semantics=semantics, vmem_limit_bytes=VMEM_LIMIT_BYTES)


def _ffn_kernel(h_ref, gpre_ref, gpost_ref, wg_ref, wu_ref, wd_ref, o_ref):
    x = h_ref[...]
    xn = _rms_norm(x, gpre_ref[...]).astype(_BF16)
    acc = jnp.zeros(x.shape, _F32)
    for c in range(0, D_FF, FF_CHUNK):
        g = _dot(xn, wg_ref[:, c:c + FF_CHUNK])
        u = _dot(xn, wu_ref[:, c:c + FF_CHUNK])
        a = (g * jax.nn.sigmoid(g) * u).astype(_BF16)
        acc = acc + _dot(a, wd_ref[c:c + FF_CHUNK, :])
    o_ref[...] = x + 0.5 * _rms_norm(acc, gpost_ref[...])


def _ffn(h, gpre, gpost, wg, wu, wd):
    m = h.shape[0]
    row = pl.BlockSpec((ROW_TILE, D_MODEL), lambda i: (i, 0))
    return pl.pallas_call(
        _ffn_kernel,
        grid=(m // ROW_TILE,),
        in_specs=[row, _const_spec((1, D_MODEL)), _const_spec((1, D_MODEL)),
                  _const_spec((D_MODEL, D_FF)), _const_spec((D_MODEL, D_FF)), _const_spec((D_FF, D_MODEL))],
        out_specs=row,
        out_shape=jax.ShapeDtypeStruct((m, D_MODEL), _F32),
        compiler_params=_params(("parallel",)),
        name="ffn",
    )(h, gpre, gpost, wg, wu, wd)


def _pool_kernel(h_ref, gpre_ref, gpost_ref, win_ref, wgrp_ref, scale_ref, wout_ref, o_ref, ubuf_ref):
    j = pl.program_id(1)
    x = h_ref[0]
    tm = x.shape[0]
    hm = _rms_norm(x, gpre_ref[...]).astype(_BF16)
    u = _dot(hm, win_ref[...])

    @pl.when(j == 0)
    def _():
        ubuf_ref[0:POOL_HALO, :] = jnp.zeros((POOL_HALO, D_MODEL), _F32)

    ubuf_ref[POOL_HALO:, :] = u
    t = j * tm + lax.broadcasted_iota(jnp.int32, (tm, 1), 0)
    ys = []
    for g, w in enumerate(POOL_WINDOWS):
        lo = g * POOL_GROUP_DIM
        s = ubuf_ref[:, lo:lo + POOL_GROUP_DIM]
        k = 1
        while k < w:
            s = s + pltpu.roll(s, k, axis=0)
            k *= 2
        count = jnp.minimum(t + 1, w).astype(_F32)
        p = s[POOL_HALO:] / count - u[:, lo:lo + POOL_GROUP_DIM]
        ys.append(_dot(p.astype(_BF16), wgrp_ref[g]))
    y = jnp.concatenate(ys, axis=-1) * scale_ref[...]
    mix = _dot(y.astype(_BF16), wout_ref[...])
    ubuf_ref[0:POOL_HALO, :] = u[tm - POOL_HALO:, :]
    o_ref[0] = x + _rms_norm(mix, gpost_ref[...])


def _pool(h3, gpre, gpost, w_in, w_group, scale, w_out):
    b, s, _ = h3.shape
    row = pl.BlockSpec((1, ROW_TILE, D_MODEL), lambda bi, j: (bi, j, 0))
    n_grp = len(POOL_WINDOWS)
    return pl.pallas_call(
        _pool_kernel,
        grid=(b, s // ROW_TILE),
        in_specs=[row, _const_spec((1, D_MODEL)), _const_spec((1, D_MODEL)),
                  _const_spec((D_MODEL, D_MODEL)), _const_spec((n_grp, POOL_GROUP_DIM, POOL_GROUP_DIM)),
                  _const_spec((1, D_MODEL)), _const_spec((D_MODEL, D_MODEL))],
        out_specs=row,
        out_shape=jax.ShapeDtypeStruct(h3.shape, _F32),
        scratch_shapes=[pltpu.VMEM((POOL_HALO + ROW_TILE, D_MODEL), _F32)],
        compiler_params=_params(("arbitrary", "arbitrary")),
        name="pool_mixer",
    )(h3, gpre, gpost, w_in, w_group, scale, w_out)


def _rope_tables(pos, invf):
    ang = pos * invf
    c = jnp.cos(ang)
    s = jnp.sin(ang)
    lane = lax.broadcasted_iota(jnp.int32, ang.shape, 1)
    upper = (lane % HEAD_DIM) >= HEAD_DIM // 2
    s_from_lower = jnp.where(upper, s, 0.0)
    s_from_upper = jnp.where(upper, 0.0, -s)
    return c, s_from_lower, s_from_upper


def _rope_tile(xt, tables):
    c, s_lo, s_up = tables
    half = HEAD_DIM // 2
    return xt * c + pltpu.roll(xt, half, axis=1) * s_lo + pltpu.roll(xt, LANES - half, axis=1) * s_up


def _project_to_slabs(xn, w_ref, slab_ref, tables=None, scale=None):
    y = _dot(xn, w_ref[...])
    for j in range(y.shape[1] // LANES):
        yt = y[:, j * LANES:(j + 1) * LANES]
        if tables is not None:
            yt = _rope_tile(yt, tables)
        if scale is not None:
            yt = yt * scale
        slab_ref[j] = yt


def _residue_rows(slab_ref, j, r, dilation):
    n = slab_ref.shape[1] // dilation
    return slab_ref[j, pl.ds(r, n, stride=dilation), :]


def _store_compact(slab_ref, out_refs):
    for g, d in enumerate(ATTN_DILATIONS):
        for r in range(d):
            for p in range(PAIRS_PER_GROUP):
                rows = _residue_rows(slab_ref, g * PAIRS_PER_GROUP + p, r, d)
                out_refs[g][0, r, :, p * LANES:(p + 1) * LANES] = rows.astype(_BF16)


def _store_slotted(slab_ref, out_refs, pad):
    for g, d in enumerate(ATTN_DILATIONS):
        for r in range(d):
            for p in range(PAIRS_PER_GROUP):
                rows = _residue_rows(slab_ref, g * PAIRS_PER_GROUP + p, r, d)
                lower = lax.broadcasted_iota(jnp.int32, rows.shape, 1) < HEAD_DIM
                even = jnp.where(lower, rows, pad).astype(_BF16)
                odd = jnp.where(lower, pad, rows).astype(_BF16)
                out_refs[g][0, r, :, (2 * p) * LANES:(2 * p + 1) * LANES] = even
                out_refs[g][0, r, :, (2 * p + 1) * LANES:(2 * p + 2) * LANES] = odd


def _kv_kernel(h_ref, gain_ref, pos_ref, invf_ref, wk_ref, wv_ref,
               k0_ref, k1_ref, k2_ref, v0_ref, v1_ref, v2_ref, kslab_ref, vslab_ref):
    hk = _rms_norm(h_ref[0], gain_ref[...]).astype(_BF16)
    tables = _rope_tables(pos_ref[0], invf_ref[...])
    _project_to_slabs(hk, wk_ref, kslab_ref, tables=tables)
    _project_to_slabs(hk, wv_ref, vslab_ref)
    _store_slotted(kslab_ref, (k0_ref, k1_ref, k2_ref), 0.0)
    _store_slotted(vslab_ref, (v0_ref, v1_ref, v2_ref), 0.0)


def _residue_major_specs(batch, seq, width, dtype):
    specs, shapes = [], []
    for d in ATTN_DILATIONS:
        specs.append(pl.BlockSpec((1, d, ROW_TILE // d, width), lambda b, i: (b, 0, i, 0)))
        shapes.append(jax.ShapeDtypeStruct((batch, d, seq // d, width), dtype))
    return specs, shapes


def _shared_kv(h3, gain, pos3, invf, w_k, w_v):
    batch, seq, _ = h3.shape
    specs, shapes = _residue_major_specs(batch, seq, SLOT_WIDTH, _BF16)
    slab = pltpu.VMEM((QKV_WIDTH // LANES, ROW_TILE, LANES), _F32)
    outs = pl.pallas_call(
        _kv_kernel,
        grid=(batch, seq // ROW_TILE),
        in_specs=[pl.BlockSpec((1, ROW_TILE, D_MODEL), lambda b, i: (b, i, 0)), _const_spec((1, D_MODEL)),
                  pl.BlockSpec((1, ROW_TILE, 1), lambda b, i: (b, i, 0)), _const_spec((1, LANES)),
                  _const_spec((D_MODEL, QKV_WIDTH)), _const_spec((D_MODEL, QKV_WIDTH))],
        out_specs=specs * 2,
        out_shape=shapes * 2,
        scratch_shapes=[slab, slab],
        compiler_params=_params(("parallel", "parallel")),
        name="shared_kv",
    )(h3, gain, pos3, invf, w_k, w_v)
    return outs[:N_GROUPS], outs[N_GROUPS:]


def _q_kernel(h_ref, gain_ref, pos_ref, invf_ref, wq_ref, q0_ref, q1_ref, q2_ref, qslab_ref):
    hm = _rms_norm(h_ref[0], gain_ref[...]).astype(_BF16)
    tables = _rope_tables(pos_ref[0], invf_ref[...])
    _project_to_slabs(hm, wq_ref, qslab_ref, tables=tables, scale=HEAD_DIM ** -0.5)
    _store_compact(qslab_ref, (q0_ref, q1_ref, q2_ref))


def _q_proj(h3, gain, pos3, invf, w_q):
    batch, seq, _ = h3.shape
    specs, shapes = _residue_major_specs(batch, seq, GROUP_WIDTH, _BF16)
    return pl.pallas_call(
        _q_kernel,
        grid=(batch, seq // ROW_TILE),
        in_specs=[pl.BlockSpec((1, ROW_TILE, D_MODEL), lambda b, i: (b, i, 0)), _const_spec((1, D_MODEL)),
                  pl.BlockSpec((1, ROW_TILE, 1), lambda b, i: (b, i, 0)), _const_spec((1, LANES)),
                  _const_spec((D_MODEL, QKV_WIDTH))],
        out_specs=specs,
        out_shape=shapes,
        scratch_shapes=[pltpu.VMEM((QKV_WIDTH // LANES, ROW_TILE, LANES), _F32)],
        compiler_params=_params(("parallel", "parallel")),
        name="q_proj",
    )(h3, gain, pos3, invf, w_q)


def _attn_kernel(q_ref, kp_ref, kc_ref, vp_ref, vc_ref, o_ref, lse_ref):
    step = pl.program_id(1)
    n_blk = ATTN_Q_TILE // ATTN_BLK
    qi = lax.broadcasted_iota(jnp.int32, (ATTN_BLK, 2 * ATTN_BLK), 0)
    kj = lax.broadcasted_iota(jnp.int32, (ATTN_BLK, 2 * ATTN_BLK), 1)
    band = (kj >= qi) & (kj <= qi + ATTN_BLK)
    band_first = (kj >= jnp.maximum(qi, jnp.where(step == 0, ATTN_BLK, 0))) & (kj <= qi + ATTN_BLK)
    nt = (((1,), (1,)), ((), ()))
    work = [(j, h) for j in range(n_blk) for h in range(HEADS_PER_GROUP)]

    def window(prev_ref, cur_ref, j, h):
        slot = slice(h * LANES, (h + 1) * LANES)
        if j == 0:
            return jnp.concatenate([prev_ref[0, :, slot], cur_ref[0, 0:ATTN_BLK, slot]], axis=0)
        return cur_ref[0, (j - 1) * ATTN_BLK:(j + 1) * ATTN_BLK, slot]

    def scores(j, h):
        pair = slice((h // 2) * LANES, (h // 2 + 1) * LANES)
        q = q_ref[0, j * ATTN_BLK:(j + 1) * ATTN_BLK, pair]
        s = lax.dot_general(q, window(kp_ref, kc_ref, j, h), nt, preferred_element_type=_F32)
        return jnp.where(band_first if j == 0 else band, s, NEG_BIG)

    def finish(j, h, s):
        m = jnp.max(s, axis=-1, keepdims=True)
        p = jnp.exp(s - m)
        den = jnp.sum(p, axis=-1, keepdims=True)
        o = _dot(p.astype(_BF16), window(vp_ref, vc_ref, j, h)) * (1.0 / den)
        rows = slice(j * ATTN_BLK, (j + 1) * ATTN_BLK)
        cols = slice(h * HEAD_DIM, (h + 1) * HEAD_DIM)
        half = slice((h % 2) * HEAD_DIM, (h % 2 + 1) * HEAD_DIM)
        o_ref[0, rows, cols] = o[:, half]
        lse_ref[0, rows, cols] = jnp.broadcast_to(m + jnp.log(den), (ATTN_BLK, HEAD_DIM))

    ahead = 2
    pending = [scores(*work[n]) for n in range(ahead)]
    for n, (j, h) in enumerate(work):
        s = pending.pop(0)
        if n + ahead < len(work):
            pending.append(scores(*work[n + ahead]))
        finish(j, h, s)


def _attn_group(q, k, v):
    batch, d, rows, _ = q.shape
    n_seq = batch * d
    q = q.reshape(n_seq, rows, GROUP_WIDTH)
    k = k.reshape(n_seq, rows, SLOT_WIDTH)
    v = v.reshape(n_seq, rows, SLOT_WIDTH)
    blocks_per_step = ATTN_Q_TILE // ATTN_BLK

    def cur(s, i):
        return (s, i, 0)

    def prev(s, i):
        return (s, jnp.maximum(i * blocks_per_step - 1, 0), 0)

    out_spec = pl.BlockSpec((1, ATTN_Q_TILE, GROUP_WIDTH), cur)
    out_shape = jax.ShapeDtypeStruct((n_seq, rows, GROUP_WIDTH), _F32)
    o, lse = pl.pallas_call(
        _attn_kernel,
        grid=(n_seq, rows // ATTN_Q_TILE),
        in_specs=[pl.BlockSpec((1, ATTN_Q_TILE, GROUP_WIDTH), cur),
                  pl.BlockSpec((1, ATTN_BLK, SLOT_WIDTH), prev), pl.BlockSpec((1, ATTN_Q_TILE, SLOT_WIDTH), cur),
                  pl.BlockSpec((1, ATTN_BLK, SLOT_WIDTH), prev), pl.BlockSpec((1, ATTN_Q_TILE, SLOT_WIDTH), cur)],
        out_specs=[out_spec, out_spec],
        out_shape=[out_shape, out_shape],
        compiler_params=_params(("parallel", "arbitrary")),
        name=f"window_attn_d{d}",
    )(q, k, k, v, v)
    shape = (batch, d, rows, GROUP_WIDTH)
    return o.reshape(shape), lse.reshape(shape)


def _merge_kernel(h_ref, gpost_ref, wo_ref, o0_ref, o1_ref, o2_ref, l0_ref, l1_ref, l2_ref, out_ref, nat_ref):
    for a, (ref, d) in enumerate(((o1_ref, ATTN_DILATIONS[1]), (l1_ref, ATTN_DILATIONS[1]),
                                  (o2_ref, ATTN_DILATIONS[2]), (l2_ref, ATTN_DILATIONS[2]))):
        n = ROW_TILE // d
        for r in range(d):
            for p in range(PAIRS_PER_GROUP):
                nat_ref[a, p, pl.ds(r, n, stride=d), :] = ref[0, r, :, p * LANES:(p + 1) * LANES]

    def natural(a):
        return jnp.concatenate([nat_ref[a, p] for p in range(PAIRS_PER_GROUP)], axis=-1)

    o0, l0 = o0_ref[0, 0], l0_ref[0, 0]
    o1, l1, o2, l2 = natural(0), natural(1), natural(2), natural(3)
    top = jnp.maximum(jnp.maximum(l0, l1), l2)
    e0, e1, e2 = jnp.exp(l0 - top), jnp.exp(l1 - top), jnp.exp(l2 - top)
    o = (e0 * o0 + e1 * o1 + e2 * o2) / (e0 + e1 + e2)
    mix = _dot(o.astype(_BF16), wo_ref[...])
    out_ref[0] = h_ref[0] + _rms_norm(mix, gpost_ref[...])


def _merge(h3, gpost, w_o, outs, lses):
    batch, seq, _ = h3.shape
    row = pl.BlockSpec((1, ROW_TILE, D_MODEL), lambda b, i: (b, i, 0))
    specs, _ = _residue_major_specs(batch, seq, GROUP_WIDTH, _F32)
    return pl.pallas_call(
        _merge_kernel,
        grid=(batch, seq // ROW_TILE),
        in_specs=[row, _const_spec((1, D_MODEL)), _const_spec((GROUP_WIDTH, D_MODEL))] + specs * 2,
        out_specs=row,
        out_shape=jax.ShapeDtypeStruct(h3.shape, _F32),
        scratch_shapes=[pltpu.VMEM((4, PAIRS_PER_GROUP, ROW_TILE, LANES), _F32)],
        compiler_params=_params(("parallel", "parallel")),
        name="attn_merge",
    )(h3, gpost, w_o, *outs, *lses)


def kernel(x, positions, norm_gain, ffn_w_gate, ffn_w_up, ffn_w_down, pool_w_in, pool_w_group, pool_scale,
           pool_w_out, kv_norm_gain, w_k, w_v, attn_w_q, attn_w_o):
    batch, seq, _ = x.shape
    m = batch * seq
    gains = norm_gain.reshape(2, 6, 1, D_MODEL)
    bf = lambda w: w.astype(_BF16)

    def ffn(h, layer, slot):
        h = _ffn(h.reshape(m, D_MODEL), gains[layer, 4 * slot], gains[layer, 4 * slot + 1],
                 bf(ffn_w_gate[layer, slot]), bf(ffn_w_up[layer, slot]), bf(ffn_w_down[layer, slot]))
        return h.reshape(batch, seq, D_MODEL)

    pos = positions.astype(_F32).reshape(batch, seq, 1)
    inv_freq = ROPE_THETA ** (-jnp.arange(0, HEAD_DIM, 2, dtype=_F32) / HEAD_DIM)
    invf = jnp.tile(inv_freq, LANES // (HEAD_DIM // 2)).reshape(1, LANES)

    h = ffn(x, 0, 0)
    h = _pool(h, gains[0, 2], gains[0, 3], bf(pool_w_in[0]), bf(pool_w_group[0]),
              pool_scale[0].reshape(1, D_MODEL), bf(pool_w_out[0]))
    h = ffn(h, 0, 1)
    k_sh, v_sh = _shared_kv(h, kv_norm_gain.reshape(1, D_MODEL), pos, invf, bf(w_k), bf(w_v))
    h = ffn(h, 1, 0)
    q = _q_proj(h, gains[1, 2], pos, invf, bf(attn_w_q[0]))
    outs, lses = [], []
    for g in range(N_GROUPS):
        o, lse = _attn_group(q[g], k_sh[g], v_sh[g])
        outs.append(o)
        lses.append(lse)
    h = _merge(h, gains[1, 3], bf(attn_w_o[0]), outs, lses)
    return ffn(h, 1, 1)
```

```python
import functools

import jax
import jax.numpy as jnp
from jax import lax
from jax.experimental import pallas as pl
from jax.experimental.pallas import tpu as pltpu

D_MODEL = 1024
D_FF = 2816
POOL_WINDOWS = (2, 4, 8, 16)
POOL_GROUP_DIM = D_MODEL // len(POOL_WINDOWS)
POOL_HALO = 16
ATTN_DILATIONS = (1, 4, 16)
N_GROUPS = len(ATTN_DILATIONS)
ATTN_BLK = 128
HEAD_DIM = 64
HEADS_PER_GROUP = 8
GROUP_WIDTH = HEADS_PER_GROUP * HEAD_DIM
QKV_WIDTH = N_GROUPS * GROUP_WIDTH
ROPE_THETA = 10000.0
RMS_EPS = 1e-6
NEG_BIG = -1e30

LANES = 128
PAIRS_PER_GROUP = GROUP_WIDTH // LANES
QKV_TILES = QKV_WIDTH // LANES
ROW_TILE = 512
ATTN_Q_TILE = 512
FF_CHUNK = 256
VMEM_LIMIT_BYTES = 56 * 1024 * 1024

_F32 = jnp.float32
_BF16 = jnp.bfloat16


def _rms_norm(x, gain):
    ms = jnp.mean(x * x, axis=-1, keepdims=True)
    return x * lax.rsqrt(ms + RMS_EPS) * gain


def _dot(a, b):
    return jnp.dot(a, b, preferred_element_type=_F32)


def _const_spec(shape):
    return pl.BlockSpec(shape, lambda *_: (0,) * len(shape), pipeline_mode=pl.Buffered(1))


def _params(semantics):
    return pltpu.CompilerParams(dimension_semantics=semantics, vmem_limit_bytes=VMEM_LIMIT_BYTES)


def _ffn_body(x, gpre_ref, gpost_ref, wg_ref, wu_ref, wd_ref):
    xn = _rms_norm(x, gpre_ref[...]).astype(_BF16)
    acc = jnp.zeros(x.shape, _F32)
    for c in range(0, D_FF, FF_CHUNK):
        g = _dot(xn, wg_ref[:, c:c + FF_CHUNK])
        u = _dot(xn, wu_ref[:, c:c + FF_CHUNK])
        a = (g * jax.nn.sigmoid(g) * u).astype(_BF16)
        acc = acc + _dot(a, wd_ref[c:c + FF_CHUNK, :])
    return x + 0.5 * _rms_norm(acc, gpost_ref[...])


def _pool_body(x, j, gpre_ref, gpost_ref, win_ref, wgrp_ref, scale_ref, wout_ref, ubuf_ref):
    tm = x.shape[0]
    hm = _rms_norm(x, gpre_ref[...]).astype(_BF16)
    u = _dot(hm, win_ref[...])

    ubuf_ref[0:POOL_HALO, :] = jnp.where(j == 0, 0.0, ubuf_ref[0:POOL_HALO, :])
    ubuf_ref[POOL_HALO:, :] = u
    t = j * tm + lax.broadcasted_iota(jnp.int32, (tm, 1), 0)
    ys = []
    for g, w in enumerate(POOL_WINDOWS):
        lo = g * POOL_GROUP_DIM
        s = ubuf_ref[:, lo:lo + POOL_GROUP_DIM]
        k = 1
        while k < w:
            s = s + pltpu.roll(s, k, axis=0)
            k *= 2
        count = jnp.minimum(t + 1, w).astype(_F32)
        p = s[POOL_HALO:] / count - u[:, lo:lo + POOL_GROUP_DIM]
        ys.append(_dot(p.astype(_BF16), wgrp_ref[g]))
    y = jnp.concatenate(ys, axis=-1) * scale_ref[...]
    mix = _dot(y.astype(_BF16), wout_ref[...])
    ubuf_ref[0:POOL_HALO, :] = u[tm - POOL_HALO:, :]
    return x + _rms_norm(mix, gpost_ref[...])


def _rope_tables(pos, invf):
    ang = pos * invf
    c = jnp.cos(ang)
    s = jnp.sin(ang)
    lane = lax.broadcasted_iota(jnp.int32, ang.shape, 1)
    upper = (lane % HEAD_DIM) >= HEAD_DIM // 2
    s_from_lower = jnp.where(upper, s, 0.0)
    s_from_upper = jnp.where(upper, 0.0, -s)
    return c, s_from_lower, s_from_upper


def _rope_tile(xt, tables):
    c, s_lo, s_up = tables
    half = HEAD_DIM // 2
    return xt * c + pltpu.roll(xt, half, axis=1) * s_lo + pltpu.roll(xt, LANES - half, axis=1) * s_up


def _project_residue_major(xn, w_ref, slab_ref, out_refs, tables=None, scale=None):
    y = _dot(xn, w_ref[...])
    for j in range(QKV_TILES):
        yt = y[:, j * LANES:(j + 1) * LANES]
        if tables is not None:
            yt = _rope_tile(yt, tables)
        if scale is not None:
            yt = yt * scale
        slab_ref[j] = yt
    for g, d in enumerate(ATTN_DILATIONS):
        n = slab_ref.shape[1] // d
        for r in range(d):
            for p in range(PAIRS_PER_GROUP):
                rows = slab_ref[g * PAIRS_PER_GROUP + p, pl.ds(r, n, stride=d), :]
                out_refs[g][0, r, :, p * LANES:(p + 1) * LANES] = rows.astype(_BF16)


def _kv_body(h, pos, gain_ref, invf_ref, wk_ref, wv_ref, k_refs, v_refs, kslab_ref, vslab_ref):
    hk = _rms_norm(h, gain_ref[...]).astype(_BF16)
    tables = _rope_tables(pos, invf_ref[...])
    _project_residue_major(hk, wk_ref, kslab_ref, k_refs, tables=tables)
    _project_residue_major(hk, wv_ref, vslab_ref, v_refs)


def _q_body(h, pos, gain_ref, invf_ref, wq_ref, q_refs, qslab_ref):
    hm = _rms_norm(h, gain_ref[...]).astype(_BF16)
    tables = _rope_tables(pos, invf_ref[...])
    _project_residue_major(hm, wq_ref, qslab_ref, q_refs, tables=tables, scale=HEAD_DIM ** -0.5)


def _merge_body(h, gpost_ref, wo_ref, o_refs, l_refs, nat_ref):
    for a, (ref, d) in enumerate(((o_refs[1], ATTN_DILATIONS[1]), (l_refs[1], ATTN_DILATIONS[1]),
                                  (o_refs[2], ATTN_DILATIONS[2]), (l_refs[2], ATTN_DILATIONS[2]))):
        n = nat_ref.shape[2] // d
        for r in range(d):
            for p in range(PAIRS_PER_GROUP):
                nat_ref[a, p, pl.ds(r, n, stride=d), :] = ref[0, r, :, p * LANES:(p + 1) * LANES]

    def natural(a):
        return jnp.concatenate([nat_ref[a, p] for p in range(PAIRS_PER_GROUP)], axis=-1)

    o0, l0 = o_refs[0][0, 0], l_refs[0][0, 0]
    o1, l1, o2, l2 = natural(0), natural(1), natural(2), natural(3)
    top = jnp.maximum(jnp.maximum(l0, l1), l2)
    e0, e1, e2 = jnp.exp(l0 - top), jnp.exp(l1 - top), jnp.exp(l2 - top)
    o = (e0 * o0 + e1 * o1 + e2 * o2) / (e0 + e1 + e2)
    mix = _dot(o.astype(_BF16), wo_ref[...])
    return h + _rms_norm(mix, gpost_ref[...])


def _two_stage(n_tiles, first, second, init=None):
    i = pl.program_id(0)

    @pl.when(i == 0)
    def _():
        if init is not None:
            init()
        first()

    @pl.when(jnp.logical_and(i > 0, i < n_tiles))
    def _():
        second()
        first()

    @pl.when(i == n_tiles)
    def _():
        second()


def _cur_tile(n_tiles):
    return lambda i: jnp.minimum(i, n_tiles - 1)


def _prev_tile():
    return lambda i: jnp.maximum(i - 1, 0)


def _row_spec(tile_of):
    return pl.BlockSpec((ROW_TILE, D_MODEL), lambda i: (tile_of(i), 0))


def _residue_major_specs(batch, seq, dtype, tile_of):
    tiles_per_seq = seq // ROW_TILE
    specs, shapes = [], []
    for d in ATTN_DILATIONS:
        specs.append(pl.BlockSpec((1, d, ROW_TILE // d, GROUP_WIDTH),
                                  lambda i: (tile_of(i) // tiles_per_seq, 0, tile_of(i) % tiles_per_seq, 0)))
        shapes.append(jax.ShapeDtypeStruct((batch, d, seq // d, GROUP_WIDTH), dtype))
    return specs, shapes


def _ffn_weight_specs():
    return [_const_spec((1, D_MODEL)), _const_spec((1, D_MODEL)),
            _const_spec((D_MODEL, D_FF)), _const_spec((D_MODEL, D_FF)), _const_spec((D_FF, D_MODEL))]


_HANDOVER = pltpu.VMEM((2, ROW_TILE, D_MODEL), _F32)
_SLAB = pltpu.VMEM((QKV_TILES, ROW_TILE, LANES), _F32)


def _ffn_pool_kernel(n_tiles, tiles_per_seq, h_ref, *refs):
    ffn_w, pool_w, (o_ref, hand_ref, ubuf_ref) = refs[:5], refs[5:11], refs[11:]
    i = pl.program_id(0)

    def first():
        hand_ref[i % 2] = _ffn_body(h_ref[...], *ffn_w)

    def second():
        o_ref[...] = _pool_body(hand_ref[(i + 1) % 2], (i - 1) % tiles_per_seq, *pool_w, ubuf_ref)

    def init():
        ubuf_ref[0:POOL_HALO, :] = jnp.zeros((POOL_HALO, D_MODEL), _F32)

    _two_stage(n_tiles, first, second, init)


def _ffn_pool(h, ffn_w, pool_w, seq):
    m = h.shape[0]
    n_tiles = m // ROW_TILE
    n_grp = len(POOL_WINDOWS)
    return pl.pallas_call(
        functools.partial(_ffn_pool_kernel, n_tiles, seq // ROW_TILE),
        grid=(n_tiles + 1,),
        in_specs=[_row_spec(_cur_tile(n_tiles))] + _ffn_weight_specs() + [
            _const_spec((1, D_MODEL)), _const_spec((1, D_MODEL)), _const_spec((D_MODEL, D_MODEL)),
            _const_spec((n_grp, POOL_GROUP_DIM, POOL_GROUP_DIM)), _const_spec((1, D_MODEL)),
            _const_spec((D_MODEL, D_MODEL))],
        out_specs=_row_spec(_prev_tile()),
        out_shape=jax.ShapeDtypeStruct((m, D_MODEL), _F32),
        scratch_shapes=[_HANDOVER, pltpu.VMEM((POOL_HALO + ROW_TILE, D_MODEL), _F32)],
        compiler_params=_params(("arbitrary",)),
        name="ffn_pool",
    )(h, *ffn_w, *pool_w)


def _ffn_kv_kernel(n_tiles, h_ref, pos_ref, *refs):
    ffn_w, kv_w, outs, (hand_ref, kslab_ref, vslab_ref) = refs[:5], refs[5:9], refs[9:16], refs[16:]
    o_ref, k_refs, v_refs = outs[0], outs[1:4], outs[4:7]
    i = pl.program_id(0)

    def first():
        y = _ffn_body(h_ref[...], *ffn_w)
        o_ref[...] = y
        hand_ref[i % 2] = y

    def second():
        _kv_body(hand_ref[(i + 1) % 2], pos_ref[...], *kv_w, k_refs, v_refs, kslab_ref, vslab_ref)

    _two_stage(n_tiles, first, second)


def _ffn_kv(h, pos, ffn_w, kv_w, batch, seq):
    m = h.shape[0]
    n_tiles = m // ROW_TILE
    specs, shapes = _residue_major_specs(batch, seq, _BF16, _prev_tile())
    outs = pl.pallas_call(
        functools.partial(_ffn_kv_kernel, n_tiles),
        grid=(n_tiles + 1,),
        in_specs=[_row_spec(_cur_tile(n_tiles)), pl.BlockSpec((ROW_TILE, 1), lambda i: (jnp.maximum(i - 1, 0), 0))]
        + _ffn_weight_specs() + [_const_spec((1, D_MODEL)), _const_spec((1, LANES)),
                                 _const_spec((D_MODEL, QKV_WIDTH)), _const_spec((D_MODEL, QKV_WIDTH))],
        out_specs=[_row_spec(_cur_tile(n_tiles))] + specs * 2,
        out_shape=[jax.ShapeDtypeStruct((m, D_MODEL), _F32)] + shapes * 2,
        scratch_shapes=[_HANDOVER, _SLAB, _SLAB],
        compiler_params=_params(("arbitrary",)),
        name="ffn_kv",
    )(h, pos, *ffn_w, *kv_w)
    return outs[0], outs[1:4], outs[4:7]


def _ffn_q_kernel(n_tiles, h_ref, pos_ref, *refs):
    ffn_w, q_w, outs, (hand_ref, qslab_ref) = refs[:5], refs[5:8], refs[8:12], refs[12:]
    o_ref, q_refs = outs[0], outs[1:4]
    i = pl.program_id(0)

    def first():
        y = _ffn_body(h_ref[...], *ffn_w)
        o_ref[...] = y
        hand_ref[i % 2] = y

    def second():
        _q_body(hand_ref[(i + 1) % 2], pos_ref[...], *q_w, q_refs, qslab_ref)

    _two_stage(n_tiles, first, second)


def _ffn_q(h, pos, ffn_w, q_w, batch, seq):
    m = h.shape[0]
    n_tiles = m // ROW_TILE
    specs, shapes = _residue_major_specs(batch, seq, _BF16, _prev_tile())
    outs = pl.pallas_call(
        functools.partial(_ffn_q_kernel, n_tiles),
        grid=(n_tiles + 1,),
        in_specs=[_row_spec(_cur_tile(n_tiles)), pl.BlockSpec((ROW_TILE, 1), lambda i: (jnp.maximum(i - 1, 0), 0))]
        + _ffn_weight_specs() + [_const_spec((1, D_MODEL)), _const_spec((1, LANES)),
                                 _const_spec((D_MODEL, QKV_WIDTH))],
        out_specs=[_row_spec(_cur_tile(n_tiles))] + specs,
        out_shape=[jax.ShapeDtypeStruct((m, D_MODEL), _F32)] + shapes,
        scratch_shapes=[_HANDOVER, _SLAB],
        compiler_params=_params(("arbitrary",)),
        name="ffn_q",
    )(h, pos, *ffn_w, *q_w)
    return outs[0], outs[1:4]


def _merge_ffn_kernel(n_tiles, h_ref, *refs):
    merge_w, o_refs, l_refs, ffn_w, (out_ref, hand_ref, nat_ref) = (
        refs[:2], refs[2:5], refs[5:8], refs[8:13], refs[13:])
    i = pl.program_id(0)

    def first():
        hand_ref[i % 2] = _merge_body(h_ref[...], *merge_w, o_refs, l_refs, nat_ref)

    def second():
        out_ref[...] = _ffn_body(hand_ref[(i + 1) % 2], *ffn_w)

    _two_stage(n_tiles, first, second)


def _merge_ffn(h, merge_w, outs, lses, ffn_w, batch, seq):
    m = h.shape[0]
    n_tiles = m // ROW_TILE
    specs, _ = _residue_major_specs(batch, seq, _F32, _cur_tile(n_tiles))
    return pl.pallas_call(
        functools.partial(_merge_ffn_kernel, n_tiles),
        grid=(n_tiles + 1,),
        in_specs=[_row_spec(_cur_tile(n_tiles)), _const_spec((1, D_MODEL)), _const_spec((GROUP_WIDTH, D_MODEL))]
        + specs * 2 + _ffn_weight_specs(),
        out_specs=_row_spec(_prev_tile()),
        out_shape=jax.ShapeDtypeStruct((m, D_MODEL), _F32),
        scratch_shapes=[_HANDOVER, pltpu.VMEM((4, PAIRS_PER_GROUP, ROW_TILE, LANES), _F32)],
        compiler_params=_params(("arbitrary",)),
        name="merge_ffn",
    )(h, *merge_w, *outs, *lses, *ffn_w)


def _attn_kernel(q_ref, kp_ref, kc_ref, vp_ref, vc_ref, o_ref, lse_ref):
    step = pl.program_id(1)
    n_blk = ATTN_Q_TILE // ATTN_BLK
    qi = lax.broadcasted_iota(jnp.int32, (ATTN_BLK, 2 * ATTN_BLK), 0)
    kj = lax.broadcasted_iota(jnp.int32, (ATTN_BLK, 2 * ATTN_BLK), 1)
    band = (kj >= qi) & (kj <= qi + ATTN_BLK)
    band_first = (kj >= jnp.maximum(qi, jnp.where(step == 0, ATTN_BLK, 0))) & (kj <= qi + ATTN_BLK)
    lower_lanes = lax.broadcasted_iota(jnp.int32, (ATTN_BLK, LANES), 1) < HEAD_DIM
    nt = (((1,), (1,)), ((), ()))
    work = [(j, h) for j in range(n_blk) for h in range(HEADS_PER_GROUP)]

    def window(prev_ref, cur_ref, j, h):
        pair = slice((h // 2) * LANES, (h // 2 + 1) * LANES)
        if j == 0:
            return jnp.concatenate([prev_ref[0, :, pair], cur_ref[0, 0:ATTN_BLK, pair]], axis=0)
        return cur_ref[0, (j - 1) * ATTN_BLK:(j + 1) * ATTN_BLK, pair]

    def scores(j, h):
        pair = slice((h // 2) * LANES, (h // 2 + 1) * LANES)
        q = q_ref[0, j * ATTN_BLK:(j + 1) * ATTN_BLK, pair]
        q = jnp.where(lower_lanes if h % 2 == 0 else jnp.logical_not(lower_lanes), q, jnp.zeros_like(q))
        s = lax.dot_general(q, window(kp_ref, kc_ref, j, h), nt, preferred_element_type=_F32)
        return jnp.where(band_first if j == 0 else band, s, NEG_BIG)

    def finish(j, h, s):
        m = jnp.max(s, axis=-1, keepdims=True)
        p = jnp.exp(s - m)
        den = jnp.sum(p, axis=-1, keepdims=True)
        o = _dot(p.astype(_BF16), window(vp_ref, vc_ref, j, h)) * (1.0 / den)
        rows = slice(j * ATTN_BLK, (j + 1) * ATTN_BLK)
        cols = slice(h * HEAD_DIM, (h + 1) * HEAD_DIM)
        half = slice((h % 2) * HEAD_DIM, (h % 2 + 1) * HEAD_DIM)
        o_ref[0, rows, cols] = o[:, half]
        lse_ref[0, rows, cols] = jnp.broadcast_to(m + jnp.log(den), (ATTN_BLK, HEAD_DIM))

    ahead = 2
    pending = [scores(*work[n]) for n in range(ahead)]
    for n, (j, h) in enumerate(work):
        s = pending.pop(0)
        if n + ahead < len(work):
            pending.append(scores(*work[n + ahead]))
        finish(j, h, s)


def _attn_group(q, k, v):
    batch, d, rows, _ = q.shape
    n_seq = batch * d
    q, k, v = (a.reshape(n_seq, rows, GROUP_WIDTH) for a in (q, k, v))
    blocks_per_step = ATTN_Q_TILE // ATTN_BLK

    def cur(s, i):
        return (s, i, 0)

    def prev(s, i):
        return (s, jnp.maximum(i * blocks_per_step - 1, 0), 0)

    cur_spec = pl.BlockSpec((1, ATTN_Q_TILE, GROUP_WIDTH), cur)
    prev_spec = pl.BlockSpec((1, ATTN_BLK, GROUP_WIDTH), prev)
    out_shape = jax.ShapeDtypeStruct((n_seq, rows, GROUP_WIDTH), _F32)
    o, lse = pl.pallas_call(
        _attn_kernel,
        grid=(n_seq, rows // ATTN_Q_TILE),
        in_specs=[cur_spec, prev_spec, cur_spec, prev_spec, cur_spec],
        out_specs=[cur_spec, cur_spec],
        out_shape=[out_shape, out_shape],
        compiler_params=_params(("parallel", "arbitrary")),
        name=f"window_attn_d{d}",
    )(q, k, k, v, v)
    shape = (batch, d, rows, GROUP_WIDTH)
    return o.reshape(shape), lse.reshape(shape)


def kernel(x, positions, norm_gain, ffn_w_gate, ffn_w_up, ffn_w_down, pool_w_in, pool_w_group, pool_scale,
           pool_w_out, kv_norm_gain, w_k, w_v, attn_w_q, attn_w_o):
    batch, seq, _ = x.shape
    m = batch * seq
    gains = norm_gain.reshape(2, 6, 1, D_MODEL)
    bf = lambda w: w.astype(_BF16)

    def ffn_w(layer, slot):
        return (gains[layer, 4 * slot], gains[layer, 4 * slot + 1],
                bf(ffn_w_gate[layer, slot]), bf(ffn_w_up[layer, slot]), bf(ffn_w_down[layer, slot]))

    pos = positions.astype(_F32).reshape(m, 1)
    inv_freq = ROPE_THETA ** (-jnp.arange(0, HEAD_DIM, 2, dtype=_F32) / HEAD_DIM)
    invf = jnp.tile(inv_freq, LANES // (HEAD_DIM // 2)).reshape(1, LANES)

    h = x.reshape(m, D_MODEL)
    pool_w = (gains[0, 2], gains[0, 3], bf(pool_w_in[0]), bf(pool_w_group[0]),
              pool_scale[0].reshape(1, D_MODEL), bf(pool_w_out[0]))
    h = _ffn_pool(h, ffn_w(0, 0), pool_w, seq)
    kv_w = (kv_norm_gain.reshape(1, D_MODEL), invf, bf(w_k), bf(w_v))
    h, k_sh, v_sh = _ffn_kv(h, pos, ffn_w(0, 1), kv_w, batch, seq)
    h, q = _ffn_q(h, pos, ffn_w(1, 0), (gains[1, 2], invf, bf(attn_w_q[0])), batch, seq)
    outs, lses = [], []
    for g in range(N_GROUPS):
        o, lse = _attn_group(q[g], k_sh[g], v_sh[g])
        outs.append(o)
        lses.append(lse)
    h = _merge_ffn(h, (gains[1, 3], bf(attn_w_o[0])), outs, lses, ffn_w(1, 1), batch, seq)
    return h.reshape(batch, seq, D_MODEL)
```

```python
import functools

import jax
import jax.numpy as jnp
from jax import lax
from jax.experimental import pallas as pl
from jax.experimental.pallas import tpu as pltpu

D_MODEL = 1024
D_FF = 2816
POOL_WINDOWS = (2, 4, 8, 16)
POOL_GROUP_DIM = D_MODEL // len(POOL_WINDOWS)
POOL_HALO = 16
ATTN_DILATIONS = (1, 4, 16)
N_GROUPS = len(ATTN_DILATIONS)
ATTN_BLK = 128
HEAD_DIM = 64
HEADS_PER_GROUP = 8
GROUP_WIDTH = HEADS_PER_GROUP * HEAD_DIM
QKV_WIDTH = N_GROUPS * GROUP_WIDTH
ROPE_THETA = 10000.0
RMS_EPS = 1e-6
NEG_BIG = -1e30

LANES = 128
PAIRS_PER_GROUP = GROUP_WIDTH // LANES
QKV_TILES = QKV_WIDTH // LANES
ROW_TILE = 512
ATTN_Q_TILE = 512
FF_CHUNK = 256
VMEM_LIMIT_BYTES = 56 * 1024 * 1024

_F32 = jnp.float32
_BF16 = jnp.bfloat16


def _rms_norm(x, gain):
    ms = jnp.mean(x * x, axis=-1, keepdims=True)
    return x * lax.rsqrt(ms + RMS_EPS) * gain


def _dot(a, b):
    return jnp.dot(a, b, preferred_element_type=_F32)


def _const_spec(shape):
    return pl.BlockSpec(shape, lambda *_: (0,) * len(shape), pipeline_mode=pl.Buffered(1))


def _params(semantics):
    return pltpu.CompilerParams(dimension_semantics=semantics, vmem_limit_bytes=VMEM_LIMIT_BYTES)


def _ffn_steps(read, write, gpre_ref, gpost_ref, wg_ref, wu_ref, wd_ref, xn_ref, acc_ref):
    def start():
        xn_ref[...] = _rms_norm(read(), gpre_ref[...]).astype(_BF16)

    def chunk(c):
        def run():
            g = _dot(xn_ref[...], wg_ref[:, c:c + FF_CHUNK])
            u = _dot(xn_ref[...], wu_ref[:, c:c + FF_CHUNK])
            a = (g * jax.nn.sigmoid(g) * u).astype(_BF16)
            down = _dot(a, wd_ref[c:c + FF_CHUNK, :])
            if c == 0:
                acc_ref[...] = down
            else:
                acc_ref[...] += down
        return run

    def end():
        write(read() + 0.5 * _rms_norm(acc_ref[...], gpost_ref[...]))

    return [start] + [chunk(c) for c in range(0, D_FF, FF_CHUNK)] + [end]


def _pool_steps(read, write, j, gpre_ref, gpost_ref, win_ref, wgrp_ref, scale_ref, wout_ref, ubuf_ref):
    st = {"y": []}

    def start():
        st["x"] = read()
        hm = _rms_norm(st["x"], gpre_ref[...]).astype(_BF16)
        st["u"] = _dot(hm, win_ref[...])
        ubuf_ref[0:POOL_HALO, :] = jnp.where(j == 0, 0.0, ubuf_ref[0:POOL_HALO, :])
        ubuf_ref[POOL_HALO:, :] = st["u"]

    def group(g, w):
        def run():
            tm = st["x"].shape[0]
            cols = slice(g * POOL_GROUP_DIM, (g + 1) * POOL_GROUP_DIM)
            t = j * tm + lax.broadcasted_iota(jnp.int32, (tm, 1), 0)
            s = ubuf_ref[:, cols]
            k = 1
            while k < w:
                s = s + pltpu.roll(s, k, axis=0)
                k *= 2
            count = jnp.minimum(t + 1, w).astype(_F32)
            p = s[POOL_HALO:] / count - st["u"][:, cols]
            st["y"].append((_dot(p.astype(_BF16), wgrp_ref[g]) * scale_ref[:, cols]).astype(_BF16))
        return run

    def end():
        tm = st["x"].shape[0]
        mix = _dot(jnp.concatenate(st["y"], axis=-1), wout_ref[...])
        ubuf_ref[0:POOL_HALO, :] = st["u"][tm - POOL_HALO:, :]
        write(st["x"] + _rms_norm(mix, gpost_ref[...]))

    return [start] + [group(g, w) for g, w in enumerate(POOL_WINDOWS)] + [end]


def _rope_tables(c, s):
    lane = lax.broadcasted_iota(jnp.int32, c.shape, 1)
    upper = (lane % HEAD_DIM) >= HEAD_DIM // 2
    s_from_lower = jnp.where(upper, s, 0.0)
    s_from_upper = jnp.where(upper, 0.0, -s)
    return c, s_from_lower, s_from_upper


def _rope_tile(xt, tables):
    c, s_lo, s_up = tables
    half = HEAD_DIM // 2
    return xt * c + pltpu.roll(xt, half, axis=1) * s_lo + pltpu.roll(xt, LANES - half, axis=1) * s_up


def _projection_steps(st, w_ref, slab_ref, out_refs, rope=False, scale=None):
    def project(g):
        def run():
            y = _dot(st["xn"], w_ref[:, g * GROUP_WIDTH:(g + 1) * GROUP_WIDTH])
            for p in range(PAIRS_PER_GROUP):
                yt = y[:, p * LANES:(p + 1) * LANES]
                if rope:
                    yt = _rope_tile(yt, st["tables"])
                if scale is not None:
                    yt = yt * scale
                slab_ref[g * PAIRS_PER_GROUP + p] = yt
        return run

    def scatter(g, d):
        def run():
            n = slab_ref.shape[1] // d
            for r in range(d):
                for p in range(PAIRS_PER_GROUP):
                    rows = slab_ref[g * PAIRS_PER_GROUP + p, pl.ds(r, n, stride=d), :]
                    out_refs[g][0, r, :, p * LANES:(p + 1) * LANES] = rows.astype(_BF16)
        return run

    steps = []
    for g, d in enumerate(ATTN_DILATIONS):
        steps += [project(g), scatter(g, d)]
    return steps


def _kv_steps(read, pos_ref, gain_ref, invf_ref, wk_ref, wv_ref, k_refs, v_refs, cos_ref, sin_ref, slab_ref):
    st = {}

    def start():
        st["xn"] = _rms_norm(read(), gain_ref[...]).astype(_BF16)
        ang = pos_ref[...] * invf_ref[...]
        cos_ref[...] = jnp.cos(ang)
        sin_ref[...] = jnp.sin(ang)
        st["tables"] = _rope_tables(cos_ref[...], sin_ref[...])

    return ([start] + _projection_steps(st, wk_ref, slab_ref, k_refs, rope=True)
            + _projection_steps(st, wv_ref, slab_ref, v_refs))


def _q_steps(read, cos_ref, sin_ref, gain_ref, wq_ref, q_refs, slab_ref):
    st = {}

    def start():
        st["xn"] = _rms_norm(read(), gain_ref[...]).astype(_BF16)
        st["tables"] = _rope_tables(cos_ref[...], sin_ref[...])

    return [start] + _projection_steps(st, wq_ref, slab_ref, q_refs, rope=True, scale=HEAD_DIM ** -0.5)


def _merge_steps(read, write, gpost_ref, wo_ref, o_refs, l_refs, nat_ref):
    st = {"o": []}
    permuted = ((o_refs[1], ATTN_DILATIONS[1]), (l_refs[1], ATTN_DILATIONS[1]),
                (o_refs[2], ATTN_DILATIONS[2]), (l_refs[2], ATTN_DILATIONS[2]))

    def lane_tile(p):
        def run():
            cols = slice(p * LANES, (p + 1) * LANES)
            for a, (ref, d) in enumerate(permuted):
                n = nat_ref.shape[2] // d
                for r in range(d):
                    nat_ref[a, p, pl.ds(r, n, stride=d), :] = ref[0, r, :, cols]
            o0, l0 = o_refs[0][0, 0, :, cols], l_refs[0][0, 0, :, cols]
            o1, l1, o2, l2 = (nat_ref[a, p] for a in range(4))
            top = jnp.maximum(jnp.maximum(l0, l1), l2)
            e0, e1, e2 = jnp.exp(l0 - top), jnp.exp(l1 - top), jnp.exp(l2 - top)
            st["o"].append(((e0 * o0 + e1 * o1 + e2 * o2) / (e0 + e1 + e2)).astype(_BF16))
        return run

    def end():
        mix = _dot(jnp.concatenate(st["o"], axis=-1), wo_ref[...])
        write(read() + _rms_norm(mix, gpost_ref[...]))

    return [lane_tile(p) for p in range(PAIRS_PER_GROUP)] + [end]


def _run(steps):
    for step in steps:
        step()


FILL_LEAD = 1


def _interleave(main, fill):
    done = 0
    slots = len(main) - 1 - FILL_LEAD
    for k, step in enumerate(main):
        step()
        due = -(-(k + 1 - FILL_LEAD) * len(fill) // slots) if k >= FILL_LEAD else 0
        while done < min(due, len(fill)):
            fill[done]()
            done += 1


def _two_stage(n_tiles, first, second, second_is_main, init=None):
    i = pl.program_id(0)

    @pl.when(i == 0)
    def _():
        if init is not None:
            init()
        _run(first())

    @pl.when(jnp.logical_and(i > 0, i < n_tiles))
    def _():
        if second_is_main:
            _interleave(second(), first())
        else:
            _interleave(first(), second())

    @pl.when(i == n_tiles)
    def _():
        _run(second())


def _cur_tile(n_tiles):
    return lambda i: jnp.minimum(i, n_tiles - 1)


def _prev_tile():
    return lambda i: jnp.maximum(i - 1, 0)


def _row_spec(tile_of):
    return pl.BlockSpec((ROW_TILE, D_MODEL), lambda i: (tile_of(i), 0))


def _residue_major_specs(batch, seq, dtype, tile_of):
    tiles_per_seq = seq // ROW_TILE
    specs, shapes = [], []
    for d in ATTN_DILATIONS:
        specs.append(pl.BlockSpec((1, d, ROW_TILE // d, GROUP_WIDTH),
                                  lambda i: (tile_of(i) // tiles_per_seq, 0, tile_of(i) % tiles_per_seq, 0)))
        shapes.append(jax.ShapeDtypeStruct((batch, d, seq // d, GROUP_WIDTH), dtype))
    return specs, shapes


def _ffn_weight_specs():
    return [_const_spec((1, D_MODEL)), _const_spec((1, D_MODEL)),
            _const_spec((D_MODEL, D_FF)), _const_spec((D_MODEL, D_FF)), _const_spec((D_FF, D_MODEL))]


_HANDOVER = pltpu.VMEM((2, ROW_TILE, D_MODEL), _F32)
_SLAB = pltpu.VMEM((QKV_TILES, ROW_TILE, LANES), _F32)
_FFN_WORK = [pltpu.VMEM((ROW_TILE, D_MODEL), _BF16), pltpu.VMEM((ROW_TILE, D_MODEL), _F32)]
BF16_SUBLANES = 16


def _hand_over(hand_ref):
    i = pl.program_id(0)

    def put(y):
        hand_ref[i % 2] = y

    def get():
        return hand_ref[(i + 1) % 2]

    return put, get


def _split(refs, *counts):
    out, at = [], 0
    for n in counts:
        out.append(refs[at:at + n])
        at += n
    assert at == len(refs)
    return out


def _cast_specs(weights, n_tiles):
    in_specs, out_specs, out_shapes = [], [], []
    for w, lead in weights:
        rows, cols = w.shape[len(lead):]
        n_blocks = n_tiles if (rows // n_tiles) % BF16_SUBLANES == 0 else n_tiles // 2
        assert rows % n_blocks == 0 and (rows // n_blocks) % BF16_SUBLANES == 0

        def block(i, n_blocks=n_blocks):
            return (jnp.minimum(i, n_blocks - 1), 0)

        in_specs.append(pl.BlockSpec((None,) * len(lead) + (rows // n_blocks, cols),
                                     lambda i, lead=lead, block=block: lead + block(i)))
        out_specs.append(pl.BlockSpec((rows // n_blocks, cols), block))
        out_shapes.append(jax.ShapeDtypeStruct((rows, cols), _BF16))
    return in_specs, out_specs, out_shapes


def _cast_blocks(src_refs, dst_refs):
    for src, dst in zip(src_refs, dst_refs):
        dst[...] = src[...].astype(_BF16)


def _ffn_pool_kernel(n_tiles, tiles_per_seq, n_cast, h_ref, *refs):
    ffn_w, pool_w, cast_src, (o_ref,), cast_dst, (hand_ref, ubuf_ref, *work) = _split(
        refs, 5, 6, n_cast, 1, n_cast, 4)
    put, get = _hand_over(hand_ref)
    j = (pl.program_id(0) - 1) % tiles_per_seq

    def write(y):
        o_ref[...] = y

    def init():
        ubuf_ref[0:POOL_HALO, :] = jnp.zeros((POOL_HALO, D_MODEL), _F32)

    _cast_blocks(cast_src, cast_dst)
    _two_stage(n_tiles,
               lambda: _ffn_steps(lambda: h_ref[...], put, *ffn_w, *work),
               lambda: _pool_steps(get, write, j, *pool_w, ubuf_ref),
               second_is_main=False, init=init)


def _ffn_pool(h, ffn_w, pool_w, seq, to_cast):
    m = h.shape[0]
    n_tiles = m // ROW_TILE
    n_grp = len(POOL_WINDOWS)
    cast_in, cast_out, cast_shapes = _cast_specs(to_cast, n_tiles)
    outs = pl.pallas_call(
        functools.partial(_ffn_pool_kernel, n_tiles, seq // ROW_TILE, len(to_cast)),
        grid=(n_tiles + 1,),
        in_specs=[_row_spec(_cur_tile(n_tiles))] + _ffn_weight_specs() + [
            _const_spec((1, D_MODEL)), _const_spec((1, D_MODEL)), _const_spec((D_MODEL, D_MODEL)),
            _const_spec((n_grp, POOL_GROUP_DIM, POOL_GROUP_DIM)), _const_spec((1, D_MODEL)),
            _const_spec((D_MODEL, D_MODEL))] + cast_in,
        out_specs=[_row_spec(_prev_tile())] + cast_out,
        out_shape=[jax.ShapeDtypeStruct((m, D_MODEL), _F32)] + cast_shapes,
        scratch_shapes=[_HANDOVER, pltpu.VMEM((POOL_HALO + ROW_TILE, D_MODEL), _F32)] + _FFN_WORK,
        compiler_params=_params(("arbitrary",)),
        name="ffn_pool",
    )(h, *ffn_w, *pool_w, *(w for w, _ in to_cast))
    return outs[0], outs[1:]


def _ffn_and_keep(h_ref, o_ref, put, ffn_w, work):
    def write(y):
        o_ref[...] = y
        put(y)

    return lambda: _ffn_steps(lambda: h_ref[...], write, *ffn_w, *work)


def _table_spec():
    return pl.BlockSpec((ROW_TILE, LANES), lambda i: (jnp.maximum(i - 1, 0), 0))


def _ffn_kv_kernel(n_tiles, n_cast, h_ref, pos_ref, *refs):
    ffn_w, kv_w, cast_src, (o_ref,), k_refs, v_refs, (cos_ref, sin_ref), cast_dst, (hand_ref, slab_ref, *work) = (
        _split(refs, 5, 4, n_cast, 1, N_GROUPS, N_GROUPS, 2, n_cast, 4))
    put, get = _hand_over(hand_ref)
    _cast_blocks(cast_src, cast_dst)
    _two_stage(n_tiles, _ffn_and_keep(h_ref, o_ref, put, ffn_w, work),
               lambda: _kv_steps(get, pos_ref, *kv_w, k_refs, v_refs, cos_ref, sin_ref, slab_ref),
               second_is_main=False)


def _ffn_kv(h, pos, ffn_w, kv_w, batch, seq, to_cast):
    m = h.shape[0]
    n_tiles = m // ROW_TILE
    specs, shapes = _residue_major_specs(batch, seq, _BF16, _prev_tile())
    cast_in, cast_out, cast_shapes = _cast_specs(to_cast, n_tiles)
    table_shape = jax.ShapeDtypeStruct((m, LANES), _F32)
    outs = pl.pallas_call(
        functools.partial(_ffn_kv_kernel, n_tiles, len(to_cast)),
        grid=(n_tiles + 1,),
        in_specs=[_row_spec(_cur_tile(n_tiles)), pl.BlockSpec((ROW_TILE, 1), lambda i: (jnp.maximum(i - 1, 0), 0))]
        + _ffn_weight_specs() + [_const_spec((1, D_MODEL)), _const_spec((1, LANES)),
                                 _const_spec((D_MODEL, QKV_WIDTH)), _const_spec((D_MODEL, QKV_WIDTH))] + cast_in,
        out_specs=[_row_spec(_cur_tile(n_tiles))] + specs * 2 + [_table_spec()] * 2 + cast_out,
        out_shape=[jax.ShapeDtypeStruct((m, D_MODEL), _F32)] + shapes * 2 + [table_shape] * 2 + cast_shapes,
        scratch_shapes=[_HANDOVER, _SLAB] + _FFN_WORK,
        compiler_params=_params(("arbitrary",)),
        name="ffn_kv",
    )(h, pos, *ffn_w, *kv_w, *(w for w, _ in to_cast))
    return outs[0], outs[1:4], outs[4:7], outs[7:9], outs[9:]


def _ffn_q_kernel(n_tiles, n_cast, h_ref, cos_ref, sin_ref, *refs):
    ffn_w, q_w, cast_src, (o_ref,), q_refs, cast_dst, (hand_ref, slab_ref, *work) = _split(
        refs, 5, 2, n_cast, 1, N_GROUPS, n_cast, 4)
    put, get = _hand_over(hand_ref)
    _cast_blocks(cast_src, cast_dst)
    _two_stage(n_tiles, _ffn_and_keep(h_ref, o_ref, put, ffn_w, work),
               lambda: _q_steps(get, cos_ref, sin_ref, *q_w, q_refs, slab_ref),
               second_is_main=False)


def _ffn_q(h, tables, ffn_w, q_w, batch, seq, to_cast):
    m = h.shape[0]
    n_tiles = m // ROW_TILE
    specs, shapes = _residue_major_specs(batch, seq, _BF16, _prev_tile())
    cast_in, cast_out, cast_shapes = _cast_specs(to_cast, n_tiles)
    outs = pl.pallas_call(
        functools.partial(_ffn_q_kernel, n_tiles, len(to_cast)),
        grid=(n_tiles + 1,),
        in_specs=[_row_spec(_cur_tile(n_tiles))] + [_table_spec()] * 2 + _ffn_weight_specs()
        + [_const_spec((1, D_MODEL)), _const_spec((D_MODEL, QKV_WIDTH))] + cast_in,
        out_specs=[_row_spec(_cur_tile(n_tiles))] + specs + cast_out,
        out_shape=[jax.ShapeDtypeStruct((m, D_MODEL), _F32)] + shapes + cast_shapes,
        scratch_shapes=[_HANDOVER, _SLAB] + _FFN_WORK,
        compiler_params=_params(("arbitrary",)),
        name="ffn_q",
    )(h, *tables, *ffn_w, *q_w, *(w for w, _ in to_cast))
    return outs[0], outs[1:4], outs[4:]


def _merge_ffn_kernel(n_tiles, h_ref, *refs):
    merge_w, o_refs, l_refs, ffn_w, (out_ref, hand_ref, nat_ref, *work) = (
        refs[:2], refs[2:5], refs[5:8], refs[8:13], refs[13:])
    put, get = _hand_over(hand_ref)

    def write(y):
        out_ref[...] = y

    _two_stage(n_tiles,
               lambda: _merge_steps(lambda: h_ref[...], put, *merge_w, o_refs, l_refs, nat_ref),
               lambda: _ffn_steps(get, write, *ffn_w, *work),
               second_is_main=True)


def _merge_ffn(h, merge_w, outs, lses, ffn_w, batch, seq):
    m = h.shape[0]
    n_tiles = m // ROW_TILE
    specs, _ = _residue_major_specs(batch, seq, _F32, _cur_tile(n_tiles))
    return pl.pallas_call(
        functools.partial(_merge_ffn_kernel, n_tiles),
        grid=(n_tiles + 1,),
        in_specs=[_row_spec(_cur_tile(n_tiles)), _const_spec((1, D_MODEL)), _const_spec((GROUP_WIDTH, D_MODEL))]
        + specs * 2 + _ffn_weight_specs(),
        out_specs=_row_spec(_prev_tile()),
        out_shape=jax.ShapeDtypeStruct((m, D_MODEL), _F32),
        scratch_shapes=[_HANDOVER, pltpu.VMEM((4, PAIRS_PER_GROUP, ROW_TILE, LANES), _F32)] + _FFN_WORK,
        compiler_params=_params(("arbitrary",)),
        name="merge_ffn",
    )(h, *merge_w, *outs, *lses, *ffn_w)


def _attn_kernel(q_ref, kp_ref, kc_ref, vp_ref, vc_ref, o_ref, lse_ref):
    step = pl.program_id(1)
    n_blk = ATTN_Q_TILE // ATTN_BLK
    qi = lax.broadcasted_iota(jnp.int32, (ATTN_BLK, 2 * ATTN_BLK), 0)
    kj = lax.broadcasted_iota(jnp.int32, (ATTN_BLK, 2 * ATTN_BLK), 1)
    band = (kj >= qi) & (kj <= qi + ATTN_BLK)
    band_first = (kj >= jnp.maximum(qi, jnp.where(step == 0, ATTN_BLK, 0))) & (kj <= qi + ATTN_BLK)
    lower_lanes = lax.broadcasted_iota(jnp.int32, (ATTN_BLK, LANES), 1) < HEAD_DIM
    nt = (((1,), (1,)), ((), ()))
    work = [(j, h) for j in range(n_blk) for h in range(HEADS_PER_GROUP)]

    def window(prev_ref, cur_ref, j, h):
        pair = slice((h // 2) * LANES, (h // 2 + 1) * LANES)
        if j == 0:
            return jnp.concatenate([prev_ref[0, :, pair], cur_ref[0, 0:ATTN_BLK, pair]], axis=0)
        return cur_ref[0, (j - 1) * ATTN_BLK:(j + 1) * ATTN_BLK, pair]

    def scores(j, h):
        pair = slice((h // 2) * LANES, (h // 2 + 1) * LANES)
        q = q_ref[0, j * ATTN_BLK:(j + 1) * ATTN_BLK, pair]
        q = jnp.where(lower_lanes if h % 2 == 0 else jnp.logical_not(lower_lanes), q, jnp.zeros_like(q))
        s = lax.dot_general(q, window(kp_ref, kc_ref, j, h), nt, preferred_element_type=_F32)
        return jnp.where(band_first if j == 0 else band, s, NEG_BIG)

    def finish(j, h, s):
        m = jnp.max(s, axis=-1, keepdims=True)
        p = jnp.exp(s - m)
        den = jnp.sum(p, axis=-1, keepdims=True)
        o = _dot(p.astype(_BF16), window(vp_ref, vc_ref, j, h)) * (1.0 / den)
        rows = slice(j * ATTN_BLK, (j + 1) * ATTN_BLK)
        cols = slice(h * HEAD_DIM, (h + 1) * HEAD_DIM)
        half = slice((h % 2) * HEAD_DIM, (h % 2 + 1) * HEAD_DIM)
        o_ref[0, rows, cols] = o[:, half]
        lse_ref[0, rows, cols] = jnp.broadcast_to(m + jnp.log(den), (ATTN_BLK, HEAD_DIM))

    ahead = 2
    pending = [scores(*work[n]) for n in range(ahead)]
    for n, (j, h) in enumerate(work):
        s = pending.pop(0)
        if n + ahead < len(work):
            pending.append(scores(*work[n + ahead]))
        finish(j, h, s)


def _attn_group(q, k, v):
    batch, d, rows, _ = q.shape
    n_seq = batch * d
    q, k, v = (a.reshape(n_seq, rows, GROUP_WIDTH) for a in (q, k, v))
    blocks_per_step = ATTN_Q_TILE // ATTN_BLK

    def cur(s, i):
        return (s, i, 0)

    def prev(s, i):
        return (s, jnp.maximum(i * blocks_per_step - 1, 0), 0)

    cur_spec = pl.BlockSpec((1, ATTN_Q_TILE, GROUP_WIDTH), cur)
    prev_spec = pl.BlockSpec((1, ATTN_BLK, GROUP_WIDTH), prev)
    out_shape = jax.ShapeDtypeStruct((n_seq, rows, GROUP_WIDTH), _F32)
    o, lse = pl.pallas_call(
        _attn_kernel,
        grid=(n_seq, rows // ATTN_Q_TILE),
        in_specs=[cur_spec, prev_spec, cur_spec, prev_spec, cur_spec],
        out_specs=[cur_spec, cur_spec],
        out_shape=[out_shape, out_shape],
        compiler_params=_params(("parallel", "arbitrary")),
        name=f"window_attn_d{d}",
    )(q, k, k, v, v)
    shape = (batch, d, rows, GROUP_WIDTH)
    return o.reshape(shape), lse.reshape(shape)


def kernel(x, positions, norm_gain, ffn_w_gate, ffn_w_up, ffn_w_down, pool_w_in, pool_w_group, pool_scale,
           pool_w_out, kv_norm_gain, w_k, w_v, attn_w_q, attn_w_o):
    batch, seq, _ = x.shape
    m = batch * seq
    gains = norm_gain.reshape(2, 6, 1, D_MODEL)
    bf = lambda w: w.astype(_BF16)

    def ffn_f32(layer, slot):
        return tuple((w, (layer, slot)) for w in (ffn_w_gate, ffn_w_up, ffn_w_down))

    def ffn_gains(layer, slot):
        return (gains[layer, 4 * slot], gains[layer, 4 * slot + 1])

    pos = positions.astype(_F32).reshape(m, 1)
    inv_freq = ROPE_THETA ** (-jnp.arange(0, HEAD_DIM, 2, dtype=_F32) / HEAD_DIM)
    invf = jnp.tile(inv_freq, LANES // (HEAD_DIM // 2)).reshape(1, LANES)

    h = x.reshape(m, D_MODEL)
    pool_w = (gains[0, 2], gains[0, 3], bf(pool_w_in[0]), bf(pool_w_group[0]),
              pool_scale[0].reshape(1, D_MODEL), bf(pool_w_out[0]))
    ffn00 = tuple(bf(w[lead]) for w, lead in ffn_f32(0, 0))
    h, (*ffn01, wk_bf, wv_bf) = _ffn_pool(h, ffn_gains(0, 0) + ffn00, pool_w, seq,
                                          to_cast=ffn_f32(0, 1) + ((w_k, ()), (w_v, ())))
    kv_w = (kv_norm_gain.reshape(1, D_MODEL), invf, wk_bf, wv_bf)
    h, k_sh, v_sh, tables, (*ffn10, wq_bf) = _ffn_kv(h, pos, ffn_gains(0, 1) + tuple(ffn01), kv_w, batch, seq,
                                                     to_cast=ffn_f32(1, 0) + ((attn_w_q, (0,)),))
    h, q, (*ffn11, wo_bf) = _ffn_q(h, tables, ffn_gains(1, 0) + tuple(ffn10), (gains[1, 2], wq_bf), batch, seq,
                                   to_cast=ffn_f32(1, 1) + ((attn_w_o, (0,)),))
    outs, lses = [], []
    for g in range(N_GROUPS):
        o, lse = _attn_group(q[g], k_sh[g], v_sh[g])
        outs.append(o)
        lses.append(lse)
    h = _merge_ffn(h, (gains[1, 3], wo_bf), outs, lses, ffn_gains(1, 1) + tuple(ffn11), batch, seq)
    return h.reshape(batch, seq, D_MODEL)
```

```python
import functools

import jax
import jax.numpy as jnp
from jax import lax
from jax.experimental import pallas as pl
from jax.experimental.pallas import tpu as pltpu

D_MODEL = 1024
D_FF = 2816
POOL_WINDOWS = (2, 4, 8, 16)
POOL_GROUP_DIM = D_MODEL // len(POOL_WINDOWS)
POOL_HALO = 16
ATTN_DILATIONS = (1, 4, 16)
N_GROUPS = len(ATTN_DILATIONS)
ATTN_BLK = 128
HEAD_DIM = 64
HEADS_PER_GROUP = 8
GROUP_WIDTH = HEADS_PER_GROUP * HEAD_DIM
QKV_WIDTH = N_GROUPS * GROUP_WIDTH
ROPE_THETA = 10000.0
RMS_EPS = 1e-6
NEG_BIG = -1e30

LANES = 128
PAIRS_PER_GROUP = GROUP_WIDTH // LANES
QKV_TILES = QKV_WIDTH // LANES
ROW_TILE = 512
ATTN_Q_TILE = 512
FF_CHUNK = 256
VMEM_LIMIT_BYTES = 56 * 1024 * 1024

_F32 = jnp.float32
_BF16 = jnp.bfloat16


def _rms_norm(x, gain):
    ms = jnp.mean(x * x, axis=-1, keepdims=True)
    return x * lax.rsqrt(ms + RMS_EPS) * gain


def _dot(a, b):
    return jnp.dot(a, b, preferred_element_type=_F32)


def _const_spec(shape):
    return pl.BlockSpec(shape, lambda *_: (0,) * len(shape), pipeline_mode=pl.Buffered(1))


def _params(semantics):
    return pltpu.CompilerParams(dimension_semantics=semantics, vmem_limit_bytes=VMEM_LIMIT_BYTES)


def _ffn_steps(read, write, gpre_ref, gpost_ref, wg_ref, wu_ref, wd_ref, xn_ref, acc_ref):
    def start():
        xn_ref[...] = _rms_norm(read(), gpre_ref[...]).astype(_BF16)

    def chunk(c):
        def run():
            g = _dot(xn_ref[...], wg_ref[:, c:c + FF_CHUNK])
            u = _dot(xn_ref[...], wu_ref[:, c:c + FF_CHUNK])
            a = (g * jax.nn.sigmoid(g) * u).astype(_BF16)
            down = _dot(a, wd_ref[c:c + FF_CHUNK, :])
            if c == 0:
                acc_ref[...] = down
            else:
                acc_ref[...] += down
        return run

    def end():
        write(read() + 0.5 * _rms_norm(acc_ref[...], gpost_ref[...]))

    return [start] + [chunk(c) for c in range(0, D_FF, FF_CHUNK)] + [end]


def _pool_steps(read, write, j, gpre_ref, gpost_ref, win_ref, wgrp_ref, scale_ref, wout_ref, ubuf_ref):
    st = {"y": []}

    def start():
        st["x"] = read()
        hm = _rms_norm(st["x"], gpre_ref[...]).astype(_BF16)
        st["u"] = _dot(hm, win_ref[...])
        ubuf_ref[0:POOL_HALO, :] = jnp.where(j == 0, 0.0, ubuf_ref[0:POOL_HALO, :])
        ubuf_ref[POOL_HALO:, :] = st["u"]

    def group(g, w):
        def run():
            tm = st["x"].shape[0]
            cols = slice(g * POOL_GROUP_DIM, (g + 1) * POOL_GROUP_DIM)
            t = j * tm + lax.broadcasted_iota(jnp.int32, (tm, 1), 0)
            s = ubuf_ref[:, cols]
            k = 1
            while k < w:
                s = s + pltpu.roll(s, k, axis=0)
                k *= 2
            count = jnp.minimum(t + 1, w).astype(_F32)
            p = s[POOL_HALO:] / count - st["u"][:, cols]
            st["y"].append((_dot(p.astype(_BF16), wgrp_ref[g]) * scale_ref[:, cols]).astype(_BF16))
        return run

    def end():
        tm = st["x"].shape[0]
        mix = _dot(jnp.concatenate(st["y"], axis=-1), wout_ref[...])
        ubuf_ref[0:POOL_HALO, :] = st["u"][tm - POOL_HALO:, :]
        write(st["x"] + _rms_norm(mix, gpost_ref[...]))

    return [start] + [group(g, w) for g, w in enumerate(POOL_WINDOWS)] + [end]


def _rope_tables(c, s):
    lane = lax.broadcasted_iota(jnp.int32, c.shape, 1)
    return c, jnp.where(lane < LANES // 2, -s, s)


def _rope_tile(xt, tables):
    c, s_signed = tables
    return xt * c + pltpu.roll(xt, LANES // 2, axis=1) * s_signed


def _pair_interleaved(t):
    half = HEAD_DIM // 2
    block = lax.broadcasted_iota(jnp.int32, t.shape, 1) // half
    return jnp.where(block == 1, pltpu.roll(t, LANES - half, axis=1),
                     jnp.where(block == 2, pltpu.roll(t, half, axis=1), t))


def _projection_steps(st, xn_ref, w_ref, slab_ref, out_refs, rope=False, scale=None):
    def project(g):
        def run():
            y = _dot(xn_ref[...], w_ref[:, g * GROUP_WIDTH:(g + 1) * GROUP_WIDTH])
            for p in range(PAIRS_PER_GROUP):
                yt = y[:, p * LANES:(p + 1) * LANES]
                if rope:
                    yt = _rope_tile(yt, st["tables"])
                if scale is not None:
                    yt = yt * scale
                slab_ref[g * PAIRS_PER_GROUP + p] = yt
        return run

    def scatter(g, d):
        def run():
            n = slab_ref.shape[1] // d
            for r in range(d):
                for p in range(PAIRS_PER_GROUP):
                    rows = slab_ref[g * PAIRS_PER_GROUP + p, pl.ds(r, n, stride=d), :]
                    out_refs[g][0, r, :, p * LANES:(p + 1) * LANES] = rows.astype(_BF16)
        return run

    steps = []
    for g, d in enumerate(ATTN_DILATIONS):
        steps += [project(g), scatter(g, d)]
    return steps


def _kv_steps(read, pos_ref, gain_ref, invf_ref, wk_ref, wv_ref, k_refs, v_refs, cos_ref, sin_ref, slab_ref, xn_ref):
    st = {}

    def start():
        xn_ref[...] = _rms_norm(read(), gain_ref[...]).astype(_BF16)
        ang = pos_ref[...] * invf_ref[...]
        cos_ref[...] = jnp.cos(ang)
        sin_ref[...] = jnp.sin(ang)
        st["tables"] = _rope_tables(cos_ref[...], sin_ref[...])

    return ([start] + _projection_steps(st, xn_ref, wk_ref, slab_ref, k_refs, rope=True)
            + _projection_steps(st, xn_ref, wv_ref, slab_ref, v_refs))


def _q_steps(read, cos_ref, sin_ref, gain_ref, wq_ref, q_refs, slab_ref, xn_ref):
    st = {}

    def start():
        xn_ref[...] = _rms_norm(read(), gain_ref[...]).astype(_BF16)
        st["tables"] = _rope_tables(cos_ref[...], sin_ref[...])

    return [start] + _projection_steps(st, xn_ref, wq_ref, slab_ref, q_refs, rope=True, scale=HEAD_DIM ** -0.5)


def _merge_steps(read, write, gpost_ref, wo_ref, o_refs, l_refs, nat_ref):
    st = {"o": []}
    permuted = ((o_refs[1], ATTN_DILATIONS[1]), (l_refs[1], ATTN_DILATIONS[1]),
                (o_refs[2], ATTN_DILATIONS[2]), (l_refs[2], ATTN_DILATIONS[2]))

    def lane_tile(p):
        def run():
            cols = slice(p * LANES, (p + 1) * LANES)
            for a, (ref, d) in enumerate(permuted):
                n = nat_ref.shape[2] // d
                for r in range(d):
                    nat_ref[a, p, pl.ds(r, n, stride=d), :] = ref[0, r, :, cols]
            o0, l0 = o_refs[0][0, 0, :, cols], l_refs[0][0, 0, :, cols]
            o1, l1, o2, l2 = (nat_ref[a, p] for a in range(4))
            top = jnp.maximum(jnp.maximum(l0, l1), l2)
            e0, e1, e2 = jnp.exp(l0 - top), jnp.exp(l1 - top), jnp.exp(l2 - top)
            st["o"].append(((e0 * o0 + e1 * o1 + e2 * o2) / (e0 + e1 + e2)).astype(_BF16))
        return run

    def end():
        mix = _dot(jnp.concatenate(st["o"], axis=-1), wo_ref[...])
        write(read() + _rms_norm(mix, gpost_ref[...]))

    return [lane_tile(p) for p in range(PAIRS_PER_GROUP)] + [end]


def _run(steps):
    for step in steps:
        step()


FILL_LEAD = 1


def _interleave(main, fill):
    done = 0
    slots = len(main) - 1 - FILL_LEAD
    for k, step in enumerate(main):
        step()
        due = -(-(k + 1 - FILL_LEAD) * len(fill) // slots) if k >= FILL_LEAD else 0
        while done < min(due, len(fill)):
            fill[done]()
            done += 1


def _two_stage(n_tiles, first, second, second_is_main, init=None):
    i = pl.program_id(0)

    @pl.when(i == 0)
    def _():
        if init is not None:
            init()
        _run(first())

    @pl.when(jnp.logical_and(i > 0, i < n_tiles))
    def _():
        if second_is_main:
            _interleave(second(), first())
        else:
            _interleave(first(), second())

    @pl.when(i == n_tiles)
    def _():
        _run(second())


def _cur_tile(n_tiles):
    return lambda i: jnp.minimum(i, n_tiles - 1)


def _prev_tile():
    return lambda i: jnp.maximum(i - 1, 0)


def _row_spec(tile_of):
    return pl.BlockSpec((ROW_TILE, D_MODEL), lambda i: (tile_of(i), 0))


def _residue_major_specs(batch, seq, dtype, tile_of):
    tiles_per_seq = seq // ROW_TILE
    specs, shapes = [], []
    for d in ATTN_DILATIONS:
        specs.append(pl.BlockSpec((1, d, ROW_TILE // d, GROUP_WIDTH),
                                  lambda i: (tile_of(i) // tiles_per_seq, 0, tile_of(i) % tiles_per_seq, 0)))
        shapes.append(jax.ShapeDtypeStruct((batch, d, seq // d, GROUP_WIDTH), dtype))
    return specs, shapes


def _ffn_weight_specs():
    return [_const_spec((1, D_MODEL)), _const_spec((1, D_MODEL)),
            _const_spec((D_MODEL, D_FF)), _const_spec((D_MODEL, D_FF)), _const_spec((D_FF, D_MODEL))]


_HANDOVER = pltpu.VMEM((2, ROW_TILE, D_MODEL), _F32)
_SLAB = pltpu.VMEM((QKV_TILES, ROW_TILE, LANES), _F32)
_FFN_WORK = [pltpu.VMEM((ROW_TILE, D_MODEL), _BF16), pltpu.VMEM((ROW_TILE, D_MODEL), _F32)]
BF16_SUBLANES = 16


def _hand_over(hand_ref):
    i = pl.program_id(0)

    def put(y):
        hand_ref[i % 2] = y

    def get():
        return hand_ref[(i + 1) % 2]

    return put, get


def _split(refs, *counts):
    out, at = [], 0
    for n in counts:
        out.append(refs[at:at + n])
        at += n
    assert at == len(refs)
    return out


def _cast_specs(weights, n_tiles):
    in_specs, out_specs, out_shapes = [], [], []
    for w, lead, _ in weights:
        rows, cols = w.shape[len(lead):]
        n_blocks = max(n for n in range(1, n_tiles + 1) if rows % (n * BF16_SUBLANES) == 0)

        def block(i, n_blocks=n_blocks):
            return (jnp.minimum(i, n_blocks - 1), 0)

        in_specs.append(pl.BlockSpec((None,) * len(lead) + (rows // n_blocks, cols),
                                     lambda i, lead=lead, block=block: lead + block(i)))
        out_specs.append(pl.BlockSpec((rows // n_blocks, cols), block))
        out_shapes.append(jax.ShapeDtypeStruct((rows, cols), _BF16))
    return in_specs, out_specs, out_shapes


def _cast_blocks(src_refs, dst_refs, interleave):
    for src, dst, flag in zip(src_refs, dst_refs, interleave):
        if flag:
            for j in range(src.shape[1] // LANES):
                cols = slice(j * LANES, (j + 1) * LANES)
                dst[:, cols] = _pair_interleaved(src[:, cols]).astype(_BF16)
        else:
            dst[...] = src[...].astype(_BF16)


def _ffn_pool_kernel(n_tiles, tiles_per_seq, interleave, h_ref, *refs):
    n_cast = len(interleave)
    ffn_w, pool_w, cast_src, (o_ref,), cast_dst, (hand_ref, ubuf_ref, *work) = _split(
        refs, 5, 6, n_cast, 1, n_cast, 4)
    put, get = _hand_over(hand_ref)
    j = (pl.program_id(0) - 1) % tiles_per_seq

    def write(y):
        o_ref[...] = y

    def init():
        ubuf_ref[0:POOL_HALO, :] = jnp.zeros((POOL_HALO, D_MODEL), _F32)

    _cast_blocks(cast_src, cast_dst, interleave)
    _two_stage(n_tiles,
               lambda: _ffn_steps(lambda: h_ref[...], put, *ffn_w, *work),
               lambda: _pool_steps(get, write, j, *pool_w, ubuf_ref),
               second_is_main=False, init=init)


def _ffn_pool(h, ffn_w, pool_w, seq, to_cast):
    m = h.shape[0]
    n_tiles = m // ROW_TILE
    n_grp = len(POOL_WINDOWS)
    cast_in, cast_out, cast_shapes = _cast_specs(to_cast, n_tiles)
    outs = pl.pallas_call(
        functools.partial(_ffn_pool_kernel, n_tiles, seq // ROW_TILE, tuple(f for _, _, f in to_cast)),
        grid=(n_tiles + 1,),
        in_specs=[_row_spec(_cur_tile(n_tiles))] + _ffn_weight_specs() + [
            _const_spec((1, D_MODEL)), _const_spec((1, D_MODEL)), _const_spec((D_MODEL, D_MODEL)),
            _const_spec((n_grp, POOL_GROUP_DIM, POOL_GROUP_DIM)), _const_spec((1, D_MODEL)),
            _const_spec((D_MODEL, D_MODEL))] + cast_in,
        out_specs=[_row_spec(_prev_tile())] + cast_out,
        out_shape=[jax.ShapeDtypeStruct((m, D_MODEL), _F32)] + cast_shapes,
        scratch_shapes=[_HANDOVER, pltpu.VMEM((POOL_HALO + ROW_TILE, D_MODEL), _F32)] + _FFN_WORK,
        compiler_params=_params(("arbitrary",)),
        name="ffn_pool",
    )(h, *ffn_w, *pool_w, *(w for w, _, _ in to_cast))
    return outs[0], outs[1:]


def _ffn_and_keep(h_ref, o_ref, put, ffn_w, work):
    def write(y):
        o_ref[...] = y
        put(y)

    return lambda: _ffn_steps(lambda: h_ref[...], write, *ffn_w, *work)


def _table_spec():
    return pl.BlockSpec((ROW_TILE, LANES), lambda i: (jnp.maximum(i - 1, 0), 0))


def _ffn_kv_kernel(n_tiles, interleave, h_ref, pos_ref, *refs):
    n_cast = len(interleave)
    (ffn_w, kv_w, cast_src, (o_ref,), k_refs, v_refs, (cos_ref, sin_ref), cast_dst,
     (hand_ref, slab_ref, xn2_ref, *work)) = _split(refs, 5, 4, n_cast, 1, N_GROUPS, N_GROUPS, 2, n_cast, 5)
    put, get = _hand_over(hand_ref)
    _cast_blocks(cast_src, cast_dst, interleave)
    _two_stage(n_tiles, _ffn_and_keep(h_ref, o_ref, put, ffn_w, work),
               lambda: _kv_steps(get, pos_ref, *kv_w, k_refs, v_refs, cos_ref, sin_ref, slab_ref, xn2_ref),
               second_is_main=False)


def _ffn_kv(h, pos, ffn_w, kv_w, batch, seq, to_cast):
    m = h.shape[0]
    n_tiles = m // ROW_TILE
    specs, shapes = _residue_major_specs(batch, seq, _BF16, _prev_tile())
    cast_in, cast_out, cast_shapes = _cast_specs(to_cast, n_tiles)
    table_shape = jax.ShapeDtypeStruct((m, LANES), _F32)
    outs = pl.pallas_call(
        functools.partial(_ffn_kv_kernel, n_tiles, tuple(f for _, _, f in to_cast)),
        grid=(n_tiles + 1,),
        in_specs=[_row_spec(_cur_tile(n_tiles)), pl.BlockSpec((ROW_TILE, 1), lambda i: (jnp.maximum(i - 1, 0), 0))]
        + _ffn_weight_specs() + [_const_spec((1, D_MODEL)), _const_spec((1, LANES)),
                                 _const_spec((D_MODEL, QKV_WIDTH)), _const_spec((D_MODEL, QKV_WIDTH))] + cast_in,
        out_specs=[_row_spec(_cur_tile(n_tiles))] + specs * 2 + [_table_spec()] * 2 + cast_out,
        out_shape=[jax.ShapeDtypeStruct((m, D_MODEL), _F32)] + shapes * 2 + [table_shape] * 2 + cast_shapes,
        scratch_shapes=[_HANDOVER, _SLAB, _FFN_WORK[0]] + _FFN_WORK,
        compiler_params=_params(("arbitrary",)),
        name="ffn_kv",
    )(h, pos, *ffn_w, *kv_w, *(w for w, _, _ in to_cast))
    return outs[0], outs[1:4], outs[4:7], outs[7:9], outs[9:]


def _ffn_q_kernel(n_tiles, interleave, h_ref, cos_ref, sin_ref, *refs):
    n_cast = len(interleave)
    ffn_w, q_w, cast_src, (o_ref,), q_refs, cast_dst, (hand_ref, slab_ref, xn2_ref, *work) = _split(
        refs, 5, 2, n_cast, 1, N_GROUPS, n_cast, 5)
    put, get = _hand_over(hand_ref)
    _cast_blocks(cast_src, cast_dst, interleave)
    _two_stage(n_tiles, _ffn_and_keep(h_ref, o_ref, put, ffn_w, work),
               lambda: _q_steps(get, cos_ref, sin_ref, *q_w, q_refs, slab_ref, xn2_ref),
               second_is_main=False)


def _ffn_q(h, tables, ffn_w, q_w, batch, seq, to_cast):
    m = h.shape[0]
    n_tiles = m // ROW_TILE
    specs, shapes = _residue_major_specs(batch, seq, _BF16, _prev_tile())
    cast_in, cast_out, cast_shapes = _cast_specs(to_cast, n_tiles)
    outs = pl.pallas_call(
        functools.partial(_ffn_q_kernel, n_tiles, tuple(f for _, _, f in to_cast)),
        grid=(n_tiles + 1,),
        in_specs=[_row_spec(_cur_tile(n_tiles))] + [_table_spec()] * 2 + _ffn_weight_specs()
        + [_const_spec((1, D_MODEL)), _const_spec((D_MODEL, QKV_WIDTH))] + cast_in,
        out_specs=[_row_spec(_cur_tile(n_tiles))] + specs + cast_out,
        out_shape=[jax.ShapeDtypeStruct((m, D_MODEL), _F32)] + shapes + cast_shapes,
        scratch_shapes=[_HANDOVER, _SLAB, _FFN_WORK[0]] + _FFN_WORK,
        compiler_params=_params(("arbitrary",)),
        name="ffn_q",
    )(h, *tables, *ffn_w, *q_w, *(w for w, _, _ in to_cast))
    return outs[0], outs[1:4], outs[4:]


def _merge_ffn_kernel(n_tiles, h_ref, *refs):
    merge_w, o_refs, l_refs, ffn_w, (out_ref, hand_ref, nat_ref, *work) = (
        refs[:2], refs[2:5], refs[5:8], refs[8:13], refs[13:])
    put, get = _hand_over(hand_ref)

    def write(y):
        out_ref[...] = y

    _two_stage(n_tiles,
               lambda: _merge_steps(lambda: h_ref[...], put, *merge_w, o_refs, l_refs, nat_ref),
               lambda: _ffn_steps(get, write, *ffn_w, *work),
               second_is_main=True)


def _merge_ffn(h, merge_w, outs, lses, ffn_w, batch, seq):
    m = h.shape[0]
    n_tiles = m // ROW_TILE
    specs, _ = _residue_major_specs(batch, seq, _F32, _cur_tile(n_tiles))
    return pl.pallas_call(
        functools.partial(_merge_ffn_kernel, n_tiles),
        grid=(n_tiles + 1,),
        in_specs=[_row_spec(_cur_tile(n_tiles)), _const_spec((1, D_MODEL)), _const_spec((GROUP_WIDTH, D_MODEL))]
        + specs * 2 + _ffn_weight_specs(),
        out_specs=_row_spec(_prev_tile()),
        out_shape=jax.ShapeDtypeStruct((m, D_MODEL), _F32),
        scratch_shapes=[_HANDOVER, pltpu.VMEM((4, PAIRS_PER_GROUP, ROW_TILE, LANES), _F32)] + _FFN_WORK,
        compiler_params=_params(("arbitrary",)),
        name="merge_ffn",
    )(h, *merge_w, *outs, *lses, *ffn_w)


def _attn_kernel(q_ref, kp_ref, kc_ref, vp_ref, vc_ref, o_ref, lse_ref):
    step = pl.program_id(1)
    n_blk = ATTN_Q_TILE // ATTN_BLK
    win = 2 * ATTN_BLK
    qi = lax.broadcasted_iota(jnp.int32, (win, win), 0) % ATTN_BLK
    kj = lax.broadcasted_iota(jnp.int32, (win, win), 1)
    band = jnp.where((kj >= qi) & (kj <= qi + ATTN_BLK), 0.0, NEG_BIG)
    first_lo = jnp.maximum(qi, jnp.where(step == 0, ATTN_BLK, 0))
    band_first = jnp.where((kj >= first_lo) & (kj <= qi + ATTN_BLK), 0.0, NEG_BIG)
    q_lane = lax.broadcasted_iota(jnp.int32, (ATTN_BLK, LANES), 1)
    q_is_a = (q_lane % HEAD_DIM) < HEAD_DIM // 2
    v_lane = lax.broadcasted_iota(jnp.int32, (win, LANES), 1)
    v_is_a = v_lane < HEAD_DIM
    ones_a = jnp.where(v_is_a, 1.0, 0.0).astype(_BF16)
    ones_b = jnp.where(v_is_a, 0.0, 1.0).astype(_BF16)
    o_is_a = q_lane < HEAD_DIM
    nt = (((1,), (1,)), ((), ()))
    work = [(j, p) for j in range(n_blk) for p in range(PAIRS_PER_GROUP)]

    def window(prev_ref, cur_ref, j, p):
        pair = slice(p * LANES, (p + 1) * LANES)
        if j == 0:
            return jnp.concatenate([prev_ref[0, :, pair], cur_ref[0, 0:ATTN_BLK, pair]], axis=0)
        return cur_ref[0, (j - 1) * ATTN_BLK:(j + 1) * ATTN_BLK, pair]

    def scores(j, p):
        q = q_ref[0, j * ATTN_BLK:(j + 1) * ATTN_BLK, p * LANES:(p + 1) * LANES]
        zero = jnp.zeros_like(q)
        q2 = jnp.concatenate([jnp.where(q_is_a, q, zero), jnp.where(q_is_a, zero, q)], axis=0)
        s = lax.dot_general(q2, window(kp_ref, kc_ref, j, p), nt, preferred_element_type=_F32)
        return s + (band_first if j == 0 else band)

    def finish(j, p, s):
        m = jnp.max(s, axis=-1, keepdims=True)
        prob = jnp.exp(s - m).astype(_BF16)
        prob = jnp.concatenate([prob[:ATTN_BLK], prob[ATTN_BLK:]], axis=1)
        v = window(vp_ref, vc_ref, j, p)
        zero = jnp.zeros_like(v)
        v2 = jnp.concatenate([jnp.concatenate([jnp.where(v_is_a, v, zero), ones_a], axis=1),
                              jnp.concatenate([jnp.where(v_is_a, zero, v), ones_b], axis=1)], axis=0)
        r = _dot(prob, v2)
        den = r[:, LANES:]
        rows = slice(j * ATTN_BLK, (j + 1) * ATTN_BLK)
        cols = slice(p * LANES, (p + 1) * LANES)
        o_ref[0, rows, cols] = r[:, :LANES] * (1.0 / den)
        lse_ref[0, rows, cols] = jnp.where(o_is_a, m[:ATTN_BLK], m[ATTN_BLK:]) + jnp.log(den)

    ahead = 1
    pending = [scores(*work[n]) for n in range(ahead)]
    for n, (j, p) in enumerate(work):
        s = pending.pop(0)
        if n + ahead < len(work):
            pending.append(scores(*work[n + ahead]))
        finish(j, p, s)


def _attn_group(q, k, v):
    batch, d, rows, _ = q.shape
    n_seq = batch * d
    q, k, v = (a.reshape(n_seq, rows, GROUP_WIDTH) for a in (q, k, v))
    blocks_per_step = ATTN_Q_TILE // ATTN_BLK

    def cur(s, i):
        return (s, i, 0)

    def prev(s, i):
        return (s, jnp.maximum(i * blocks_per_step - 1, 0), 0)

    cur_spec = pl.BlockSpec((1, ATTN_Q_TILE, GROUP_WIDTH), cur)
    prev_spec = pl.BlockSpec((1, ATTN_BLK, GROUP_WIDTH), prev)
    out_shape = jax.ShapeDtypeStruct((n_seq, rows, GROUP_WIDTH), _F32)
    o, lse = pl.pallas_call(
        _attn_kernel,
        grid=(n_seq, rows // ATTN_Q_TILE),
        in_specs=[cur_spec, prev_spec, cur_spec, prev_spec, cur_spec],
        out_specs=[cur_spec, cur_spec],
        out_shape=[out_shape, out_shape],
        compiler_params=_params(("parallel", "arbitrary")),
        name=f"window_attn_d{d}",
    )(q, k, k, v, v)
    shape = (batch, d, rows, GROUP_WIDTH)
    return o.reshape(shape), lse.reshape(shape)


def kernel(x, positions, norm_gain, ffn_w_gate, ffn_w_up, ffn_w_down, pool_w_in, pool_w_group, pool_scale,
           pool_w_out, kv_norm_gain, w_k, w_v, attn_w_q, attn_w_o):
    batch, seq, _ = x.shape
    m = batch * seq
    gains = norm_gain.reshape(2, 6, 1, D_MODEL)
    bf = lambda w: w.astype(_BF16)

    def ffn_f32(layer, slot):
        return tuple((w, (layer, slot), False) for w in (ffn_w_gate, ffn_w_up, ffn_w_down))

    def ffn_gains(layer, slot):
        return (gains[layer, 4 * slot], gains[layer, 4 * slot + 1])

    pos = positions.astype(_F32).reshape(m, 1)
    inv_freq = ROPE_THETA ** (-jnp.arange(0, HEAD_DIM, 2, dtype=_F32) / HEAD_DIM)
    invf = jnp.tile(inv_freq, LANES // (HEAD_DIM // 2)).reshape(1, LANES)

    h = x.reshape(m, D_MODEL)
    pool_w = (gains[0, 2], gains[0, 3], bf(pool_w_in[0]), bf(pool_w_group[0]),
              pool_scale[0].reshape(1, D_MODEL), bf(pool_w_out[0]))
    ffn00 = tuple(bf(w[lead]) for w, lead, _ in ffn_f32(0, 0))
    h, (*ffn01, wk_bf, wv_bf) = _ffn_pool(h, ffn_gains(0, 0) + ffn00, pool_w, seq,
                                          to_cast=ffn_f32(0, 1) + ((w_k, (), True), (w_v, (), False)))
    kv_w = (kv_norm_gain.reshape(1, D_MODEL), invf, wk_bf, wv_bf)
    h, k_sh, v_sh, tables, (*ffn10, wq_bf) = _ffn_kv(h, pos, ffn_gains(0, 1) + tuple(ffn01), kv_w, batch, seq,
                                                     to_cast=ffn_f32(1, 0) + ((attn_w_q, (0,), True),))
    h, q, (*ffn11, wo_bf) = _ffn_q(h, tables, ffn_gains(1, 0) + tuple(ffn10), (gains[1, 2], wq_bf), batch, seq,
                                   to_cast=ffn_f32(1, 1) + ((attn_w_o, (0,), False),))
    outs, lses = [], []
    for g in range(N_GROUPS):
        o, lse = _attn_group(q[g], k_sh[g], v_sh[g])
        outs.append(o)
        lses.append(lse)
    h = _merge_ffn(h, (gains[1, 3], wo_bf), outs, lses, ffn_gains(1, 1) + tuple(ffn11), batch, seq)
    return h.reshape(batch, seq, D_MODEL)
```

```python
import functools

import jax
import jax.numpy as jnp
from jax import lax
from jax.experimental import pallas as pl
from jax.experimental.pallas import tpu as pltpu

D_MODEL = 1024
D_FF = 2816
POOL_WINDOWS = (2, 4, 8, 16)
POOL_GROUP_DIM = D_MODEL // len(POOL_WINDOWS)
POOL_HALO = 16
ATTN_DILATIONS = (1, 4, 16)
N_GROUPS = len(ATTN_DILATIONS)
ATTN_BLK = 128
HEAD_DIM = 64
HEADS_PER_GROUP = 8
GROUP_WIDTH = HEADS_PER_GROUP * HEAD_DIM
QKV_WIDTH = N_GROUPS * GROUP_WIDTH
ROPE_THETA = 10000.0
RMS_EPS = 1e-6
NEG_BIG = -1e30

LANES = 128
PAIRS_PER_GROUP = GROUP_WIDTH // LANES
QKV_TILES = QKV_WIDTH // LANES
ROW_TILE = 512
ATTN_Q_TILE = 512
FF_CHUNK = 256
VMEM_LIMIT_BYTES = 56 * 1024 * 1024

_F32 = jnp.float32
_BF16 = jnp.bfloat16


def _rms_norm(x, gain):
    ms = jnp.mean(x * x, axis=-1, keepdims=True)
    return x * lax.rsqrt(ms + RMS_EPS) * gain


def _dot(a, b):
    return jnp.dot(a, b, preferred_element_type=_F32)


def _const_spec(shape):
    return pl.BlockSpec(shape, lambda *_: (0,) * len(shape), pipeline_mode=pl.Buffered(1))


def _params(semantics):
    return pltpu.CompilerParams(dimension_semantics=semantics, vmem_limit_bytes=VMEM_LIMIT_BYTES)


def _ffn_steps(read, write, gpre_ref, gpost_ref, wg_ref, wu_ref, wd_ref, xn_ref, acc_ref):
    def start():
        xn_ref[...] = _rms_norm(read(), gpre_ref[...]).astype(_BF16)

    def chunk(c):
        def run():
            g = _dot(xn_ref[...], wg_ref[:, c:c + FF_CHUNK])
            u = _dot(xn_ref[...], wu_ref[:, c:c + FF_CHUNK])
            a = (g * jax.nn.sigmoid(g) * u).astype(_BF16)
            down = _dot(a, wd_ref[c:c + FF_CHUNK, :])
            if c == 0:
                acc_ref[...] = down
            else:
                acc_ref[...] += down
        return run

    def end():
        write(read() + 0.5 * _rms_norm(acc_ref[...], gpost_ref[...]))

    return [start] + [chunk(c) for c in range(0, D_FF, FF_CHUNK)] + [end]


def _pool_steps(read, write, j, gpre_ref, gpost_ref, win_ref, wgrp_ref, scale_ref, wout_ref, ubuf_ref):
    st = {"y": []}

    def start():
        st["x"] = read()
        hm = _rms_norm(st["x"], gpre_ref[...]).astype(_BF16)
        st["u"] = _dot(hm, win_ref[...])
        ubuf_ref[0:POOL_HALO, :] = jnp.where(j == 0, 0.0, ubuf_ref[0:POOL_HALO, :])
        ubuf_ref[POOL_HALO:, :] = st["u"]

    def group(g, w):
        def run():
            tm = st["x"].shape[0]
            cols = slice(g * POOL_GROUP_DIM, (g + 1) * POOL_GROUP_DIM)
            t = j * tm + lax.broadcasted_iota(jnp.int32, (tm, 1), 0)
            s = ubuf_ref[:, cols]
            k = 1
            while k < w:
                s = s + pltpu.roll(s, k, axis=0)
                k *= 2
            count = jnp.minimum(t + 1, w).astype(_F32)
            p = s[POOL_HALO:] / count - st["u"][:, cols]
            st["y"].append((_dot(p.astype(_BF16), wgrp_ref[g]) * scale_ref[:, cols]).astype(_BF16))
        return run

    def end():
        tm = st["x"].shape[0]
        mix = _dot(jnp.concatenate(st["y"], axis=-1), wout_ref[...])
        ubuf_ref[0:POOL_HALO, :] = st["u"][tm - POOL_HALO:, :]
        write(st["x"] + _rms_norm(mix, gpost_ref[...]))

    return [start] + [group(g, w) for g, w in enumerate(POOL_WINDOWS)] + [end]


def _rope_tables(c, s):
    lane = lax.broadcasted_iota(jnp.int32, c.shape, 1)
    return c, jnp.where(lane < LANES // 2, -s, s)


def _rope_tile(xt, tables):
    c, s_signed = tables
    return xt * c + pltpu.roll(xt, LANES // 2, axis=1) * s_signed


def _pair_interleaved(t):
    half = HEAD_DIM // 2
    block = lax.broadcasted_iota(jnp.int32, t.shape, 1) // half
    return jnp.where(block == 1, pltpu.roll(t, LANES - half, axis=1),
                     jnp.where(block == 2, pltpu.roll(t, half, axis=1), t))


def _projection_steps(st, xn_ref, w_ref, slab_ref, out_refs, rope=False, scale=None):
    def project(g):
        def run():
            y = _dot(xn_ref[...], w_ref[:, g * GROUP_WIDTH:(g + 1) * GROUP_WIDTH])
            for p in range(PAIRS_PER_GROUP):
                yt = y[:, p * LANES:(p + 1) * LANES]
                if rope:
                    yt = _rope_tile(yt, st["tables"])
                if scale is not None:
                    yt = yt * scale
                slab_ref[g * PAIRS_PER_GROUP + p] = yt
        return run

    def scatter(g, d):
        def run():
            n = slab_ref.shape[1] // d
            for r in range(d):
                for p in range(PAIRS_PER_GROUP):
                    rows = slab_ref[g * PAIRS_PER_GROUP + p, pl.ds(r, n, stride=d), :]
                    out_refs[g][0, r, :, p * LANES:(p + 1) * LANES] = rows.astype(_BF16)
        return run

    steps = []
    for g, d in enumerate(ATTN_DILATIONS):
        steps += [project(g), scatter(g, d)]
    return steps


def _kv_steps(read, pos_ref, gain_ref, invf_ref, wk_ref, wv_ref, k_refs, v_refs, cos_ref, sin_ref, slab_ref, xn_ref):
    st = {}

    def start():
        xn_ref[...] = _rms_norm(read(), gain_ref[...]).astype(_BF16)
        ang = pos_ref[...] * invf_ref[...]
        cos_ref[...] = jnp.cos(ang)
        sin_ref[...] = jnp.sin(ang)
        st["tables"] = _rope_tables(cos_ref[...], sin_ref[...])

    return ([start] + _projection_steps(st, xn_ref, wk_ref, slab_ref, k_refs, rope=True)
            + _projection_steps(st, xn_ref, wv_ref, slab_ref, v_refs))


def _q_steps(read, cos_ref, sin_ref, gain_ref, wq_ref, q_refs, slab_ref, xn_ref):
    st = {}

    def start():
        xn_ref[...] = _rms_norm(read(), gain_ref[...]).astype(_BF16)
        st["tables"] = _rope_tables(cos_ref[...], sin_ref[...])

    return [start] + _projection_steps(st, xn_ref, wq_ref, slab_ref, q_refs, rope=True, scale=HEAD_DIM ** -0.5)


def _merge_steps(read, write, gpost_ref, wo_ref, o_refs, l_refs, nat_ref):
    st = {"o": []}
    permuted = ((o_refs[1], ATTN_DILATIONS[1]), (l_refs[1], ATTN_DILATIONS[1]),
                (o_refs[2], ATTN_DILATIONS[2]), (l_refs[2], ATTN_DILATIONS[2]))

    def lane_tile(p):
        def run():
            cols = slice(p * LANES, (p + 1) * LANES)
            for a, (ref, d) in enumerate(permuted):
                n = nat_ref.shape[2] // d
                for r in range(d):
                    nat_ref[a, p, pl.ds(r, n, stride=d), :] = ref[0, r, :, cols]
            o0, l0 = o_refs[0][0, 0, :, cols], l_refs[0][0, 0, :, cols]
            o1, l1, o2, l2 = (nat_ref[a, p] for a in range(4))
            top = jnp.maximum(jnp.maximum(l0, l1), l2)
            e0, e1, e2 = jnp.exp(l0 - top), jnp.exp(l1 - top), jnp.exp(l2 - top)
            st["o"].append(((e0 * o0 + e1 * o1 + e2 * o2) / (e0 + e1 + e2)).astype(_BF16))
        return run

    def end():
        mix = _dot(jnp.concatenate(st["o"], axis=-1), wo_ref[...])
        write(read() + _rms_norm(mix, gpost_ref[...]))

    return [lane_tile(p) for p in range(PAIRS_PER_GROUP)] + [end]


def _run(steps):
    for step in steps:
        step()


FILL_LEAD = 1


def _interleave(main, fill):
    done = 0
    slots = len(main) - 1 - FILL_LEAD
    for k, step in enumerate(main):
        step()
        due = -(-(k + 1 - FILL_LEAD) * len(fill) // slots) if k >= FILL_LEAD else 0
        while done < min(due, len(fill)):
            fill[done]()
            done += 1


def _two_stage(hand_ref, first, second, second_is_main, init=None):
    @pl.when(pl.program_id(0) == 0)
    def _():
        hand_ref[1] = jnp.zeros(hand_ref.shape[1:], hand_ref.dtype)
        if init is not None:
            init()

    if second_is_main:
        _interleave(second(), first())
    else:
        _interleave(first(), second())


def _cur_tile(n_tiles):
    return lambda i: jnp.minimum(i, n_tiles - 1)


def _prev_tile():
    return lambda i: jnp.maximum(i - 1, 0)


def _row_spec(tile_of):
    return pl.BlockSpec((ROW_TILE, D_MODEL), lambda i: (tile_of(i), 0))


def _residue_major_specs(batch, seq, dtype, tile_of):
    tiles_per_seq = seq // ROW_TILE
    specs, shapes = [], []
    for d in ATTN_DILATIONS:
        specs.append(pl.BlockSpec((1, d, ROW_TILE // d, GROUP_WIDTH),
                                  lambda i: (tile_of(i) // tiles_per_seq, 0, tile_of(i) % tiles_per_seq, 0)))
        shapes.append(jax.ShapeDtypeStruct((batch, d, seq // d, GROUP_WIDTH), dtype))
    return specs, shapes


def _ffn_weight_specs():
    return [_const_spec((1, D_MODEL)), _const_spec((1, D_MODEL)),
            _const_spec((D_MODEL, D_FF)), _const_spec((D_MODEL, D_FF)), _const_spec((D_FF, D_MODEL))]


_HANDOVER = pltpu.VMEM((2, ROW_TILE, D_MODEL), _F32)
_SLAB = pltpu.VMEM((QKV_TILES, ROW_TILE, LANES), _F32)
_FFN_WORK = [pltpu.VMEM((ROW_TILE, D_MODEL), _BF16), pltpu.VMEM((ROW_TILE, D_MODEL), _F32)]
BF16_SUBLANES = 16


def _hand_over(hand_ref):
    i = pl.program_id(0)

    def put(y):
        hand_ref[i % 2] = y

    def get():
        return hand_ref[(i + 1) % 2]

    return put, get


def _split(refs, *counts):
    out, at = [], 0
    for n in counts:
        out.append(refs[at:at + n])
        at += n
    assert at == len(refs)
    return out


def _cast_specs(weights, n_tiles):
    in_specs, out_specs, out_shapes = [], [], []
    for w, lead, _ in weights:
        rows, cols = w.shape[len(lead):]
        n_blocks = max(n for n in range(1, n_tiles + 1) if rows % (n * BF16_SUBLANES) == 0)

        def block(i, n_blocks=n_blocks):
            return (jnp.minimum(i, n_blocks - 1), 0)

        in_specs.append(pl.BlockSpec((None,) * len(lead) + (rows // n_blocks, cols),
                                     lambda i, lead=lead, block=block: lead + block(i)))
        out_specs.append(pl.BlockSpec((rows // n_blocks, cols), block))
        out_shapes.append(jax.ShapeDtypeStruct((rows, cols), _BF16))
    return in_specs, out_specs, out_shapes


def _cast_blocks(src_refs, dst_refs, interleave):
    for src, dst, flag in zip(src_refs, dst_refs, interleave):
        if flag:
            for j in range(src.shape[1] // LANES):
                cols = slice(j * LANES, (j + 1) * LANES)
                dst[:, cols] = _pair_interleaved(src[:, cols]).astype(_BF16)
        else:
            dst[...] = src[...].astype(_BF16)


def _ffn_pool_kernel(n_tiles, tiles_per_seq, interleave, h_ref, *refs):
    n_cast = len(interleave)
    ffn_w, pool_w, cast_src, (o_ref,), cast_dst, (hand_ref, ubuf_ref, *work) = _split(
        refs, 5, 6, n_cast, 1, n_cast, 4)
    put, get = _hand_over(hand_ref)
    j = (pl.program_id(0) - 1) % tiles_per_seq

    def write(y):
        o_ref[...] = y

    def init():
        ubuf_ref[0:POOL_HALO, :] = jnp.zeros((POOL_HALO, D_MODEL), _F32)

    _cast_blocks(cast_src, cast_dst, interleave)
    _two_stage(hand_ref,
               lambda: _ffn_steps(lambda: h_ref[...], put, *ffn_w, *work),
               lambda: _pool_steps(get, write, j, *pool_w, ubuf_ref),
               second_is_main=False, init=init)


def _ffn_pool(h, ffn_w, pool_w, seq, to_cast):
    m = h.shape[0]
    n_tiles = m // ROW_TILE
    n_grp = len(POOL_WINDOWS)
    cast_in, cast_out, cast_shapes = _cast_specs(to_cast, n_tiles)
    outs = pl.pallas_call(
        functools.partial(_ffn_pool_kernel, n_tiles, seq // ROW_TILE, tuple(f for _, _, f in to_cast)),
        grid=(n_tiles + 1,),
        in_specs=[_row_spec(_cur_tile(n_tiles))] + _ffn_weight_specs() + [
            _const_spec((1, D_MODEL)), _const_spec((1, D_MODEL)), _const_spec((D_MODEL, D_MODEL)),
            _const_spec((n_grp, POOL_GROUP_DIM, POOL_GROUP_DIM)), _const_spec((1, D_MODEL)),
            _const_spec((D_MODEL, D_MODEL))] + cast_in,
        out_specs=[_row_spec(_prev_tile())] + cast_out,
        out_shape=[jax.ShapeDtypeStruct((m, D_MODEL), _F32)] + cast_shapes,
        scratch_shapes=[_HANDOVER, pltpu.VMEM((POOL_HALO + ROW_TILE, D_MODEL), _F32)] + _FFN_WORK,
        compiler_params=_params(("arbitrary",)),
        name="ffn_pool",
    )(h, *ffn_w, *pool_w, *(w for w, _, _ in to_cast))
    return outs[0], outs[1:]


def _ffn_and_keep(h_ref, o_ref, put, ffn_w, work):
    def write(y):
        o_ref[...] = y
        put(y)

    return lambda: _ffn_steps(lambda: h_ref[...], write, *ffn_w, *work)


def _table_spec():
    return pl.BlockSpec((ROW_TILE, LANES), lambda i: (jnp.maximum(i - 1, 0), 0))


def _ffn_kv_kernel(n_tiles, interleave, h_ref, pos_ref, *refs):
    n_cast = len(interleave)
    (ffn_w, kv_w, cast_src, (o_ref,), k_refs, v_refs, (cos_ref, sin_ref), cast_dst,
     (hand_ref, slab_ref, xn2_ref, *work)) = _split(refs, 5, 4, n_cast, 1, N_GROUPS, N_GROUPS, 2, n_cast, 5)
    put, get = _hand_over(hand_ref)
    _cast_blocks(cast_src, cast_dst, interleave)
    _two_stage(hand_ref, _ffn_and_keep(h_ref, o_ref, put, ffn_w, work),
               lambda: _kv_steps(get, pos_ref, *kv_w, k_refs, v_refs, cos_ref, sin_ref, slab_ref, xn2_ref),
               second_is_main=False)


def _ffn_kv(h, pos, ffn_w, kv_w, batch, seq, to_cast):
    m = h.shape[0]
    n_tiles = m // ROW_TILE
    specs, shapes = _residue_major_specs(batch, seq, _BF16, _prev_tile())
    cast_in, cast_out, cast_shapes = _cast_specs(to_cast, n_tiles)
    table_shape = jax.ShapeDtypeStruct((m, LANES), _F32)
    outs = pl.pallas_call(
        functools.partial(_ffn_kv_kernel, n_tiles, tuple(f for _, _, f in to_cast)),
        grid=(n_tiles + 1,),
        in_specs=[_row_spec(_cur_tile(n_tiles)), pl.BlockSpec((ROW_TILE, 1), lambda i: (jnp.maximum(i - 1, 0), 0))]
        + _ffn_weight_specs() + [_const_spec((1, D_MODEL)), _const_spec((1, LANES)),
                                 _const_spec((D_MODEL, QKV_WIDTH)), _const_spec((D_MODEL, QKV_WIDTH))] + cast_in,
        out_specs=[_row_spec(_cur_tile(n_tiles))] + specs * 2 + [_table_spec()] * 2 + cast_out,
        out_shape=[jax.ShapeDtypeStruct((m, D_MODEL), _F32)] + shapes * 2 + [table_shape] * 2 + cast_shapes,
        scratch_shapes=[_HANDOVER, _SLAB, _FFN_WORK[0]] + _FFN_WORK,
        compiler_params=_params(("arbitrary",)),
        name="ffn_kv",
    )(h, pos, *ffn_w, *kv_w, *(w for w, _, _ in to_cast))
    return outs[0], outs[1:4], outs[4:7], outs[7:9], outs[9:]


def _ffn_q_kernel(n_tiles, interleave, h_ref, cos_ref, sin_ref, *refs):
    n_cast = len(interleave)
    ffn_w, q_w, cast_src, (o_ref,), q_refs, cast_dst, (hand_ref, slab_ref, xn2_ref, *work) = _split(
        refs, 5, 2, n_cast, 1, N_GROUPS, n_cast, 5)
    put, get = _hand_over(hand_ref)
    _cast_blocks(cast_src, cast_dst, interleave)
    _two_stage(hand_ref, _ffn_and_keep(h_ref, o_ref, put, ffn_w, work),
               lambda: _q_steps(get, cos_ref, sin_ref, *q_w, q_refs, slab_ref, xn2_ref),
               second_is_main=False)


def _ffn_q(h, tables, ffn_w, q_w, batch, seq, to_cast):
    m = h.shape[0]
    n_tiles = m // ROW_TILE
    specs, shapes = _residue_major_specs(batch, seq, _BF16, _prev_tile())
    cast_in, cast_out, cast_shapes = _cast_specs(to_cast, n_tiles)
    outs = pl.pallas_call(
        functools.partial(_ffn_q_kernel, n_tiles, tuple(f for _, _, f in to_cast)),
        grid=(n_tiles + 1,),
        in_specs=[_row_spec(_cur_tile(n_tiles))] + [_table_spec()] * 2 + _ffn_weight_specs()
        + [_const_spec((1, D_MODEL)), _const_spec((D_MODEL, QKV_WIDTH))] + cast_in,
        out_specs=[_row_spec(_cur_tile(n_tiles))] + specs + cast_out,
        out_shape=[jax.ShapeDtypeStruct((m, D_MODEL), _F32)] + shapes + cast_shapes,
        scratch_shapes=[_HANDOVER, _SLAB, _FFN_WORK[0]] + _FFN_WORK,
        compiler_params=_params(("arbitrary",)),
        name="ffn_q",
    )(h, *tables, *ffn_w, *q_w, *(w for w, _, _ in to_cast))
    return outs[0], outs[1:4], outs[4:]


def _merge_ffn_kernel(n_tiles, h_ref, *refs):
    merge_w, o_refs, l_refs, ffn_w, (out_ref, hand_ref, nat_ref, *work) = (
        refs[:2], refs[2:5], refs[5:8], refs[8:13], refs[13:])
    put, get = _hand_over(hand_ref)

    def write(y):
        out_ref[...] = y

    _two_stage(hand_ref,
               lambda: _merge_steps(lambda: h_ref[...], put, *merge_w, o_refs, l_refs, nat_ref),
               lambda: _ffn_steps(get, write, *ffn_w, *work),
               second_is_main=True)


def _merge_ffn(h, merge_w, outs, lses, ffn_w, batch, seq):
    m = h.shape[0]
    n_tiles = m // ROW_TILE
    specs, _ = _residue_major_specs(batch, seq, _F32, _cur_tile(n_tiles))
    return pl.pallas_call(
        functools.partial(_merge_ffn_kernel, n_tiles),
        grid=(n_tiles + 1,),
        in_specs=[_row_spec(_cur_tile(n_tiles)), _const_spec((1, D_MODEL)), _const_spec((GROUP_WIDTH, D_MODEL))]
        + specs * 2 + _ffn_weight_specs(),
        out_specs=_row_spec(_prev_tile()),
        out_shape=jax.ShapeDtypeStruct((m, D_MODEL), _F32),
        scratch_shapes=[_HANDOVER, pltpu.VMEM((4, PAIRS_PER_GROUP, ROW_TILE, LANES), _F32)] + _FFN_WORK,
        compiler_params=_params(("arbitrary",)),
        name="merge_ffn",
    )(h, *merge_w, *outs, *lses, *ffn_w)


def _attn_kernel(q_ref, kp_ref, kc_ref, vp_ref, vc_ref, o_ref, lse_ref):
    step = pl.program_id(1)
    n_blk = ATTN_Q_TILE // ATTN_BLK
    win = 2 * ATTN_BLK
    qi = lax.broadcasted_iota(jnp.int32, (win, win), 0) % ATTN_BLK
    kj = lax.broadcasted_iota(jnp.int32, (win, win), 1)
    band = jnp.where((kj >= qi) & (kj <= qi + ATTN_BLK), 0.0, NEG_BIG)
    first_lo = jnp.maximum(qi, jnp.where(step == 0, ATTN_BLK, 0))
    band_first = jnp.where((kj >= first_lo) & (kj <= qi + ATTN_BLK), 0.0, NEG_BIG)
    q_lane = lax.broadcasted_iota(jnp.int32, (ATTN_BLK, LANES), 1)
    q_is_a = (q_lane % HEAD_DIM) < HEAD_DIM // 2
    v_lane = lax.broadcasted_iota(jnp.int32, (win, LANES), 1)
    v_is_a = v_lane < HEAD_DIM
    ones_a = jnp.where(v_is_a, 1.0, 0.0).astype(_BF16)
    ones_b = jnp.where(v_is_a, 0.0, 1.0).astype(_BF16)
    o_is_a = q_lane < HEAD_DIM
    nt = (((1,), (1,)), ((), ()))
    work = [(j, p) for j in range(n_blk) for p in range(PAIRS_PER_GROUP)]

    def window(prev_ref, cur_ref, j, p):
        pair = slice(p * LANES, (p + 1) * LANES)
        if j == 0:
            return jnp.concatenate([prev_ref[0, :, pair], cur_ref[0, 0:ATTN_BLK, pair]], axis=0)
        return cur_ref[0, (j - 1) * ATTN_BLK:(j + 1) * ATTN_BLK, pair]

    def scores(j, p):
        q = q_ref[0, j * ATTN_BLK:(j + 1) * ATTN_BLK, p * LANES:(p + 1) * LANES]
        zero = jnp.zeros_like(q)
        q2 = jnp.concatenate([jnp.where(q_is_a, q, zero), jnp.where(q_is_a, zero, q)], axis=0)
        s = lax.dot_general(q2, window(kp_ref, kc_ref, j, p), nt, preferred_element_type=_F32)
        return s + (band_first if j == 0 else band)

    def finish(j, p, s):
        m = jnp.max(s, axis=-1, keepdims=True)
        prob = jnp.exp(s - m).astype(_BF16)
        prob = jnp.concatenate([prob[:ATTN_BLK], prob[ATTN_BLK:]], axis=1)
        v = window(vp_ref, vc_ref, j, p)
        zero = jnp.zeros_like(v)
        v2 = jnp.concatenate([jnp.concatenate([jnp.where(v_is_a, v, zero), ones_a], axis=1),
                              jnp.concatenate([jnp.where(v_is_a, zero, v), ones_b], axis=1)], axis=0)
        r = _dot(prob, v2)
        den = r[:, LANES:]
        rows = slice(j * ATTN_BLK, (j + 1) * ATTN_BLK)
        cols = slice(p * LANES, (p + 1) * LANES)
        o_ref[0, rows, cols] = r[:, :LANES] * (1.0 / den)
        lse_ref[0, rows, cols] = jnp.where(o_is_a, m[:ATTN_BLK], m[ATTN_BLK:]) + jnp.log(den)

    ahead = 1
    pending = [scores(*work[n]) for n in range(ahead)]
    for n, (j, p) in enumerate(work):
        s = pending.pop(0)
        if n + ahead < len(work):
            pending.append(scores(*work[n + ahead]))
        finish(j, p, s)


def _attn_group(q, k, v):
    batch, d, rows, _ = q.shape
    n_seq = batch * d
    q, k, v = (a.reshape(n_seq, rows, GROUP_WIDTH) for a in (q, k, v))
    blocks_per_step = ATTN_Q_TILE // ATTN_BLK

    def cur(s, i):
        return (s, i, 0)

    def prev(s, i):
        return (s, jnp.maximum(i * blocks_per_step - 1, 0), 0)

    cur_spec = pl.BlockSpec((1, ATTN_Q_TILE, GROUP_WIDTH), cur)
    prev_spec = pl.BlockSpec((1, ATTN_BLK, GROUP_WIDTH), prev)
    out_shape = jax.ShapeDtypeStruct((n_seq, rows, GROUP_WIDTH), _F32)
    o, lse = pl.pallas_call(
        _attn_kernel,
        grid=(n_seq, rows // ATTN_Q_TILE),
        in_specs=[cur_spec, prev_spec, cur_spec, prev_spec, cur_spec],
        out_specs=[cur_spec, cur_spec],
        out_shape=[out_shape, out_shape],
        compiler_params=_params(("parallel", "arbitrary")),
        name=f"window_attn_d{d}",
    )(q, k, k, v, v)
    shape = (batch, d, rows, GROUP_WIDTH)
    return o.reshape(shape), lse.reshape(shape)


def kernel(x, positions, norm_gain, ffn_w_gate, ffn_w_up, ffn_w_down, pool_w_in, pool_w_group, pool_scale,
           pool_w_out, kv_norm_gain, w_k, w_v, attn_w_q, attn_w_o):
    batch, seq, _ = x.shape
    m = batch * seq
    gains = norm_gain.reshape(2, 6, 1, D_MODEL)
    bf = lambda w: w.astype(_BF16)

    def ffn_f32(layer, slot):
        return tuple((w, (layer, slot), False) for w in (ffn_w_gate, ffn_w_up, ffn_w_down))

    def ffn_gains(layer, slot):
        return (gains[layer, 4 * slot], gains[layer, 4 * slot + 1])

    pos = positions.astype(_F32).reshape(m, 1)
    inv_freq = ROPE_THETA ** (-jnp.arange(0, HEAD_DIM, 2, dtype=_F32) / HEAD_DIM)
    invf = jnp.tile(inv_freq, LANES // (HEAD_DIM // 2)).reshape(1, LANES)

    h = x.reshape(m, D_MODEL)
    pool_w = (gains[0, 2], gains[0, 3], bf(pool_w_in[0]), bf(pool_w_group[0]),
              pool_scale[0].reshape(1, D_MODEL), bf(pool_w_out[0]))
    ffn00 = tuple(bf(w[lead]) for w, lead, _ in ffn_f32(0, 0))
    h, (*ffn01, wk_bf, wv_bf) = _ffn_pool(h, ffn_gains(0, 0) + ffn00, pool_w, seq,
                                          to_cast=ffn_f32(0, 1) + ((w_k, (), True), (w_v, (), False)))
    kv_w = (kv_norm_gain.reshape(1, D_MODEL), invf, wk_bf, wv_bf)
    h, k_sh, v_sh, tables, (*ffn10, wq_bf) = _ffn_kv(h, pos, ffn_gains(0, 1) + tuple(ffn01), kv_w, batch, seq,
                                                     to_cast=ffn_f32(1, 0) + ((attn_w_q, (0,), True),))
    h, q, (*ffn11, wo_bf) = _ffn_q(h, tables, ffn_gains(1, 0) + tuple(ffn10), (gains[1, 2], wq_bf), batch, seq,
                                   to_cast=ffn_f32(1, 1) + ((attn_w_o, (0,), False),))
    outs, lses = [], []
    for g in range(N_GROUPS):
        o, lse = _attn_group(q[g], k_sh[g], v_sh[g])
        outs.append(o)
        lses.append(lse)
    h = _merge_ffn(h, (gains[1, 3], wo_bf), outs, lses, ffn_gains(1, 1) + tuple(ffn11), batch, seq)
    return h.reshape(batch, seq, D_MODEL)
```

```python
import functools

import jax
import jax.numpy as jnp
from jax import lax
from jax.experimental import pallas as pl
from jax.experimental.pallas import tpu as pltpu

D_MODEL = 1024
D_FF = 2816
POOL_WINDOWS = (2, 4, 8, 16)
POOL_GROUP_DIM = D_MODEL // len(POOL_WINDOWS)
POOL_HALO = 16
ATTN_DILATIONS = (1, 4, 16)
N_GROUPS = len(ATTN_DILATIONS)
ATTN_BLK = 128
HEAD_DIM = 64
HEADS_PER_GROUP = 8
GROUP_WIDTH = HEADS_PER_GROUP * HEAD_DIM
QKV_WIDTH = N_GROUPS * GROUP_WIDTH
ROPE_THETA = 10000.0
RMS_EPS = 1e-6
NEG_BIG = -1e30

LANES = 128
PAIRS_PER_GROUP = GROUP_WIDTH // LANES
QKV_TILES = QKV_WIDTH // LANES
ROW_TILE = 512
ATTN_Q_TILE = 512
FF_CHUNK = 256
VMEM_LIMIT_BYTES = 56 * 1024 * 1024

_F32 = jnp.float32
_BF16 = jnp.bfloat16


def _rms_norm(x, gain):
    ms = jnp.mean(x * x, axis=-1, keepdims=True)
    return x * lax.rsqrt(ms + RMS_EPS) * gain


def _dot(a, b):
    return jnp.dot(a, b, preferred_element_type=_F32)


def _const_spec(shape):
    return pl.BlockSpec(shape, lambda *_: (0,) * len(shape), pipeline_mode=pl.Buffered(1))


def _params(semantics):
    return pltpu.CompilerParams(dimension_semantics=semantics, vmem_limit_bytes=VMEM_LIMIT_BYTES)


def _ffn_steps(read, write, gpre_ref, gpost_ref, wg_ref, wu_ref, wd_ref, xn_ref, acc_ref):
    def start():
        xn_ref[...] = _rms_norm(read(), gpre_ref[...]).astype(_BF16)

    def chunk(c):
        def run():
            g = _dot(xn_ref[...], wg_ref[:, c:c + FF_CHUNK])
            u = _dot(xn_ref[...], wu_ref[:, c:c + FF_CHUNK])
            a = (g * jax.nn.sigmoid(g) * u).astype(_BF16)
            down = _dot(a, wd_ref[c:c + FF_CHUNK, :])
            if c == 0:
                acc_ref[...] = down
            else:
                acc_ref[...] += down
        return run

    def end():
        write(read() + 0.5 * _rms_norm(acc_ref[...], gpost_ref[...]))

    return [start] + [chunk(c) for c in range(0, D_FF, FF_CHUNK)] + [end]


def _pool_steps(read, write, j, gpre_ref, gpost_ref, win_ref, wgrp_ref, scale_ref, wout_ref, ubuf_ref):
    st = {"y": []}

    def start():
        st["x"] = read()
        hm = _rms_norm(st["x"], gpre_ref[...]).astype(_BF16)
        st["u"] = _dot(hm, win_ref[...])
        ubuf_ref[0:POOL_HALO, :] = jnp.where(j == 0, 0.0, ubuf_ref[0:POOL_HALO, :])
        ubuf_ref[POOL_HALO:, :] = st["u"]

    def group(g, w):
        def run():
            tm = st["x"].shape[0]
            cols = slice(g * POOL_GROUP_DIM, (g + 1) * POOL_GROUP_DIM)
            t = j * tm + lax.broadcasted_iota(jnp.int32, (tm, 1), 0)
            s = ubuf_ref[:, cols]
            k = 1
            while k < w:
                s = s + pltpu.roll(s, k, axis=0)
                k *= 2
            count = jnp.minimum(t + 1, w).astype(_F32)
            p = s[POOL_HALO:] / count - st["u"][:, cols]
            st["y"].append((_dot(p.astype(_BF16), wgrp_ref[g]) * scale_ref[:, cols]).astype(_BF16))
        return run

    def end():
        tm = st["x"].shape[0]
        mix = _dot(jnp.concatenate(st["y"], axis=-1), wout_ref[...])
        ubuf_ref[0:POOL_HALO, :] = st["u"][tm - POOL_HALO:, :]
        write(st["x"] + _rms_norm(mix, gpost_ref[...]))

    return [start] + [group(g, w) for g, w in enumerate(POOL_WINDOWS)] + [end]


def _rope_tables(c, s):
    lane = lax.broadcasted_iota(jnp.int32, c.shape, 1)
    return c, jnp.where(lane < LANES // 2, -s, s)


def _rope_tile(xt, tables):
    c, s_signed = tables
    return xt * c + pltpu.roll(xt, LANES // 2, axis=1) * s_signed


def _pair_interleaved(t):
    half = HEAD_DIM // 2
    block = lax.broadcasted_iota(jnp.int32, t.shape, 1) // half
    return jnp.where(block == 1, pltpu.roll(t, LANES - half, axis=1),
                     jnp.where(block == 2, pltpu.roll(t, half, axis=1), t))


def _projection_steps(st, xn_ref, w_ref, slab_ref, out_refs, rope=False, scale=None):
    def project(g):
        def run():
            y = _dot(xn_ref[...], w_ref[:, g * GROUP_WIDTH:(g + 1) * GROUP_WIDTH])
            for p in range(PAIRS_PER_GROUP):
                yt = y[:, p * LANES:(p + 1) * LANES]
                if rope:
                    yt = _rope_tile(yt, st["tables"])
                if scale is not None:
                    yt = yt * scale
                slab_ref[g * PAIRS_PER_GROUP + p] = yt
        return run

    def scatter(g, d):
        def run():
            n = slab_ref.shape[1] // d
            for r in range(d):
                for p in range(PAIRS_PER_GROUP):
                    rows = slab_ref[g * PAIRS_PER_GROUP + p, pl.ds(r, n, stride=d), :]
                    out_refs[g][0, r, :, p * LANES:(p + 1) * LANES] = rows.astype(_BF16)
        return run

    steps = []
    for g, d in enumerate(ATTN_DILATIONS):
        steps += [project(g), scatter(g, d)]
    return steps


def _kv_steps(read, pos_ref, gain_ref, invf_ref, wk_ref, wv_ref, k_refs, v_refs, cos_ref, sin_ref, slab_ref, xn_ref):
    st = {}

    def start():
        xn_ref[...] = _rms_norm(read(), gain_ref[...]).astype(_BF16)
        ang = pos_ref[...] * invf_ref[...]
        cos_ref[...] = jnp.cos(ang)
        sin_ref[...] = jnp.sin(ang)
        st["tables"] = _rope_tables(cos_ref[...], sin_ref[...])

    return ([start] + _projection_steps(st, xn_ref, wk_ref, slab_ref, k_refs, rope=True)
            + _projection_steps(st, xn_ref, wv_ref, slab_ref, v_refs))


def _q_steps(read, cos_ref, sin_ref, gain_ref, wq_ref, q_refs, slab_ref, xn_ref):
    st = {}

    def start():
        xn_ref[...] = _rms_norm(read(), gain_ref[...]).astype(_BF16)
        st["tables"] = _rope_tables(cos_ref[...], sin_ref[...])

    return [start] + _projection_steps(st, xn_ref, wq_ref, slab_ref, q_refs, rope=True, scale=HEAD_DIM ** -0.5)


def _merge_steps(read, write, gpost_ref, wo_ref, o_refs, l_refs, nat_ref):
    st = {"o": []}
    permuted = ((o_refs[1], ATTN_DILATIONS[1]), (l_refs[1], ATTN_DILATIONS[1]),
                (o_refs[2], ATTN_DILATIONS[2]), (l_refs[2], ATTN_DILATIONS[2]))

    def lane_tile(p):
        def run():
            cols = slice(p * LANES, (p + 1) * LANES)
            for a, (ref, d) in enumerate(permuted):
                n = nat_ref.shape[2] // d
                for r in range(d):
                    nat_ref[a, p, pl.ds(r, n, stride=d), :] = ref[0, r, :, cols]
            o0, l0 = o_refs[0][0, 0, :, cols], l_refs[0][0, 0, :, cols]
            o1, l1, o2, l2 = (nat_ref[a, p] for a in range(4))
            top = jnp.maximum(jnp.maximum(l0, l1), l2)
            e0, e1, e2 = jnp.exp(l0 - top), jnp.exp(l1 - top), jnp.exp(l2 - top)
            st["o"].append(((e0 * o0 + e1 * o1 + e2 * o2) / (e0 + e1 + e2)).astype(_BF16))
        return run

    def end():
        mix = _dot(jnp.concatenate(st["o"], axis=-1), wo_ref[...])
        write(read() + _rms_norm(mix, gpost_ref[...]))

    return [lane_tile(p) for p in range(PAIRS_PER_GROUP)] + [end]


def _run(steps):
    for step in steps:
        step()


def _two_stage(n_tiles, first, second, init=None):
    i = pl.program_id(0)
    if init is not None:
        pl.when(i == 0)(init)

    @pl.when(i > 0)
    def _():
        _run(second())

    @pl.when(i < n_tiles)
    def _():
        _run(first())


def _cur_tile(n_tiles):
    return lambda i: jnp.minimum(i, n_tiles - 1)


def _prev_tile():
    return lambda i: jnp.maximum(i - 1, 0)


def _row_spec(tile_of):
    return pl.BlockSpec((ROW_TILE, D_MODEL), lambda i: (tile_of(i), 0))


def _residue_major_specs(batch, seq, dtype, tile_of):
    tiles_per_seq = seq // ROW_TILE
    specs, shapes = [], []
    for d in ATTN_DILATIONS:
        specs.append(pl.BlockSpec((1, d, ROW_TILE // d, GROUP_WIDTH),
                                  lambda i: (tile_of(i) // tiles_per_seq, 0, tile_of(i) % tiles_per_seq, 0)))
        shapes.append(jax.ShapeDtypeStruct((batch, d, seq // d, GROUP_WIDTH), dtype))
    return specs, shapes


def _ffn_weight_specs():
    return [_const_spec((1, D_MODEL)), _const_spec((1, D_MODEL)),
            _const_spec((D_MODEL, D_FF)), _const_spec((D_MODEL, D_FF)), _const_spec((D_FF, D_MODEL))]


_HANDOVER = pltpu.VMEM((ROW_TILE, D_MODEL), _F32)
_SLAB = pltpu.VMEM((QKV_TILES, ROW_TILE, LANES), _F32)
_FFN_WORK = [pltpu.VMEM((ROW_TILE, D_MODEL), _BF16), pltpu.VMEM((ROW_TILE, D_MODEL), _F32)]
BF16_SUBLANES = 16


def _hand_over(hand_ref):
    def put(y):
        hand_ref[...] = y

    def get():
        return hand_ref[...]

    return put, get


def _split(refs, *counts):
    out, at = [], 0
    for n in counts:
        out.append(refs[at:at + n])
        at += n
    assert at == len(refs)
    return out


def _cast_specs(weights, n_tiles):
    in_specs, out_specs, out_shapes = [], [], []
    for w, lead, _ in weights:
        rows, cols = w.shape[len(lead):]
        n_blocks = max(n for n in range(1, n_tiles + 1) if rows % (n * BF16_SUBLANES) == 0)

        def block(i, n_blocks=n_blocks):
            return (jnp.minimum(i, n_blocks - 1), 0)

        in_specs.append(pl.BlockSpec((None,) * len(lead) + (rows // n_blocks, cols),
                                     lambda i, lead=lead, block=block: lead + block(i)))
        out_specs.append(pl.BlockSpec((rows // n_blocks, cols), block))
        out_shapes.append(jax.ShapeDtypeStruct((rows, cols), _BF16))
    return in_specs, out_specs, out_shapes


def _cast_blocks(src_refs, dst_refs, interleave):
    for src, dst, flag in zip(src_refs, dst_refs, interleave):
        if flag:
            for j in range(src.shape[1] // LANES):
                cols = slice(j * LANES, (j + 1) * LANES)
                dst[:, cols] = _pair_interleaved(src[:, cols]).astype(_BF16)
        else:
            dst[...] = src[...].astype(_BF16)


def _ffn_pool_kernel(n_tiles, tiles_per_seq, interleave, h_ref, *refs):
    n_cast = len(interleave)
    ffn_w, pool_w, cast_src, (o_ref,), cast_dst, (hand_ref, ubuf_ref, *work) = _split(
        refs, 5, 6, n_cast, 1, n_cast, 4)
    put, get = _hand_over(hand_ref)
    j = (pl.program_id(0) - 1) % tiles_per_seq

    def write(y):
        o_ref[...] = y

    def init():
        ubuf_ref[0:POOL_HALO, :] = jnp.zeros((POOL_HALO, D_MODEL), _F32)

    _cast_blocks(cast_src, cast_dst, interleave)
    _two_stage(n_tiles,
               lambda: _ffn_steps(lambda: h_ref[...], put, *ffn_w, *work),
               lambda: _pool_steps(get, write, j, *pool_w, ubuf_ref),
               init=init)


def _ffn_pool(h, ffn_w, pool_w, seq, to_cast):
    m = h.shape[0]
    n_tiles = m // ROW_TILE
    n_grp = len(POOL_WINDOWS)
    cast_in, cast_out, cast_shapes = _cast_specs(to_cast, n_tiles)
    outs = pl.pallas_call(
        functools.partial(_ffn_pool_kernel, n_tiles, seq // ROW_TILE, tuple(f for _, _, f in to_cast)),
        grid=(n_tiles + 1,),
        in_specs=[_row_spec(_cur_tile(n_tiles))] + _ffn_weight_specs() + [
            _const_spec((1, D_MODEL)), _const_spec((1, D_MODEL)), _const_spec((D_MODEL, D_MODEL)),
            _const_spec((n_grp, POOL_GROUP_DIM, POOL_GROUP_DIM)), _const_spec((1, D_MODEL)),
            _const_spec((D_MODEL, D_MODEL))] + cast_in,
        out_specs=[_row_spec(_prev_tile())] + cast_out,
        out_shape=[jax.ShapeDtypeStruct((m, D_MODEL), _F32)] + cast_shapes,
        scratch_shapes=[_HANDOVER, pltpu.VMEM((POOL_HALO + ROW_TILE, D_MODEL), _F32)] + _FFN_WORK,
        compiler_params=_params(("arbitrary",)),
        name="ffn_pool",
    )(h, *ffn_w, *pool_w, *(w for w, _, _ in to_cast))
    return outs[0], outs[1:]


def _ffn_and_keep(h_ref, o_ref, put, ffn_w, work):
    def write(y):
        o_ref[...] = y
        put(y)

    return lambda: _ffn_steps(lambda: h_ref[...], write, *ffn_w, *work)


def _table_spec():
    return pl.BlockSpec((ROW_TILE, LANES), lambda i: (jnp.maximum(i - 1, 0), 0))


def _ffn_kv_kernel(n_tiles, interleave, h_ref, pos_ref, *refs):
    n_cast = len(interleave)
    (ffn_w, kv_w, cast_src, (o_ref,), k_refs, v_refs, (cos_ref, sin_ref), cast_dst,
     (hand_ref, slab_ref, xn2_ref, *work)) = _split(refs, 5, 4, n_cast, 1, N_GROUPS, N_GROUPS, 2, n_cast, 5)
    put, get = _hand_over(hand_ref)
    _cast_blocks(cast_src, cast_dst, interleave)
    _two_stage(n_tiles, _ffn_and_keep(h_ref, o_ref, put, ffn_w, work),
               lambda: _kv_steps(get, pos_ref, *kv_w, k_refs, v_refs, cos_ref, sin_ref, slab_ref, xn2_ref))


def _ffn_kv(h, pos, ffn_w, kv_w, batch, seq, to_cast):
    m = h.shape[0]
    n_tiles = m // ROW_TILE
    specs, shapes = _residue_major_specs(batch, seq, _BF16, _prev_tile())
    cast_in, cast_out, cast_shapes = _cast_specs(to_cast, n_tiles)
    table_shape = jax.ShapeDtypeStruct((m, LANES), _F32)
    outs = pl.pallas_call(
        functools.partial(_ffn_kv_kernel, n_tiles, tuple(f for _, _, f in to_cast)),
        grid=(n_tiles + 1,),
        in_specs=[_row_spec(_cur_tile(n_tiles)), pl.BlockSpec((ROW_TILE, 1), lambda i: (jnp.maximum(i - 1, 0), 0))]
        + _ffn_weight_specs() + [_const_spec((1, D_MODEL)), _const_spec((1, LANES)),
                                 _const_spec((D_MODEL, QKV_WIDTH)), _const_spec((D_MODEL, QKV_WIDTH))] + cast_in,
        out_specs=[_row_spec(_cur_tile(n_tiles))] + specs * 2 + [_table_spec()] * 2 + cast_out,
        out_shape=[jax.ShapeDtypeStruct((m, D_MODEL), _F32)] + shapes * 2 + [table_shape] * 2 + cast_shapes,
        scratch_shapes=[_HANDOVER, _SLAB, _FFN_WORK[0]] + _FFN_WORK,
        compiler_params=_params(("arbitrary",)),
        name="ffn_kv",
    )(h, pos, *ffn_w, *kv_w, *(w for w, _, _ in to_cast))
    return outs[0], outs[1:4], outs[4:7], outs[7:9], outs[9:]


def _ffn_q_kernel(n_tiles, interleave, h_ref, cos_ref, sin_ref, *refs):
    n_cast = len(interleave)
    ffn_w, q_w, cast_src, (o_ref,), q_refs, cast_dst, (hand_ref, slab_ref, xn2_ref, *work) = _split(
        refs, 5, 2, n_cast, 1, N_GROUPS, n_cast, 5)
    put, get = _hand_over(hand_ref)
    _cast_blocks(cast_src, cast_dst, interleave)
    _two_stage(n_tiles, _ffn_and_keep(h_ref, o_ref, put, ffn_w, work),
               lambda: _q_steps(get, cos_ref, sin_ref, *q_w, q_refs, slab_ref, xn2_ref))


def _ffn_q(h, tables, ffn_w, q_w, batch, seq, to_cast):
    m = h.shape[0]
    n_tiles = m // ROW_TILE
    specs, shapes = _residue_major_specs(batch, seq, _BF16, _prev_tile())
    cast_in, cast_out, cast_shapes = _cast_specs(to_cast, n_tiles)
    outs = pl.pallas_call(
        functools.partial(_ffn_q_kernel, n_tiles, tuple(f for _, _, f in to_cast)),
        grid=(n_tiles + 1,),
        in_specs=[_row_spec(_cur_tile(n_tiles))] + [_table_spec()] * 2 + _ffn_weight_specs()
        + [_const_spec((1, D_MODEL)), _const_spec((D_MODEL, QKV_WIDTH))] + cast_in,
        out_specs=[_row_spec(_cur_tile(n_tiles))] + specs + cast_out,
        out_shape=[jax.ShapeDtypeStruct((m, D_MODEL), _F32)] + shapes + cast_shapes,
        scratch_shapes=[_HANDOVER, _SLAB, _FFN_WORK[0]] + _FFN_WORK,
        compiler_params=_params(("arbitrary",)),
        name="ffn_q",
    )(h, *tables, *ffn_w, *q_w, *(w for w, _, _ in to_cast))
    return outs[0], outs[1:4], outs[4:]


def _merge_ffn_kernel(n_tiles, h_ref, *refs):
    merge_w, o_refs, l_refs, ffn_w, (out_ref, hand_ref, nat_ref, *work) = (
        refs[:2], refs[2:5], refs[5:8], refs[8:13], refs[13:])
    put, get = _hand_over(hand_ref)

    def write(y):
        out_ref[...] = y

    _two_stage(n_tiles,
               lambda: _merge_steps(lambda: h_ref[...], put, *merge_w, o_refs, l_refs, nat_ref),
               lambda: _ffn_steps(get, write, *ffn_w, *work))


def _merge_ffn(h, merge_w, outs, lses, ffn_w, batch, seq):
    m = h.shape[0]
    n_tiles = m // ROW_TILE
    specs, _ = _residue_major_specs(batch, seq, _F32, _cur_tile(n_tiles))
    return pl.pallas_call(
        functools.partial(_merge_ffn_kernel, n_tiles),
        grid=(n_tiles + 1,),
        in_specs=[_row_spec(_cur_tile(n_tiles)), _const_spec((1, D_MODEL)), _const_spec((GROUP_WIDTH, D_MODEL))]
        + specs * 2 + _ffn_weight_specs(),
        out_specs=_row_spec(_prev_tile()),
        out_shape=jax.ShapeDtypeStruct((m, D_MODEL), _F32),
        scratch_shapes=[_HANDOVER, pltpu.VMEM((4, PAIRS_PER_GROUP, ROW_TILE, LANES), _F32)] + _FFN_WORK,
        compiler_params=_params(("arbitrary",)),
        name="merge_ffn",
    )(h, *merge_w, *outs, *lses, *ffn_w)


def _attn_kernel(q_ref, kp_ref, kc_ref, vp_ref, vc_ref, o_ref, lse_ref):
    step = pl.program_id(1)
    n_blk = ATTN_Q_TILE // ATTN_BLK
    win = 2 * ATTN_BLK
    qi = lax.broadcasted_iota(jnp.int32, (win, win), 0) % ATTN_BLK
    kj = lax.broadcasted_iota(jnp.int32, (win, win), 1)
    band = jnp.where((kj >= qi) & (kj <= qi + ATTN_BLK), 0.0, NEG_BIG)
    first_lo = jnp.maximum(qi, jnp.where(step == 0, ATTN_BLK, 0))
    band_first = jnp.where((kj >= first_lo) & (kj <= qi + ATTN_BLK), 0.0, NEG_BIG)
    q_lane = lax.broadcasted_iota(jnp.int32, (ATTN_BLK, LANES), 1)
    q_is_a = (q_lane % HEAD_DIM) < HEAD_DIM // 2
    v_lane = lax.broadcasted_iota(jnp.int32, (win, LANES), 1)
    v_is_a = v_lane < HEAD_DIM
    ones_a = jnp.where(v_is_a, 1.0, 0.0).astype(_BF16)
    ones_b = jnp.where(v_is_a, 0.0, 1.0).astype(_BF16)
    o_is_a = q_lane < HEAD_DIM
    nt = (((1,), (1,)), ((), ()))
    work = [(j, p) for j in range(n_blk) for p in range(PAIRS_PER_GROUP)]

    def window(prev_ref, cur_ref, j, p):
        pair = slice(p * LANES, (p + 1) * LANES)
        if j == 0:
            return jnp.concatenate([prev_ref[0, :, pair], cur_ref[0, 0:ATTN_BLK, pair]], axis=0)
        return cur_ref[0, (j - 1) * ATTN_BLK:(j + 1) * ATTN_BLK, pair]

    def scores(j, p):
        q = q_ref[0, j * ATTN_BLK:(j + 1) * ATTN_BLK, p * LANES:(p + 1) * LANES]
        zero = jnp.zeros_like(q)
        q2 = jnp.concatenate([jnp.where(q_is_a, q, zero), jnp.where(q_is_a, zero, q)], axis=0)
        s = lax.dot_general(q2, window(kp_ref, kc_ref, j, p), nt, preferred_element_type=_F32)
        return s + (band_first if j == 0 else band)

    def finish(j, p, s):
        m = jnp.max(s, axis=-1, keepdims=True)
        prob = jnp.exp(s - m).astype(_BF16)
        prob = jnp.concatenate([prob[:ATTN_BLK], prob[ATTN_BLK:]], axis=1)
        v = window(vp_ref, vc_ref, j, p)
        zero = jnp.zeros_like(v)
        v2 = jnp.concatenate([jnp.concatenate([jnp.where(v_is_a, v, zero), ones_a], axis=1),
                              jnp.concatenate([jnp.where(v_is_a, zero, v), ones_b], axis=1)], axis=0)
        r = _dot(prob, v2)
        den = r[:, LANES:]
        rows = slice(j * ATTN_BLK, (j + 1) * ATTN_BLK)
        cols = slice(p * LANES, (p + 1) * LANES)
        o_ref[0, rows, cols] = r[:, :LANES] * (1.0 / den)
        lse_ref[0, rows, cols] = jnp.where(o_is_a, m[:ATTN_BLK], m[ATTN_BLK:]) + jnp.log(den)

    ahead = 1
    pending = [scores(*work[n]) for n in range(ahead)]
    for n, (j, p) in enumerate(work):
        s = pending.pop(0)
        if n + ahead < len(work):
            pending.append(scores(*work[n + ahead]))
        finish(j, p, s)


def _attn_group(q, k, v):
    batch, d, rows, _ = q.shape
    n_seq = batch * d
    q, k, v = (a.reshape(n_seq, rows, GROUP_WIDTH) for a in (q, k, v))
    blocks_per_step = ATTN_Q_TILE // ATTN_BLK

    def cur(s, i):
        return (s, i, 0)

    def prev(s, i):
        return (s, jnp.maximum(i * blocks_per_step - 1, 0), 0)

    cur_spec = pl.BlockSpec((1, ATTN_Q_TILE, GROUP_WIDTH), cur)
    prev_spec = pl.BlockSpec((1, ATTN_BLK, GROUP_WIDTH), prev)
    out_shape = jax.ShapeDtypeStruct((n_seq, rows, GROUP_WIDTH), _F32)
    o, lse = pl.pallas_call(
        _attn_kernel,
        grid=(n_seq, rows // ATTN_Q_TILE),
        in_specs=[cur_spec, prev_spec, cur_spec, prev_spec, cur_spec],
        out_specs=[cur_spec, cur_spec],
        out_shape=[out_shape, out_shape],
        compiler_params=_params(("parallel", "arbitrary")),
        name=f"window_attn_d{d}",
    )(q, k, k, v, v)
    shape = (batch, d, rows, GROUP_WIDTH)
    return o.reshape(shape), lse.reshape(shape)


def kernel(x, positions, norm_gain, ffn_w_gate, ffn_w_up, ffn_w_down, pool_w_in, pool_w_group, pool_scale,
           pool_w_out, kv_norm_gain, w_k, w_v, attn_w_q, attn_w_o):
    batch, seq, _ = x.shape
    m = batch * seq
    gains = norm_gain.reshape(2, 6, 1, D_MODEL)
    bf = lambda w: w.astype(_BF16)

    def ffn_f32(layer, slot):
        return tuple((w, (layer, slot), False) for w in (ffn_w_gate, ffn_w_up, ffn_w_down))

    def ffn_gains(layer, slot):
        return (gains[layer, 4 * slot], gains[layer, 4 * slot + 1])

    pos = positions.astype(_F32).reshape(m, 1)
    inv_freq = ROPE_THETA ** (-jnp.arange(0, HEAD_DIM, 2, dtype=_F32) / HEAD_DIM)
    invf = jnp.tile(inv_freq, LANES // (HEAD_DIM // 2)).reshape(1, LANES)

    h = x.reshape(m, D_MODEL)
    pool_w = (gains[0, 2], gains[0, 3], bf(pool_w_in[0]), bf(pool_w_group[0]),
              pool_scale[0].reshape(1, D_MODEL), bf(pool_w_out[0]))
    ffn00 = tuple(bf(w[lead]) for w, lead, _ in ffn_f32(0, 0))
    h, (*ffn01, wk_bf, wv_bf) = _ffn_pool(h, ffn_gains(0, 0) + ffn00, pool_w, seq,
                                          to_cast=ffn_f32(0, 1) + ((w_k, (), True), (w_v, (), False)))
    kv_w = (kv_norm_gain.reshape(1, D_MODEL), invf, wk_bf, wv_bf)
    h, k_sh, v_sh, tables, (*ffn10, wq_bf) = _ffn_kv(h, pos, ffn_gains(0, 1) + tuple(ffn01), kv_w, batch, seq,
                                                     to_cast=ffn_f32(1, 0) + ((attn_w_q, (0,), True),))
    h, q, (*ffn11, wo_bf) = _ffn_q(h, tables, ffn_gains(1, 0) + tuple(ffn10), (gains[1, 2], wq_bf), batch, seq,
                                   to_cast=ffn_f32(1, 1) + ((attn_w_o, (0,), False),))
    outs, lses = [], []
    for g in range(N_GROUPS):
        o, lse = _attn_group(q[g], k_sh[g], v_sh[g])
        outs.append(o)
        lses.append(lse)
    h = _merge_ffn(h, (gains[1, 3], wo_bf), outs, lses, ffn_gains(1, 1) + tuple(ffn11), batch, seq)
    return h.reshape(batch, seq, D_MODEL)
```

```python
import functools

import jax
import jax.numpy as jnp
from jax import lax
from jax.experimental import pallas as pl
from jax.experimental.pallas import tpu as pltpu

D_MODEL = 1024
D_FF = 2816
POOL_WINDOWS = (2, 4, 8, 16)
POOL_GROUP_DIM = D_MODEL // len(POOL_WINDOWS)
POOL_HALO = 16
ATTN_DILATIONS = (1, 4, 16)
N_GROUPS = len(ATTN_DILATIONS)
ATTN_BLK = 128
HEAD_DIM = 64
HEADS_PER_GROUP = 8
GROUP_WIDTH = HEADS_PER_GROUP * HEAD_DIM
QKV_WIDTH = N_GROUPS * GROUP_WIDTH
ROPE_THETA = 10000.0
RMS_EPS = 1e-6
NEG_BIG = -1e30

LANES = 128
PAIRS_PER_GROUP = GROUP_WIDTH // LANES
QKV_TILES = QKV_WIDTH // LANES
ROW_TILE = 512
ATTN_Q_TILE = 1024
FF_CHUNK = 256
VMEM_LIMIT_BYTES = 56 * 1024 * 1024

_F32 = jnp.float32
_BF16 = jnp.bfloat16


def _rms_norm(x, gain):
    ms = jnp.mean(x * x, axis=-1, keepdims=True)
    return x * lax.rsqrt(ms + RMS_EPS) * gain


def _dot(a, b):
    return jnp.dot(a, b, preferred_element_type=_F32)


def _const_spec(shape):
    return pl.BlockSpec(shape, lambda *_: (0,) * len(shape), pipeline_mode=pl.Buffered(1))


def _params(semantics):
    return pltpu.CompilerParams(dimension_semantics=semantics, vmem_limit_bytes=VMEM_LIMIT_BYTES)


def _ffn_steps(read, write, gpre_ref, gpost_ref, wg_ref, wu_ref, wd_ref, xn_ref, acc_ref):
    def start():
        xn_ref[...] = _rms_norm(read(), gpre_ref[...]).astype(_BF16)

    def chunk(c):
        def run():
            hi = min(c + FF_CHUNK, D_FF)
            g = _dot(xn_ref[...], wg_ref[:, c:hi])
            u = _dot(xn_ref[...], wu_ref[:, c:hi])
            a = (g * jax.nn.sigmoid(g) * u).astype(_BF16)
            down = _dot(a, wd_ref[c:hi, :])
            if c == 0:
                acc_ref[...] = down
            else:
                acc_ref[...] += down
        return run

    def end():
        write(read() + 0.5 * _rms_norm(acc_ref[...], gpost_ref[...]))

    return [start] + [chunk(c) for c in range(0, D_FF, FF_CHUNK)] + [end]


def _pool_steps(read, write, j, gpre_ref, gpost_ref, win_ref, wgrp_ref, scale_ref, wout_ref, ubuf_ref):
    st = {"y": []}

    def start():
        st["x"] = read()
        hm = _rms_norm(st["x"], gpre_ref[...]).astype(_BF16)
        st["u"] = _dot(hm, win_ref[...])
        ubuf_ref[0:POOL_HALO, :] = jnp.where(j == 0, 0.0, ubuf_ref[0:POOL_HALO, :])
        ubuf_ref[POOL_HALO:, :] = st["u"]

    def group(g, w):
        def run():
            tm = st["x"].shape[0]
            cols = slice(g * POOL_GROUP_DIM, (g + 1) * POOL_GROUP_DIM)
            t = j * tm + lax.broadcasted_iota(jnp.int32, (tm, 1), 0)
            s = ubuf_ref[:, cols]
            k = 1
            while k < w:
                s = s + pltpu.roll(s, k, axis=0)
                k *= 2
            count = jnp.minimum(t + 1, w).astype(_F32)
            p = s[POOL_HALO:] / count - st["u"][:, cols]
            st["y"].append((_dot(p.astype(_BF16), wgrp_ref[g]) * scale_ref[:, cols]).astype(_BF16))
        return run

    def end():
        tm = st["x"].shape[0]
        mix = _dot(jnp.concatenate(st["y"], axis=-1), wout_ref[...])
        ubuf_ref[0:POOL_HALO, :] = st["u"][tm - POOL_HALO:, :]
        write(st["x"] + _rms_norm(mix, gpost_ref[...]))

    return [start] + [group(g, w) for g, w in enumerate(POOL_WINDOWS)] + [end]


def _rope_tables(c, s):
    lane = lax.broadcasted_iota(jnp.int32, c.shape, 1)
    return c, jnp.where(lane < LANES // 2, -s, s)


def _rope_tile(xt, tables):
    c, s_signed = tables
    return xt * c + pltpu.roll(xt, LANES // 2, axis=1) * s_signed


def _pair_interleaved(t):
    half = HEAD_DIM // 2
    block = lax.broadcasted_iota(jnp.int32, t.shape, 1) // half
    return jnp.where(block == 1, pltpu.roll(t, LANES - half, axis=1),
                     jnp.where(block == 2, pltpu.roll(t, half, axis=1), t))


def _projection_steps(st, xn_ref, w_ref, slab_ref, out_refs, rope=False, scale=None):
    def project(g):
        def run():
            y = _dot(xn_ref[...], w_ref[:, g * GROUP_WIDTH:(g + 1) * GROUP_WIDTH])
            for p in range(PAIRS_PER_GROUP):
                yt = y[:, p * LANES:(p + 1) * LANES]
                if rope:
                    yt = _rope_tile(yt, st["tables"])
                if scale is not None:
                    yt = yt * scale
                slab_ref[g * PAIRS_PER_GROUP + p] = yt
        return run

    def scatter(g, d):
        def run():
            n = slab_ref.shape[1] // d
            for r in range(d):
                for p in range(PAIRS_PER_GROUP):
                    rows = slab_ref[g * PAIRS_PER_GROUP + p, pl.ds(r, n, stride=d), :]
                    out_refs[g][0, r, :, p * LANES:(p + 1) * LANES] = rows.astype(_BF16)
        return run

    steps = []
    for g, d in enumerate(ATTN_DILATIONS):
        steps += [project(g), scatter(g, d)]
    return steps


def _kv_steps(read, pos_ref, gain_ref, invf_ref, wk_ref, wv_ref, k_refs, v_refs, cos_ref, sin_ref, slab_ref, xn_ref):
    st = {}

    def start():
        xn_ref[...] = _rms_norm(read(), gain_ref[...]).astype(_BF16)
        ang = pos_ref[...] * invf_ref[...]
        cos_ref[...] = jnp.cos(ang)
        sin_ref[...] = jnp.sin(ang)
        st["tables"] = _rope_tables(cos_ref[...], sin_ref[...])

    return ([start] + _projection_steps(st, xn_ref, wk_ref, slab_ref, k_refs, rope=True)
            + _projection_steps(st, xn_ref, wv_ref, slab_ref, v_refs))


def _q_steps(read, cos_ref, sin_ref, gain_ref, wq_ref, q_refs, slab_ref, xn_ref):
    st = {}

    def start():
        xn_ref[...] = _rms_norm(read(), gain_ref[...]).astype(_BF16)
        st["tables"] = _rope_tables(cos_ref[...], sin_ref[...])

    return [start] + _projection_steps(st, xn_ref, wq_ref, slab_ref, q_refs, rope=True, scale=HEAD_DIM ** -0.5)


def _merge_steps(read, write, gpost_ref, wo_ref, o_refs, l_refs, nat_ref):
    st = {"o": []}
    permuted = ((o_refs[1], ATTN_DILATIONS[1]), (l_refs[1], ATTN_DILATIONS[1]),
                (o_refs[2], ATTN_DILATIONS[2]), (l_refs[2], ATTN_DILATIONS[2]))

    def lane_tile(p):
        def run():
            cols = slice(p * LANES, (p + 1) * LANES)
            for a, (ref, d) in enumerate(permuted):
                n = nat_ref.shape[2] // d
                for r in range(d):
                    nat_ref[a, p, pl.ds(r, n, stride=d), :] = ref[0, r, :, cols]
            o0, l0 = o_refs[0][0, 0, :, cols], l_refs[0][0, 0, :, cols]
            o1, l1, o2, l2 = (nat_ref[a, p] for a in range(4))
            top = jnp.maximum(jnp.maximum(l0, l1), l2)
            e0, e1, e2 = jnp.exp(l0 - top), jnp.exp(l1 - top), jnp.exp(l2 - top)
            st["o"].append(((e0 * o0 + e1 * o1 + e2 * o2) / (e0 + e1 + e2)).astype(_BF16))
        return run

    def end():
        mix = _dot(jnp.concatenate(st["o"], axis=-1), wo_ref[...])
        write(read() + _rms_norm(mix, gpost_ref[...]))

    return [lane_tile(p) for p in range(PAIRS_PER_GROUP)] + [end]


def _run(steps):
    for step in steps:
        step()


def _two_stage(n_tiles, first, second, init=None):
    i = pl.program_id(0)
    if init is not None:
        pl.when(i == 0)(init)

    @pl.when(i > 0)
    def _():
        _run(second())

    @pl.when(i < n_tiles)
    def _():
        _run(first())


def _cur_tile(n_tiles):
    return lambda i: jnp.minimum(i, n_tiles - 1)


def _prev_tile():
    return lambda i: jnp.maximum(i - 1, 0)


def _row_spec(tile_of):
    return pl.BlockSpec((ROW_TILE, D_MODEL), lambda i: (tile_of(i), 0))


def _residue_major_specs(batch, seq, dtype, tile_of):
    tiles_per_seq = seq // ROW_TILE
    specs, shapes = [], []
    for d in ATTN_DILATIONS:
        specs.append(pl.BlockSpec((1, d, ROW_TILE // d, GROUP_WIDTH),
                                  lambda i: (tile_of(i) // tiles_per_seq, 0, tile_of(i) % tiles_per_seq, 0)))
        shapes.append(jax.ShapeDtypeStruct((batch, d, seq // d, GROUP_WIDTH), dtype))
    return specs, shapes


def _ffn_weight_specs():
    return [_const_spec((1, D_MODEL)), _const_spec((1, D_MODEL)),
            _const_spec((D_MODEL, D_FF)), _const_spec((D_MODEL, D_FF)), _const_spec((D_FF, D_MODEL))]


_HANDOVER = pltpu.VMEM((ROW_TILE, D_MODEL), _F32)
_SLAB = pltpu.VMEM((QKV_TILES, ROW_TILE, LANES), _F32)
_FFN_WORK = [pltpu.VMEM((ROW_TILE, D_MODEL), _BF16), pltpu.VMEM((ROW_TILE, D_MODEL), _F32)]
BF16_SUBLANES = 16


def _hand_over(hand_ref):
    def put(y):
        hand_ref[...] = y

    def get():
        return hand_ref[...]

    return put, get


def _split(refs, *counts):
    out, at = [], 0
    for n in counts:
        out.append(refs[at:at + n])
        at += n
    assert at == len(refs)
    return out


def _cast_specs(weights, n_tiles):
    in_specs, out_specs, out_shapes = [], [], []
    for w, lead, _ in weights:
        rows, cols = w.shape[len(lead):]
        n_blocks = max(n for n in range(1, n_tiles + 1) if rows % (n * BF16_SUBLANES) == 0)

        def block(i, n_blocks=n_blocks):
            return (jnp.minimum(i, n_blocks - 1), 0)

        in_specs.append(pl.BlockSpec((None,) * len(lead) + (rows // n_blocks, cols),
                                     lambda i, lead=lead, block=block: lead + block(i)))
        out_specs.append(pl.BlockSpec((rows // n_blocks, cols), block))
        out_shapes.append(jax.ShapeDtypeStruct((rows, cols), _BF16))
    return in_specs, out_specs, out_shapes


def _cast_blocks(src_refs, dst_refs, interleave):
    for src, dst, flag in zip(src_refs, dst_refs, interleave):
        if flag:
            for j in range(src.shape[1] // LANES):
                cols = slice(j * LANES, (j + 1) * LANES)
                dst[:, cols] = _pair_interleaved(src[:, cols]).astype(_BF16)
        else:
            dst[...] = src[...].astype(_BF16)


def _ffn_pool_kernel(n_tiles, tiles_per_seq, interleave, h_ref, *refs):
    n_cast = len(interleave)
    ffn_w, pool_w, cast_src, (o_ref,), cast_dst, (hand_ref, ubuf_ref, *work) = _split(
        refs, 5, 6, n_cast, 1, n_cast, 4)
    put, get = _hand_over(hand_ref)
    j = (pl.program_id(0) - 1) % tiles_per_seq

    def write(y):
        o_ref[...] = y

    def init():
        ubuf_ref[0:POOL_HALO, :] = jnp.zeros((POOL_HALO, D_MODEL), _F32)

    _cast_blocks(cast_src, cast_dst, interleave)
    _two_stage(n_tiles,
               lambda: _ffn_steps(lambda: h_ref[...], put, *ffn_w, *work),
               lambda: _pool_steps(get, write, j, *pool_w, ubuf_ref),
               init=init)


def _ffn_pool(h, ffn_w, pool_w, seq, to_cast):
    m = h.shape[0]
    n_tiles = m // ROW_TILE
    n_grp = len(POOL_WINDOWS)
    cast_in, cast_out, cast_shapes = _cast_specs(to_cast, n_tiles)
    outs = pl.pallas_call(
        functools.partial(_ffn_pool_kernel, n_tiles, seq // ROW_TILE, tuple(f for _, _, f in to_cast)),
        grid=(n_tiles + 1,),
        in_specs=[_row_spec(_cur_tile(n_tiles))] + _ffn_weight_specs() + [
            _const_spec((1, D_MODEL)), _const_spec((1, D_MODEL)), _const_spec((D_MODEL, D_MODEL)),
            _const_spec((n_grp, POOL_GROUP_DIM, POOL_GROUP_DIM)), _const_spec((1, D_MODEL)),
            _const_spec((D_MODEL, D_MODEL))] + cast_in,
        out_specs=[_row_spec(_prev_tile())] + cast_out,
        out_shape=[jax.ShapeDtypeStruct((m, D_MODEL), _F32)] + cast_shapes,
        scratch_shapes=[_HANDOVER, pltpu.VMEM((POOL_HALO + ROW_TILE, D_MODEL), _F32)] + _FFN_WORK,
        compiler_params=_params(("arbitrary",)),
        name="ffn_pool",
    )(h, *ffn_w, *pool_w, *(w for w, _, _ in to_cast))
    return outs[0], outs[1:]


def _ffn_and_keep(h_ref, o_ref, put, ffn_w, work):
    def write(y):
        o_ref[...] = y
        put(y)

    return lambda: _ffn_steps(lambda: h_ref[...], write, *ffn_w, *work)


def _table_spec():
    return pl.BlockSpec((ROW_TILE, LANES), lambda i: (jnp.maximum(i - 1, 0), 0))


def _ffn_kv_kernel(n_tiles, interleave, h_ref, pos_ref, *refs):
    n_cast = len(interleave)
    (ffn_w, kv_w, cast_src, (o_ref,), k_refs, v_refs, (cos_ref, sin_ref), cast_dst,
     (hand_ref, slab_ref, xn2_ref, *work)) = _split(refs, 5, 4, n_cast, 1, N_GROUPS, N_GROUPS, 2, n_cast, 5)
    put, get = _hand_over(hand_ref)
    _cast_blocks(cast_src, cast_dst, interleave)
    _two_stage(n_tiles, _ffn_and_keep(h_ref, o_ref, put, ffn_w, work),
               lambda: _kv_steps(get, pos_ref, *kv_w, k_refs, v_refs, cos_ref, sin_ref, slab_ref, xn2_ref))


def _ffn_kv(h, pos, ffn_w, kv_w, batch, seq, to_cast):
    m = h.shape[0]
    n_tiles = m // ROW_TILE
    specs, shapes = _residue_major_specs(batch, seq, _BF16, _prev_tile())
    cast_in, cast_out, cast_shapes = _cast_specs(to_cast, n_tiles)
    table_shape = jax.ShapeDtypeStruct((m, LANES), _F32)
    outs = pl.pallas_call(
        functools.partial(_ffn_kv_kernel, n_tiles, tuple(f for _, _, f in to_cast)),
        grid=(n_tiles + 1,),
        in_specs=[_row_spec(_cur_tile(n_tiles)), pl.BlockSpec((ROW_TILE, 1), lambda i: (jnp.maximum(i - 1, 0), 0))]
        + _ffn_weight_specs() + [_const_spec((1, D_MODEL)), _const_spec((1, LANES)),
                                 _const_spec((D_MODEL, QKV_WIDTH)), _const_spec((D_MODEL, QKV_WIDTH))] + cast_in,
        out_specs=[_row_spec(_cur_tile(n_tiles))] + specs * 2 + [_table_spec()] * 2 + cast_out,
        out_shape=[jax.ShapeDtypeStruct((m, D_MODEL), _F32)] + shapes * 2 + [table_shape] * 2 + cast_shapes,
        scratch_shapes=[_HANDOVER, _SLAB, _FFN_WORK[0]] + _FFN_WORK,
        compiler_params=_params(("arbitrary",)),
        name="ffn_kv",
    )(h, pos, *ffn_w, *kv_w, *(w for w, _, _ in to_cast))
    return outs[0], outs[1:4], outs[4:7], outs[7:9], outs[9:]


def _ffn_q_kernel(n_tiles, interleave, h_ref, cos_ref, sin_ref, *refs):
    n_cast = len(interleave)
    ffn_w, q_w, cast_src, (o_ref,), q_refs, cast_dst, (hand_ref, slab_ref, xn2_ref, *work) = _split(
        refs, 5, 2, n_cast, 1, N_GROUPS, n_cast, 5)
    put, get = _hand_over(hand_ref)
    _cast_blocks(cast_src, cast_dst, interleave)
    _two_stage(n_tiles, _ffn_and_keep(h_ref, o_ref, put, ffn_w, work),
               lambda: _q_steps(get, cos_ref, sin_ref, *q_w, q_refs, slab_ref, xn2_ref))


def _ffn_q(h, tables, ffn_w, q_w, batch, seq, to_cast):
    m = h.shape[0]
    n_tiles = m // ROW_TILE
    specs, shapes = _residue_major_specs(batch, seq, _BF16, _prev_tile())
    cast_in, cast_out, cast_shapes = _cast_specs(to_cast, n_tiles)
    outs = pl.pallas_call(
        functools.partial(_ffn_q_kernel, n_tiles, tuple(f for _, _, f in to_cast)),
        grid=(n_tiles + 1,),
        in_specs=[_row_spec(_cur_tile(n_tiles))] + [_table_spec()] * 2 + _ffn_weight_specs()
        + [_const_spec((1, D_MODEL)), _const_spec((D_MODEL, QKV_WIDTH))] + cast_in,
        out_specs=[_row_spec(_cur_tile(n_tiles))] + specs + cast_out,
        out_shape=[jax.ShapeDtypeStruct((m, D_MODEL), _F32)] + shapes + cast_shapes,
        scratch_shapes=[_HANDOVER, _SLAB, _FFN_WORK[0]] + _FFN_WORK,
        compiler_params=_params(("arbitrary",)),
        name="ffn_q",
    )(h, *tables, *ffn_w, *q_w, *(w for w, _, _ in to_cast))
    return outs[0], outs[1:4], outs[4:]


def _merge_ffn_kernel(n_tiles, h_ref, *refs):
    merge_w, o_refs, l_refs, ffn_w, (out_ref, hand_ref, nat_ref, *work) = (
        refs[:2], refs[2:5], refs[5:8], refs[8:13], refs[13:])
    put, get = _hand_over(hand_ref)

    def write(y):
        out_ref[...] = y

    _two_stage(n_tiles,
               lambda: _merge_steps(lambda: h_ref[...], put, *merge_w, o_refs, l_refs, nat_ref),
               lambda: _ffn_steps(get, write, *ffn_w, *work))


def _merge_ffn(h, merge_w, outs, lses, ffn_w, batch, seq):
    m = h.shape[0]
    n_tiles = m // ROW_TILE
    specs, _ = _residue_major_specs(batch, seq, _F32, _cur_tile(n_tiles))
    return pl.pallas_call(
        functools.partial(_merge_ffn_kernel, n_tiles),
        grid=(n_tiles + 1,),
        in_specs=[_row_spec(_cur_tile(n_tiles)), _const_spec((1, D_MODEL)), _const_spec((GROUP_WIDTH, D_MODEL))]
        + specs * 2 + _ffn_weight_specs(),
        out_specs=_row_spec(_prev_tile()),
        out_shape=jax.ShapeDtypeStruct((m, D_MODEL), _F32),
        scratch_shapes=[_HANDOVER, pltpu.VMEM((4, PAIRS_PER_GROUP, ROW_TILE, LANES), _F32)] + _FFN_WORK,
        compiler_params=_params(("arbitrary",)),
        name="merge_ffn",
    )(h, *merge_w, *outs, *lses, *ffn_w)


def _attn_kernel(q_ref, kp_ref, kc_ref, vp_ref, vc_ref, o_ref, lse_ref):
    step = pl.program_id(1)
    n_blk = q_ref.shape[1] // ATTN_BLK
    win = 2 * ATTN_BLK
    qi = lax.broadcasted_iota(jnp.int32, (win, win), 0) % ATTN_BLK
    kj = lax.broadcasted_iota(jnp.int32, (win, win), 1)
    band = jnp.where((kj >= qi) & (kj <= qi + ATTN_BLK), 0.0, NEG_BIG)
    first_lo = jnp.maximum(qi, jnp.where(step == 0, ATTN_BLK, 0))
    band_first = jnp.where((kj >= first_lo) & (kj <= qi + ATTN_BLK), 0.0, NEG_BIG)
    q_lane = lax.broadcasted_iota(jnp.int32, (ATTN_BLK, LANES), 1)
    q_is_a = (q_lane % HEAD_DIM) < HEAD_DIM // 2
    v_lane = lax.broadcasted_iota(jnp.int32, (win, LANES), 1)
    v_is_a = v_lane < HEAD_DIM
    ones_a = jnp.where(v_is_a, 1.0, 0.0).astype(_BF16)
    ones_b = jnp.where(v_is_a, 0.0, 1.0).astype(_BF16)
    o_is_a = q_lane < HEAD_DIM
    nt = (((1,), (1,)), ((), ()))
    work = [(j, p) for j in range(n_blk) for p in range(PAIRS_PER_GROUP)]

    def window(prev_ref, cur_ref, j, p):
        pair = slice(p * LANES, (p + 1) * LANES)
        if j == 0:
            return jnp.concatenate([prev_ref[0, :, pair], cur_ref[0, 0:ATTN_BLK, pair]], axis=0)
        return cur_ref[0, (j - 1) * ATTN_BLK:(j + 1) * ATTN_BLK, pair]

    def scores(j, p):
        q = q_ref[0, j * ATTN_BLK:(j + 1) * ATTN_BLK, p * LANES:(p + 1) * LANES]
        zero = jnp.zeros_like(q)
        q2 = jnp.concatenate([jnp.where(q_is_a, q, zero), jnp.where(q_is_a, zero, q)], axis=0)
        s = lax.dot_general(q2, window(kp_ref, kc_ref, j, p), nt, preferred_element_type=_F32)
        return s + (band_first if j == 0 else band)

    def finish(j, p, s):
        m = jnp.max(s, axis=-1, keepdims=True)
        prob = jnp.exp(s - m).astype(_BF16)
        prob = jnp.concatenate([prob[:ATTN_BLK], prob[ATTN_BLK:]], axis=1)
        v = window(vp_ref, vc_ref, j, p)
        zero = jnp.zeros_like(v)
        v2 = jnp.concatenate([jnp.concatenate([jnp.where(v_is_a, v, zero), ones_a], axis=1),
                              jnp.concatenate([jnp.where(v_is_a, zero, v), ones_b], axis=1)], axis=0)
        r = _dot(prob, v2)
        den = r[:, LANES:]
        rows = slice(j * ATTN_BLK, (j + 1) * ATTN_BLK)
        cols = slice(p * LANES, (p + 1) * LANES)
        o_ref[0, rows, cols] = r[:, :LANES] * (1.0 / den)
        lse_ref[0, rows, cols] = jnp.where(o_is_a, m[:ATTN_BLK], m[ATTN_BLK:]) + jnp.log(den)

    ahead = 1
    pending = [scores(*work[n]) for n in range(ahead)]
    for n, (j, p) in enumerate(work):
        s = pending.pop(0)
        if n + ahead < len(work):
            pending.append(scores(*work[n + ahead]))
        finish(j, p, s)


def _attn_group(q, k, v):
    batch, d, rows, _ = q.shape
    n_seq = batch * d
    q, k, v = (a.reshape(n_seq, rows, GROUP_WIDTH) for a in (q, k, v))
    q_tile = min(ATTN_Q_TILE, rows)
    blocks_per_step = q_tile // ATTN_BLK

    def cur(s, i):
        return (s, i, 0)

    def prev(s, i):
        return (s, jnp.maximum(i * blocks_per_step - 1, 0), 0)

    cur_spec = pl.BlockSpec((1, q_tile, GROUP_WIDTH), cur)
    prev_spec = pl.BlockSpec((1, ATTN_BLK, GROUP_WIDTH), prev)
    out_shape = jax.ShapeDtypeStruct((n_seq, rows, GROUP_WIDTH), _F32)
    o, lse = pl.pallas_call(
        _attn_kernel,
        grid=(n_seq, rows // q_tile),
        in_specs=[cur_spec, prev_spec, cur_spec, prev_spec, cur_spec],
        out_specs=[cur_spec, cur_spec],
        out_shape=[out_shape, out_shape],
        compiler_params=_params(("parallel", "arbitrary")),
        name=f"window_attn_d{d}",
    )(q, k, k, v, v)
    shape = (batch, d, rows, GROUP_WIDTH)
    return o.reshape(shape), lse.reshape(shape)


def kernel(x, positions, norm_gain, ffn_w_gate, ffn_w_up, ffn_w_down, pool_w_in, pool_w_group, pool_scale,
           pool_w_out, kv_norm_gain, w_k, w_v, attn_w_q, attn_w_o):
    batch, seq, _ = x.shape
    m = batch * seq
    gains = norm_gain.reshape(2, 6, 1, D_MODEL)
    bf = lambda w: w.astype(_BF16)

    def ffn_f32(layer, slot):
        return tuple((w, (layer, slot), False) for w in (ffn_w_gate, ffn_w_up, ffn_w_down))

    def ffn_gains(layer, slot):
        return (gains[layer, 4 * slot], gains[layer, 4 * slot + 1])

    pos = positions.astype(_F32).reshape(m, 1)
    inv_freq = ROPE_THETA ** (-jnp.arange(0, HEAD_DIM, 2, dtype=_F32) / HEAD_DIM)
    invf = jnp.tile(inv_freq, LANES // (HEAD_DIM // 2)).reshape(1, LANES)

    h = x.reshape(m, D_MODEL)
    pool_w = (gains[0, 2], gains[0, 3], bf(pool_w_in[0]), bf(pool_w_group[0]),
              pool_scale[0].reshape(1, D_MODEL), bf(pool_w_out[0]))
    ffn00 = tuple(bf(w[lead]) for w, lead, _ in ffn_f32(0, 0))
    h, (*ffn01, wk_bf, wv_bf) = _ffn_pool(h, ffn_gains(0, 0) + ffn00, pool_w, seq,
                                          to_cast=ffn_f32(0, 1) + ((w_k, (), True), (w_v, (), False)))
    kv_w = (kv_norm_gain.reshape(1, D_MODEL), invf, wk_bf, wv_bf)
    h, k_sh, v_sh, tables, (*ffn10, wq_bf) = _ffn_kv(h, pos, ffn_gains(0, 1) + tuple(ffn01), kv_w, batch, seq,
                                                     to_cast=ffn_f32(1, 0) + ((attn_w_q, (0,), True),))
    h, q, (*ffn11, wo_bf) = _ffn_q(h, tables, ffn_gains(1, 0) + tuple(ffn10), (gains[1, 2], wq_bf), batch, seq,
                                   to_cast=ffn_f32(1, 1) + ((attn_w_o, (0,), False),))
    outs, lses = [], []
    for g in range(N_GROUPS):
        o, lse = _attn_group(q[g], k_sh[g], v_sh[g])
        outs.append(o)
        lses.append(lse)
    h = _merge_ffn(h, (gains[1, 3], wo_bf), outs, lses, ffn_gains(1, 1) + tuple(ffn11), batch, seq)
    return h.reshape(batch, seq, D_MODEL)
```

```python
import functools

import jax
import jax.numpy as jnp
from jax import lax
from jax.experimental import pallas as pl
from jax.experimental.pallas import tpu as pltpu

D_MODEL = 1024
D_FF = 2816
POOL_WINDOWS = (2, 4, 8, 16)
POOL_GROUP_DIM = D_MODEL // len(POOL_WINDOWS)
POOL_HALO = 16
ATTN_DILATIONS = (1, 4, 16)
N_GROUPS = len(ATTN_DILATIONS)
ATTN_BLK = 128
HEAD_DIM = 64
HEADS_PER_GROUP = 8
GROUP_WIDTH = HEADS_PER_GROUP * HEAD_DIM
QKV_WIDTH = N_GROUPS * GROUP_WIDTH
ROPE_THETA = 10000.0
RMS_EPS = 1e-6
NEG_BIG = -1e30

LANES = 128
PAIRS_PER_GROUP = GROUP_WIDTH // LANES
QKV_TILES = QKV_WIDTH // LANES
ROW_TILE = 512
ATTN_Q_TILE = 1024
FF_CHUNK = 256
VMEM_LIMIT_BYTES = 56 * 1024 * 1024

_F32 = jnp.float32
_BF16 = jnp.bfloat16


def _rms_norm(x, gain):
    ms = jnp.mean(x * x, axis=-1, keepdims=True)
    return x * lax.rsqrt(ms + RMS_EPS) * gain


def _dot(a, b):
    return jnp.dot(a, b, preferred_element_type=_F32)


def _const_spec(shape):
    return pl.BlockSpec(shape, lambda *_: (0,) * len(shape), pipeline_mode=pl.Buffered(1))


def _params(semantics):
    return pltpu.CompilerParams(dimension_semantics=semantics, vmem_limit_bytes=VMEM_LIMIT_BYTES)


def _ffn_steps(read, write, gpre_ref, gpost_ref, wg_ref, wu_ref, wd_ref, xn_ref, acc_ref):
    def start():
        xn_ref[...] = _rms_norm(read(), gpre_ref[...]).astype(_BF16)

    def chunk(c):
        def run():
            hi = min(c + FF_CHUNK, D_FF)
            g = _dot(xn_ref[...], wg_ref[:, c:hi])
            u = _dot(xn_ref[...], wu_ref[:, c:hi])
            a = (g * jax.nn.sigmoid(g) * u).astype(_BF16)
            down = _dot(a, wd_ref[c:hi, :])
            if c == 0:
                acc_ref[...] = down
            else:
                acc_ref[...] += down
        return run

    def end():
        write(read() + 0.5 * _rms_norm(acc_ref[...], gpost_ref[...]))

    return [start] + [chunk(c) for c in range(0, D_FF, FF_CHUNK)] + [end]


def _pool_out_weight(wgrp_ref, scale_ref, wout_ref, weff_ref):
    for g in range(len(POOL_WINDOWS)):
        rows = slice(g * POOL_GROUP_DIM, (g + 1) * POOL_GROUP_DIM)
        weff_ref[rows, :] = jnp.dot(wgrp_ref[g] * scale_ref[:, rows], wout_ref[rows, :], preferred_element_type=_F32,
                                    precision=lax.Precision.HIGHEST).astype(_BF16)


def _pool_steps(read, write, j, gpre_ref, gpost_ref, win_ref, weff_ref, ubuf_ref):
    st = {"p": []}

    def start():
        st["x"] = read()
        hm = _rms_norm(st["x"], gpre_ref[...]).astype(_BF16)
        st["u"] = _dot(hm, win_ref[...])
        ubuf_ref[0:POOL_HALO, :] = jnp.where(j == 0, 0.0, ubuf_ref[0:POOL_HALO, :])
        ubuf_ref[POOL_HALO:, :] = st["u"]

    def group(g, w):
        def run():
            tm = st["x"].shape[0]
            cols = slice(g * POOL_GROUP_DIM, (g + 1) * POOL_GROUP_DIM)
            t = j * tm + lax.broadcasted_iota(jnp.int32, (tm, 1), 0)
            s = ubuf_ref[:, cols]
            k = 1
            while k < w:
                s = s + pltpu.roll(s, k, axis=0)
                k *= 2
            count = jnp.minimum(t + 1, w).astype(_F32)
            p = s[POOL_HALO:] / count - st["u"][:, cols]
            st["p"].append(p.astype(_BF16))
        return run

    def end():
        tm = st["x"].shape[0]
        mix = _dot(jnp.concatenate(st["p"], axis=-1), weff_ref[...])
        ubuf_ref[0:POOL_HALO, :] = st["u"][tm - POOL_HALO:, :]
        write(st["x"] + _rms_norm(mix, gpost_ref[...]))

    return [start] + [group(g, w) for g, w in enumerate(POOL_WINDOWS)] + [end]


def _rope_tables(c, s):
    lane = lax.broadcasted_iota(jnp.int32, c.shape, 1)
    return c, jnp.where(lane < LANES // 2, -s, s)


def _rope_tile(xt, tables):
    c, s_signed = tables
    return xt * c + pltpu.roll(xt, LANES // 2, axis=1) * s_signed


def _pair_interleaved(t):
    half = HEAD_DIM // 2
    block = lax.broadcasted_iota(jnp.int32, t.shape, 1) // half
    return jnp.where(block == 1, pltpu.roll(t, LANES - half, axis=1),
                     jnp.where(block == 2, pltpu.roll(t, half, axis=1), t))


def _projection_steps(st, xn_ref, w_ref, slab_ref, out_refs, rope=False, scale=None):
    def project(g):
        def run():
            y = _dot(xn_ref[...], w_ref[:, g * GROUP_WIDTH:(g + 1) * GROUP_WIDTH])
            for p in range(PAIRS_PER_GROUP):
                yt = y[:, p * LANES:(p + 1) * LANES]
                if rope:
                    yt = _rope_tile(yt, st["tables"])
                if scale is not None:
                    yt = yt * scale
                slab_ref[g * PAIRS_PER_GROUP + p] = yt
        return run

    def scatter(g, d):
        def run():
            n = slab_ref.shape[1] // d
            for r in range(d):
                for p in range(PAIRS_PER_GROUP):
                    rows = slab_ref[g * PAIRS_PER_GROUP + p, pl.ds(r, n, stride=d), :]
                    out_refs[g][0, r, :, p * LANES:(p + 1) * LANES] = rows.astype(_BF16)
        return run

    steps = []
    for g, d in enumerate(ATTN_DILATIONS):
        steps += [project(g), scatter(g, d)]
    return steps


def _kv_steps(read, pos_ref, gain_ref, invf_ref, wk_ref, wv_ref, k_refs, v_refs, cos_ref, sin_ref, slab_ref, xn_ref):
    st = {}

    def start():
        xn_ref[...] = _rms_norm(read(), gain_ref[...]).astype(_BF16)
        ang = pos_ref[...] * invf_ref[...]
        cos_ref[...] = jnp.cos(ang)
        sin_ref[...] = jnp.sin(ang)
        st["tables"] = _rope_tables(cos_ref[...], sin_ref[...])

    return ([start] + _projection_steps(st, xn_ref, wk_ref, slab_ref, k_refs, rope=True)
            + _projection_steps(st, xn_ref, wv_ref, slab_ref, v_refs))


def _q_steps(read, cos_ref, sin_ref, gain_ref, wq_ref, q_refs, slab_ref, xn_ref):
    st = {}

    def start():
        xn_ref[...] = _rms_norm(read(), gain_ref[...]).astype(_BF16)
        st["tables"] = _rope_tables(cos_ref[...], sin_ref[...])

    return [start] + _projection_steps(st, xn_ref, wq_ref, slab_ref, q_refs, rope=True, scale=HEAD_DIM ** -0.5)


def _merge_steps(read, write, gpost_ref, wo_ref, o_refs, l_refs, nat_ref):
    st = {"o": []}
    permuted = ((o_refs[1], ATTN_DILATIONS[1]), (l_refs[1], ATTN_DILATIONS[1]),
                (o_refs[2], ATTN_DILATIONS[2]), (l_refs[2], ATTN_DILATIONS[2]))

    def lane_tile(p):
        def run():
            cols = slice(p * LANES, (p + 1) * LANES)
            for a, (ref, d) in enumerate(permuted):
                n = nat_ref.shape[2] // d
                for r in range(d):
                    nat_ref[a, p, pl.ds(r, n, stride=d), :] = ref[0, r, :, cols]
            o0, l0 = o_refs[0][0, 0, :, cols], l_refs[0][0, 0, :, cols]
            o1, l1, o2, l2 = (nat_ref[a, p] for a in range(4))
            top = jnp.maximum(jnp.maximum(l0, l1), l2)
            e0, e1, e2 = jnp.exp(l0 - top), jnp.exp(l1 - top), jnp.exp(l2 - top)
            st["o"].append(((e0 * o0 + e1 * o1 + e2 * o2) / (e0 + e1 + e2)).astype(_BF16))
        return run

    def end():
        mix = _dot(jnp.concatenate(st["o"], axis=-1), wo_ref[...])
        write(read() + _rms_norm(mix, gpost_ref[...]))

    return [lane_tile(p) for p in range(PAIRS_PER_GROUP)] + [end]


def _run(steps):
    for step in steps:
        step()


def _two_stage(n_tiles, first, second, init=None):
    i = pl.program_id(0)
    if init is not None:
        pl.when(i == 0)(init)

    @pl.when(i > 0)
    def _():
        _run(second())

    @pl.when(i < n_tiles)
    def _():
        _run(first())


def _cur_tile(n_tiles):
    return lambda i: jnp.minimum(i, n_tiles - 1)


def _prev_tile():
    return lambda i: jnp.maximum(i - 1, 0)


def _row_spec(tile_of):
    return pl.BlockSpec((ROW_TILE, D_MODEL), lambda i: (tile_of(i), 0))


def _residue_major_specs(batch, seq, dtype, tile_of):
    tiles_per_seq = seq // ROW_TILE
    specs, shapes = [], []
    for d in ATTN_DILATIONS:
        specs.append(pl.BlockSpec((1, d, ROW_TILE // d, GROUP_WIDTH),
                                  lambda i: (tile_of(i) // tiles_per_seq, 0, tile_of(i) % tiles_per_seq, 0)))
        shapes.append(jax.ShapeDtypeStruct((batch, d, seq // d, GROUP_WIDTH), dtype))
    return specs, shapes


def _ffn_weight_specs():
    return [_const_spec((1, D_MODEL)), _const_spec((1, D_MODEL)),
            _const_spec((D_MODEL, D_FF)), _const_spec((D_MODEL, D_FF)), _const_spec((D_FF, D_MODEL))]


_HANDOVER = pltpu.VMEM((ROW_TILE, D_MODEL), _F32)
_SLAB = pltpu.VMEM((QKV_TILES, ROW_TILE, LANES), _F32)
_FFN_WORK = [pltpu.VMEM((ROW_TILE, D_MODEL), _BF16), pltpu.VMEM((ROW_TILE, D_MODEL), _F32)]
BF16_SUBLANES = 16


def _hand_over(hand_ref):
    def put(y):
        hand_ref[...] = y

    def get():
        return hand_ref[...]

    return put, get


def _split(refs, *counts):
    out, at = [], 0
    for n in counts:
        out.append(refs[at:at + n])
        at += n
    assert at == len(refs)
    return out


def _cast_specs(weights, n_tiles):
    in_specs, out_specs, out_shapes = [], [], []
    for w, lead, _ in weights:
        rows, cols = w.shape[len(lead):]
        n_blocks = max(n for n in range(1, n_tiles + 1) if rows % (n * BF16_SUBLANES) == 0)

        def block(i, n_blocks=n_blocks):
            return (jnp.minimum(i, n_blocks - 1), 0)

        in_specs.append(pl.BlockSpec((None,) * len(lead) + (rows // n_blocks, cols),
                                     lambda i, lead=lead, block=block: lead + block(i)))
        out_specs.append(pl.BlockSpec((rows // n_blocks, cols), block))
        out_shapes.append(jax.ShapeDtypeStruct((rows, cols), _BF16))
    return in_specs, out_specs, out_shapes


def _cast_blocks(src_refs, dst_refs, interleave):
    for src, dst, flag in zip(src_refs, dst_refs, interleave):
        if flag:
            for j in range(src.shape[1] // LANES):
                cols = slice(j * LANES, (j + 1) * LANES)
                dst[:, cols] = _pair_interleaved(src[:, cols]).astype(_BF16)
        else:
            dst[...] = src[...].astype(_BF16)


def _ffn_pool_kernel(n_tiles, tiles_per_seq, interleave, h_ref, *refs):
    n_cast = len(interleave)
    ffn_w, pool_w, cast_src, (o_ref,), cast_dst, (hand_ref, ubuf_ref, weff_ref, *work) = _split(
        refs, 5, 6, n_cast, 1, n_cast, 5)
    gpre_ref, gpost_ref, win_ref, wgrp_ref, scale_ref, wout_ref = pool_w
    put, get = _hand_over(hand_ref)
    j = (pl.program_id(0) - 1) % tiles_per_seq

    def write(y):
        o_ref[...] = y

    def init():
        ubuf_ref[0:POOL_HALO, :] = jnp.zeros((POOL_HALO, D_MODEL), _F32)
        _pool_out_weight(wgrp_ref, scale_ref, wout_ref, weff_ref)

    _cast_blocks(cast_src, cast_dst, interleave)
    _two_stage(n_tiles,
               lambda: _ffn_steps(lambda: h_ref[...], put, *ffn_w, *work),
               lambda: _pool_steps(get, write, j, gpre_ref, gpost_ref, win_ref, weff_ref, ubuf_ref),
               init=init)


def _ffn_pool(h, ffn_w, pool_w, seq, to_cast):
    m = h.shape[0]
    n_tiles = m // ROW_TILE
    n_grp = len(POOL_WINDOWS)
    cast_in, cast_out, cast_shapes = _cast_specs(to_cast, n_tiles)
    outs = pl.pallas_call(
        functools.partial(_ffn_pool_kernel, n_tiles, seq // ROW_TILE, tuple(f for _, _, f in to_cast)),
        grid=(n_tiles + 1,),
        in_specs=[_row_spec(_cur_tile(n_tiles))] + _ffn_weight_specs() + [
            _const_spec((1, D_MODEL)), _const_spec((1, D_MODEL)), _const_spec((D_MODEL, D_MODEL)),
            _const_spec((n_grp, POOL_GROUP_DIM, POOL_GROUP_DIM)), _const_spec((1, D_MODEL)),
            _const_spec((D_MODEL, D_MODEL))] + cast_in,
        out_specs=[_row_spec(_prev_tile())] + cast_out,
        out_shape=[jax.ShapeDtypeStruct((m, D_MODEL), _F32)] + cast_shapes,
        scratch_shapes=[_HANDOVER, pltpu.VMEM((POOL_HALO + ROW_TILE, D_MODEL), _F32),
                        pltpu.VMEM((D_MODEL, D_MODEL), _BF16)] + _FFN_WORK,
        compiler_params=_params(("arbitrary",)),
        name="ffn_pool",
    )(h, *ffn_w, *pool_w, *(w for w, _, _ in to_cast))
    return outs[0], outs[1:]


def _ffn_and_keep(h_ref, o_ref, put, ffn_w, work):
    def write(y):
        o_ref[...] = y
        put(y)

    return lambda: _ffn_steps(lambda: h_ref[...], write, *ffn_w, *work)


def _table_spec():
    return pl.BlockSpec((ROW_TILE, LANES), lambda i: (jnp.maximum(i - 1, 0), 0))


def _ffn_kv_kernel(n_tiles, interleave, h_ref, pos_ref, *refs):
    n_cast = len(interleave)
    (ffn_w, kv_w, cast_src, (o_ref,), k_refs, v_refs, (cos_ref, sin_ref), cast_dst,
     (hand_ref, slab_ref, xn2_ref, *work)) = _split(refs, 5, 4, n_cast, 1, N_GROUPS, N_GROUPS, 2, n_cast, 5)
    put, get = _hand_over(hand_ref)
    _cast_blocks(cast_src, cast_dst, interleave)
    _two_stage(n_tiles, _ffn_and_keep(h_ref, o_ref, put, ffn_w, work),
               lambda: _kv_steps(get, pos_ref, *kv_w, k_refs, v_refs, cos_ref, sin_ref, slab_ref, xn2_ref))


def _ffn_kv(h, pos, ffn_w, kv_w, batch, seq, to_cast):
    m = h.shape[0]
    n_tiles = m // ROW_TILE
    specs, shapes = _residue_major_specs(batch, seq, _BF16, _prev_tile())
    cast_in, cast_out, cast_shapes = _cast_specs(to_cast, n_tiles)
    table_shape = jax.ShapeDtypeStruct((m, LANES), _F32)
    outs = pl.pallas_call(
        functools.partial(_ffn_kv_kernel, n_tiles, tuple(f for _, _, f in to_cast)),
        grid=(n_tiles + 1,),
        in_specs=[_row_spec(_cur_tile(n_tiles)), pl.BlockSpec((ROW_TILE, 1), lambda i: (jnp.maximum(i - 1, 0), 0))]
        + _ffn_weight_specs() + [_const_spec((1, D_MODEL)), _const_spec((1, LANES)),
                                 _const_spec((D_MODEL, QKV_WIDTH)), _const_spec((D_MODEL, QKV_WIDTH))] + cast_in,
        out_specs=[_row_spec(_cur_tile(n_tiles))] + specs * 2 + [_table_spec()] * 2 + cast_out,
        out_shape=[jax.ShapeDtypeStruct((m, D_MODEL), _F32)] + shapes * 2 + [table_shape] * 2 + cast_shapes,
        scratch_shapes=[_HANDOVER, _SLAB, _FFN_WORK[0]] + _FFN_WORK,
        compiler_params=_params(("arbitrary",)),
        name="ffn_kv",
    )(h, pos, *ffn_w, *kv_w, *(w for w, _, _ in to_cast))
    return outs[0], outs[1:4], outs[4:7], outs[7:9], outs[9:]


def _ffn_q_kernel(n_tiles, interleave, h_ref, cos_ref, sin_ref, *refs):
    n_cast = len(interleave)
    ffn_w, q_w, cast_src, (o_ref,), q_refs, cast_dst, (hand_ref, slab_ref, xn2_ref, *work) = _split(
        refs, 5, 2, n_cast, 1, N_GROUPS, n_cast, 5)
    put, get = _hand_over(hand_ref)
    _cast_blocks(cast_src, cast_dst, interleave)
    _two_stage(n_tiles, _ffn_and_keep(h_ref, o_ref, put, ffn_w, work),
               lambda: _q_steps(get, cos_ref, sin_ref, *q_w, q_refs, slab_ref, xn2_ref))


def _ffn_q(h, tables, ffn_w, q_w, batch, seq, to_cast):
    m = h.shape[0]
    n_tiles = m // ROW_TILE
    specs, shapes = _residue_major_specs(batch, seq, _BF16, _prev_tile())
    cast_in, cast_out, cast_shapes = _cast_specs(to_cast, n_tiles)
    outs = pl.pallas_call(
        functools.partial(_ffn_q_kernel, n_tiles, tuple(f for _, _, f in to_cast)),
        grid=(n_tiles + 1,),
        in_specs=[_row_spec(_cur_tile(n_tiles))] + [_table_spec()] * 2 + _ffn_weight_specs()
        + [_const_spec((1, D_MODEL)), _const_spec((D_MODEL, QKV_WIDTH))] + cast_in,
        out_specs=[_row_spec(_cur_tile(n_tiles))] + specs + cast_out,
        out_shape=[jax.ShapeDtypeStruct((m, D_MODEL), _F32)] + shapes + cast_shapes,
        scratch_shapes=[_HANDOVER, _SLAB, _FFN_WORK[0]] + _FFN_WORK,
        compiler_params=_params(("arbitrary",)),
        name="ffn_q",
    )(h, *tables, *ffn_w, *q_w, *(w for w, _, _ in to_cast))
    return outs[0], outs[1:4], outs[4:]


def _merge_ffn_kernel(n_tiles, h_ref, *refs):
    merge_w, o_refs, l_refs, ffn_w, (out_ref, hand_ref, nat_ref, *work) = (
        refs[:2], refs[2:5], refs[5:8], refs[8:13], refs[13:])
    put, get = _hand_over(hand_ref)

    def write(y):
        out_ref[...] = y

    _two_stage(n_tiles,
               lambda: _merge_steps(lambda: h_ref[...], put, *merge_w, o_refs, l_refs, nat_ref),
               lambda: _ffn_steps(get, write, *ffn_w, *work))


def _merge_ffn(h, merge_w, outs, lses, ffn_w, batch, seq):
    m = h.shape[0]
    n_tiles = m // ROW_TILE
    specs, _ = _residue_major_specs(batch, seq, _F32, _cur_tile(n_tiles))
    return pl.pallas_call(
        functools.partial(_merge_ffn_kernel, n_tiles),
        grid=(n_tiles + 1,),
        in_specs=[_row_spec(_cur_tile(n_tiles)), _const_spec((1, D_MODEL)), _const_spec((GROUP_WIDTH, D_MODEL))]
        + specs * 2 + _ffn_weight_specs(),
        out_specs=_row_spec(_prev_tile()),
        out_shape=jax.ShapeDtypeStruct((m, D_MODEL), _F32),
        scratch_shapes=[_HANDOVER, pltpu.VMEM((4, PAIRS_PER_GROUP, ROW_TILE, LANES), _F32)] + _FFN_WORK,
        compiler_params=_params(("arbitrary",)),
        name="merge_ffn",
    )(h, *merge_w, *outs, *lses, *ffn_w)


def _attn_kernel(q_ref, kp_ref, kc_ref, vp_ref, vc_ref, o_ref, lse_ref):
    step = pl.program_id(1)
    n_blk = q_ref.shape[1] // ATTN_BLK
    win = 2 * ATTN_BLK
    qi = lax.broadcasted_iota(jnp.int32, (win, win), 0) % ATTN_BLK
    kj = lax.broadcasted_iota(jnp.int32, (win, win), 1)
    band = jnp.where((kj >= qi) & (kj <= qi + ATTN_BLK), 0.0, NEG_BIG)
    first_lo = jnp.maximum(qi, jnp.where(step == 0, ATTN_BLK, 0))
    band_first = jnp.where((kj >= first_lo) & (kj <= qi + ATTN_BLK), 0.0, NEG_BIG)
    q_lane = lax.broadcasted_iota(jnp.int32, (ATTN_BLK, LANES), 1)
    q_is_a = (q_lane % HEAD_DIM) < HEAD_DIM // 2
    v_lane = lax.broadcasted_iota(jnp.int32, (win, LANES), 1)
    v_is_a = v_lane < HEAD_DIM
    ones_a = jnp.where(v_is_a, 1.0, 0.0).astype(_BF16)
    ones_b = jnp.where(v_is_a, 0.0, 1.0).astype(_BF16)
    o_is_a = q_lane < HEAD_DIM
    nt = (((1,), (1,)), ((), ()))
    work = [(j, p) for j in range(n_blk) for p in range(PAIRS_PER_GROUP)]

    def window(prev_ref, cur_ref, j, p):
        pair = slice(p * LANES, (p + 1) * LANES)
        if j == 0:
            return jnp.concatenate([prev_ref[0, :, pair], cur_ref[0, 0:ATTN_BLK, pair]], axis=0)
        return cur_ref[0, (j - 1) * ATTN_BLK:(j + 1) * ATTN_BLK, pair]

    def scores(j, p):
        q = q_ref[0, j * ATTN_BLK:(j + 1) * ATTN_BLK, p * LANES:(p + 1) * LANES]
        zero = jnp.zeros_like(q)
        q2 = jnp.concatenate([jnp.where(q_is_a, q, zero), jnp.where(q_is_a, zero, q)], axis=0)
        s = lax.dot_general(q2, window(kp_ref, kc_ref, j, p), nt, preferred_element_type=_F32)
        return s + (band_first if j == 0 else band)

    def finish(j, p, s):
        m = jnp.max(s, axis=-1, keepdims=True)
        prob = jnp.exp(s - m).astype(_BF16)
        prob = jnp.concatenate([prob[:ATTN_BLK], prob[ATTN_BLK:]], axis=1)
        v = window(vp_ref, vc_ref, j, p)
        zero = jnp.zeros_like(v)
        v2 = jnp.concatenate([jnp.concatenate([jnp.where(v_is_a, v, zero), ones_a], axis=1),
                              jnp.concatenate([jnp.where(v_is_a, zero, v), ones_b], axis=1)], axis=0)
        r = _dot(prob, v2)
        den = r[:, LANES:]
        rows = slice(j * ATTN_BLK, (j + 1) * ATTN_BLK)
        cols = slice(p * LANES, (p + 1) * LANES)
        o_ref[0, rows, cols] = r[:, :LANES] * (1.0 / den)
        lse_ref[0, rows, cols] = jnp.where(o_is_a, m[:ATTN_BLK], m[ATTN_BLK:]) + jnp.log(den)

    ahead = 1
    pending = [scores(*work[n]) for n in range(ahead)]
    for n, (j, p) in enumerate(work):
        s = pending.pop(0)
        if n + ahead < len(work):
            pending.append(scores(*work[n + ahead]))
        finish(j, p, s)


def _attn_group(q, k, v):
    batch, d, rows, _ = q.shape
    n_seq = batch * d
    q, k, v = (a.reshape(n_seq, rows, GROUP_WIDTH) for a in (q, k, v))
    q_tile = min(ATTN_Q_TILE, rows)
    blocks_per_step = q_tile // ATTN_BLK

    def cur(s, i):
        return (s, i, 0)

    def prev(s, i):
        return (s, jnp.maximum(i * blocks_per_step - 1, 0), 0)

    cur_spec = pl.BlockSpec((1, q_tile, GROUP_WIDTH), cur)
    prev_spec = pl.BlockSpec((1, ATTN_BLK, GROUP_WIDTH), prev)
    out_shape = jax.ShapeDtypeStruct((n_seq, rows, GROUP_WIDTH), _F32)
    o, lse = pl.pallas_call(
        _attn_kernel,
        grid=(n_seq, rows // q_tile),
        in_specs=[cur_spec, prev_spec, cur_spec, prev_spec, cur_spec],
        out_specs=[cur_spec, cur_spec],
        out_shape=[out_shape, out_shape],
        compiler_params=_params(("parallel", "arbitrary")),
        name=f"window_attn_d{d}",
    )(q, k, k, v, v)
    shape = (batch, d, rows, GROUP_WIDTH)
    return o.reshape(shape), lse.reshape(shape)


def kernel(x, positions, norm_gain, ffn_w_gate, ffn_w_up, ffn_w_down, pool_w_in, pool_w_group, pool_scale,
           pool_w_out, kv_norm_gain, w_k, w_v, attn_w_q, attn_w_o):
    batch, seq, _ = x.shape
    m = batch * seq
    gains = norm_gain.reshape(2, 6, 1, D_MODEL)
    bf = lambda w: w.astype(_BF16)

    def ffn_f32(layer, slot):
        return tuple((w, (layer, slot), False) for w in (ffn_w_gate, ffn_w_up, ffn_w_down))

    def ffn_gains(layer, slot):
        return (gains[layer, 4 * slot], gains[layer, 4 * slot + 1])

    pos = positions.astype(_F32).reshape(m, 1)
    inv_freq = ROPE_THETA ** (-jnp.arange(0, HEAD_DIM, 2, dtype=_F32) / HEAD_DIM)
    invf = jnp.tile(inv_freq, LANES // (HEAD_DIM // 2)).reshape(1, LANES)

    h = x.reshape(m, D_MODEL)
    pool_w = (gains[0, 2], gains[0, 3], bf(pool_w_in[0]), pool_w_group[0],
              pool_scale[0].reshape(1, D_MODEL), pool_w_out[0])
    ffn00 = tuple(bf(w[lead]) for w, lead, _ in ffn_f32(0, 0))
    h, (*ffn01, wk_bf, wv_bf) = _ffn_pool(h, ffn_gains(0, 0) + ffn00, pool_w, seq,
                                          to_cast=ffn_f32(0, 1) + ((w_k, (), True), (w_v, (), False)))
    kv_w = (kv_norm_gain.reshape(1, D_MODEL), invf, wk_bf, wv_bf)
    h, k_sh, v_sh, tables, (*ffn10, wq_bf) = _ffn_kv(h, pos, ffn_gains(0, 1) + tuple(ffn01), kv_w, batch, seq,
                                                     to_cast=ffn_f32(1, 0) + ((attn_w_q, (0,), True),))
    h, q, (*ffn11, wo_bf) = _ffn_q(h, tables, ffn_gains(1, 0) + tuple(ffn10), (gains[1, 2], wq_bf), batch, seq,
                                   to_cast=ffn_f32(1, 1) + ((attn_w_o, (0,), False),))
    outs, lses = [], []
    for g in range(N_GROUPS):
        o, lse = _attn_group(q[g], k_sh[g], v_sh[g])
        outs.append(o)
        lses.append(lse)
    h = _merge_ffn(h, (gains[1, 3], wo_bf), outs, lses, ffn_gains(1, 1) + tuple(ffn11), batch, seq)
    return h.reshape(batch, seq, D_MODEL)
```

```python
import functools

import jax
import jax.numpy as jnp
from jax import lax
from jax.experimental import pallas as pl
from jax.experimental.pallas import tpu as pltpu

D_MODEL = 1024
D_FF = 2816
POOL_WINDOWS = (2, 4, 8, 16)
POOL_GROUP_DIM = D_MODEL // len(POOL_WINDOWS)
POOL_HALO = 16
ATTN_DILATIONS = (1, 4, 16)
N_GROUPS = len(ATTN_DILATIONS)
ATTN_BLK = 128
HEAD_DIM = 64
HEADS_PER_GROUP = 8
GROUP_WIDTH = HEADS_PER_GROUP * HEAD_DIM
QKV_WIDTH = N_GROUPS * GROUP_WIDTH
ROPE_THETA = 10000.0
RMS_EPS = 1e-6
NEG_BIG = -1e30

LANES = 128
PAIRS_PER_GROUP = GROUP_WIDTH // LANES
QKV_TILES = QKV_WIDTH // LANES
ROW_TILE = 512
ATTN_Q_TILE = 1024
FF_CHUNK = 256
VMEM_LIMIT_BYTES = 56 * 1024 * 1024

_F32 = jnp.float32
_BF16 = jnp.bfloat16


def _rms_norm(x, gain):
    ms = jnp.mean(x * x, axis=-1, keepdims=True)
    return x * lax.rsqrt(ms + RMS_EPS) * gain


def _dot(a, b):
    return jnp.dot(a, b, preferred_element_type=_F32)


def _const_spec(shape):
    return pl.BlockSpec(shape, lambda *_: (0,) * len(shape), pipeline_mode=pl.Buffered(1))


def _params(semantics):
    return pltpu.CompilerParams(dimension_semantics=semantics, vmem_limit_bytes=VMEM_LIMIT_BYTES)


def _ffn_steps(read, write, gpre_ref, gpost_ref, wg_ref, wu_ref, wd_ref, xn_ref, acc_ref):
    def start():
        xn_ref[...] = _rms_norm(read(), gpre_ref[...]).astype(_BF16)

    def chunk(c):
        def run():
            hi = min(c + FF_CHUNK, D_FF)
            g = _dot(xn_ref[...], wg_ref[:, c:hi])
            u = _dot(xn_ref[...], wu_ref[:, c:hi])
            a = (g * jax.nn.sigmoid(g) * u).astype(_BF16)
            down = _dot(a, wd_ref[c:hi, :])
            if c == 0:
                acc_ref[...] = down
            else:
                acc_ref[...] += down
        return run

    def end():
        write(read() + _rms_norm(acc_ref[...], gpost_ref[...]))

    return [start] + [chunk(c) for c in range(0, D_FF, FF_CHUNK)] + [end]


def _pool_out_weight(wgrp_ref, scale_ref, wout_ref, weff_ref):
    for g in range(len(POOL_WINDOWS)):
        rows = slice(g * POOL_GROUP_DIM, (g + 1) * POOL_GROUP_DIM)
        weff_ref[rows, :] = jnp.dot(wgrp_ref[g] * scale_ref[:, rows], wout_ref[rows, :], preferred_element_type=_F32,
                                    precision=lax.Precision.HIGHEST).astype(_BF16)


def _pool_steps(read, write, j, gpre_ref, gpost_ref, win_ref, weff_ref, ubuf_ref):
    st = {"p": []}

    def start():
        st["x"] = read()
        hm = _rms_norm(st["x"], gpre_ref[...]).astype(_BF16)
        st["u"] = _dot(hm, win_ref[...])
        ubuf_ref[0:POOL_HALO, :] = jnp.where(j == 0, 0.0, ubuf_ref[0:POOL_HALO, :])
        ubuf_ref[POOL_HALO:, :] = st["u"]

    def group(g, w):
        def run():
            tm = st["x"].shape[0]
            cols = slice(g * POOL_GROUP_DIM, (g + 1) * POOL_GROUP_DIM)
            t = j * tm + lax.broadcasted_iota(jnp.int32, (tm, 1), 0)
            s = ubuf_ref[:, cols]
            k = 1
            while k < w:
                s = s + pltpu.roll(s, k, axis=0)
                k *= 2
            count = jnp.minimum(t + 1, w).astype(_F32)
            p = s[POOL_HALO:] / count - st["u"][:, cols]
            st["p"].append(p.astype(_BF16))
        return run

    def end():
        tm = st["x"].shape[0]
        mix = _dot(jnp.concatenate(st["p"], axis=-1), weff_ref[...])
        ubuf_ref[0:POOL_HALO, :] = st["u"][tm - POOL_HALO:, :]
        write(st["x"] + _rms_norm(mix, gpost_ref[...]))

    return [start] + [group(g, w) for g, w in enumerate(POOL_WINDOWS)] + [end]


def _rope_tables(c, s):
    lane = lax.broadcasted_iota(jnp.int32, c.shape, 1)
    return c, jnp.where(lane < LANES // 2, -s, s)


def _rope_tile(xt, tables):
    c, s_signed = tables
    return xt * c + pltpu.roll(xt, LANES // 2, axis=1) * s_signed


def _pair_interleaved(t):
    half = HEAD_DIM // 2
    block = lax.broadcasted_iota(jnp.int32, t.shape, 1) // half
    return jnp.where(block == 1, pltpu.roll(t, LANES - half, axis=1),
                     jnp.where(block == 2, pltpu.roll(t, half, axis=1), t))


def _projection_steps(st, xn_ref, w_ref, slab_ref, out_refs, rope=False, scale=None):
    def project(g):
        def run():
            y = _dot(xn_ref[...], w_ref[:, g * GROUP_WIDTH:(g + 1) * GROUP_WIDTH])
            for p in range(PAIRS_PER_GROUP):
                yt = y[:, p * LANES:(p + 1) * LANES]
                if rope:
                    yt = _rope_tile(yt, st["tables"])
                if scale is not None:
                    yt = yt * scale
                slab_ref[g * PAIRS_PER_GROUP + p] = yt
        return run

    def scatter(g, d):
        def run():
            n = slab_ref.shape[1] // d
            for r in range(d):
                for p in range(PAIRS_PER_GROUP):
                    rows = slab_ref[g * PAIRS_PER_GROUP + p, pl.ds(r, n, stride=d), :]
                    out_refs[g][0, r, :, p * LANES:(p + 1) * LANES] = rows.astype(_BF16)
        return run

    steps = []
    for g, d in enumerate(ATTN_DILATIONS):
        steps += [project(g), scatter(g, d)]
    return steps


def _kv_steps(read, pos_ref, gain_ref, invf_ref, wk_ref, wv_ref, k_refs, v_refs, cos_ref, sin_ref, slab_ref, xn_ref):
    st = {}

    def start():
        xn_ref[...] = _rms_norm(read(), gain_ref[...]).astype(_BF16)
        ang = pos_ref[...] * invf_ref[...]
        cos_ref[...] = jnp.cos(ang)
        sin_ref[...] = jnp.sin(ang)
        st["tables"] = _rope_tables(cos_ref[...], sin_ref[...])

    return ([start] + _projection_steps(st, xn_ref, wk_ref, slab_ref, k_refs, rope=True)
            + _projection_steps(st, xn_ref, wv_ref, slab_ref, v_refs))


def _q_steps(read, cos_ref, sin_ref, gain_ref, wq_ref, q_refs, slab_ref, xn_ref):
    st = {}

    def start():
        xn_ref[...] = _rms_norm(read(), gain_ref[...]).astype(_BF16)
        st["tables"] = _rope_tables(cos_ref[...], sin_ref[...])

    return [start] + _projection_steps(st, xn_ref, wq_ref, slab_ref, q_refs, rope=True, scale=HEAD_DIM ** -0.5)


def _merge_steps(read, write, gpost_ref, wo_ref, o_refs, l_refs, nat_ref):
    st = {"o": []}
    permuted = ((o_refs[1], ATTN_DILATIONS[1]), (l_refs[1], ATTN_DILATIONS[1]),
                (o_refs[2], ATTN_DILATIONS[2]), (l_refs[2], ATTN_DILATIONS[2]))

    def lane_tile(p):
        def run():
            cols = slice(p * LANES, (p + 1) * LANES)
            for a, (ref, d) in enumerate(permuted):
                n = nat_ref.shape[2] // d
                for r in range(d):
                    nat_ref[a, p, pl.ds(r, n, stride=d), :] = ref[0, r, :, cols]
            o0, l0 = o_refs[0][0, 0, :, cols], l_refs[0][0, 0, :, cols]
            o1, l1, o2, l2 = (nat_ref[a, p] for a in range(4))
            top = jnp.maximum(jnp.maximum(l0, l1), l2)
            e0, e1, e2 = jnp.exp(l0 - top), jnp.exp(l1 - top), jnp.exp(l2 - top)
            st["o"].append(((e0 * o0 + e1 * o1 + e2 * o2) / (e0 + e1 + e2)).astype(_BF16))
        return run

    def end():
        mix = _dot(jnp.concatenate(st["o"], axis=-1), wo_ref[...])
        write(read() + _rms_norm(mix, gpost_ref[...]))

    return [lane_tile(p) for p in range(PAIRS_PER_GROUP)] + [end]


def _run(steps):
    for step in steps:
        step()


def _two_stage(n_tiles, first, second, init=None):
    i = pl.program_id(0)
    if init is not None:
        pl.when(i == 0)(init)

    @pl.when(i > 0)
    def _():
        _run(second())

    @pl.when(i < n_tiles)
    def _():
        _run(first())


def _cur_tile(n_tiles):
    return lambda i: jnp.minimum(i, n_tiles - 1)


def _prev_tile():
    return lambda i: jnp.maximum(i - 1, 0)


def _row_spec(tile_of):
    return pl.BlockSpec((ROW_TILE, D_MODEL), lambda i: (tile_of(i), 0))


def _residue_major_specs(batch, seq, dtype, tile_of):
    tiles_per_seq = seq // ROW_TILE
    specs, shapes = [], []
    for d in ATTN_DILATIONS:
        specs.append(pl.BlockSpec((1, d, ROW_TILE // d, GROUP_WIDTH),
                                  lambda i: (tile_of(i) // tiles_per_seq, 0, tile_of(i) % tiles_per_seq, 0)))
        shapes.append(jax.ShapeDtypeStruct((batch, d, seq // d, GROUP_WIDTH), dtype))
    return specs, shapes


def _ffn_weight_specs():
    return [_const_spec((1, D_MODEL)), _const_spec((1, D_MODEL)),
            _const_spec((D_MODEL, D_FF)), _const_spec((D_MODEL, D_FF)), _const_spec((D_FF, D_MODEL))]


_HANDOVER = pltpu.VMEM((ROW_TILE, D_MODEL), _F32)
_SLAB = pltpu.VMEM((QKV_TILES, ROW_TILE, LANES), _F32)
_FFN_WORK = [pltpu.VMEM((ROW_TILE, D_MODEL), _BF16), pltpu.VMEM((ROW_TILE, D_MODEL), _F32)]
BF16_SUBLANES = 16


def _hand_over(hand_ref):
    def put(y):
        hand_ref[...] = y

    def get():
        return hand_ref[...]

    return put, get


def _split(refs, *counts):
    out, at = [], 0
    for n in counts:
        out.append(refs[at:at + n])
        at += n
    assert at == len(refs)
    return out


def _cast_specs(weights, n_tiles):
    in_specs, out_specs, out_shapes = [], [], []
    for w, lead, _ in weights:
        rows, cols = w.shape[len(lead):]
        n_blocks = max(n for n in range(1, n_tiles + 1) if rows % (n * BF16_SUBLANES) == 0)

        def block(i, n_blocks=n_blocks):
            return (jnp.minimum(i, n_blocks - 1), 0)

        in_specs.append(pl.BlockSpec((None,) * len(lead) + (rows // n_blocks, cols),
                                     lambda i, lead=lead, block=block: lead + block(i)))
        out_specs.append(pl.BlockSpec((rows // n_blocks, cols), block))
        out_shapes.append(jax.ShapeDtypeStruct((rows, cols), _BF16))
    return in_specs, out_specs, out_shapes


def _cast_blocks(src_refs, dst_refs, interleave):
    for src, dst, flag in zip(src_refs, dst_refs, interleave):
        if flag:
            for j in range(src.shape[1] // LANES):
                cols = slice(j * LANES, (j + 1) * LANES)
                dst[:, cols] = _pair_interleaved(src[:, cols]).astype(_BF16)
        else:
            dst[...] = src[...].astype(_BF16)


def _ffn_pool_kernel(n_tiles, tiles_per_seq, interleave, h_ref, *refs):
    n_cast = len(interleave)
    ffn_w, pool_w, cast_src, (o_ref,), cast_dst, (hand_ref, ubuf_ref, weff_ref, *work) = _split(
        refs, 5, 6, n_cast, 1, n_cast, 5)
    gpre_ref, gpost_ref, win_ref, wgrp_ref, scale_ref, wout_ref = pool_w
    put, get = _hand_over(hand_ref)
    j = (pl.program_id(0) - 1) % tiles_per_seq

    def write(y):
        o_ref[...] = y

    def init():
        ubuf_ref[0:POOL_HALO, :] = jnp.zeros((POOL_HALO, D_MODEL), _F32)
        _pool_out_weight(wgrp_ref, scale_ref, wout_ref, weff_ref)

    _cast_blocks(cast_src, cast_dst, interleave)
    _two_stage(n_tiles,
               lambda: _ffn_steps(lambda: h_ref[...], put, *ffn_w, *work),
               lambda: _pool_steps(get, write, j, gpre_ref, gpost_ref, win_ref, weff_ref, ubuf_ref),
               init=init)


def _ffn_pool(h, ffn_w, pool_w, seq, to_cast):
    m = h.shape[0]
    n_tiles = m // ROW_TILE
    n_grp = len(POOL_WINDOWS)
    cast_in, cast_out, cast_shapes = _cast_specs(to_cast, n_tiles)
    outs = pl.pallas_call(
        functools.partial(_ffn_pool_kernel, n_tiles, seq // ROW_TILE, tuple(f for _, _, f in to_cast)),
        grid=(n_tiles + 1,),
        in_specs=[_row_spec(_cur_tile(n_tiles))] + _ffn_weight_specs() + [
            _const_spec((1, D_MODEL)), _const_spec((1, D_MODEL)), _const_spec((D_MODEL, D_MODEL)),
            _const_spec((n_grp, POOL_GROUP_DIM, POOL_GROUP_DIM)), _const_spec((1, D_MODEL)),
            _const_spec((D_MODEL, D_MODEL))] + cast_in,
        out_specs=[_row_spec(_prev_tile())] + cast_out,
        out_shape=[jax.ShapeDtypeStruct((m, D_MODEL), _F32)] + cast_shapes,
        scratch_shapes=[_HANDOVER, pltpu.VMEM((POOL_HALO + ROW_TILE, D_MODEL), _F32),
                        pltpu.VMEM((D_MODEL, D_MODEL), _BF16)] + _FFN_WORK,
        compiler_params=_params(("arbitrary",)),
        name="ffn_pool",
    )(h, *ffn_w, *pool_w, *(w for w, _, _ in to_cast))
    return outs[0], outs[1:]


def _ffn_and_keep(h_ref, o_ref, put, ffn_w, work):
    def write(y):
        o_ref[...] = y
        put(y)

    return lambda: _ffn_steps(lambda: h_ref[...], write, *ffn_w, *work)


def _table_spec():
    return pl.BlockSpec((ROW_TILE, LANES), lambda i: (jnp.maximum(i - 1, 0), 0))


def _ffn_kv_kernel(n_tiles, interleave, h_ref, pos_ref, *refs):
    n_cast = len(interleave)
    (ffn_w, kv_w, cast_src, (o_ref,), k_refs, v_refs, (cos_ref, sin_ref), cast_dst,
     (hand_ref, slab_ref, xn2_ref, *work)) = _split(refs, 5, 4, n_cast, 1, N_GROUPS, N_GROUPS, 2, n_cast, 5)
    put, get = _hand_over(hand_ref)
    _cast_blocks(cast_src, cast_dst, interleave)
    _two_stage(n_tiles, _ffn_and_keep(h_ref, o_ref, put, ffn_w, work),
               lambda: _kv_steps(get, pos_ref, *kv_w, k_refs, v_refs, cos_ref, sin_ref, slab_ref, xn2_ref))


def _ffn_kv(h, pos, ffn_w, kv_w, batch, seq, to_cast):
    m = h.shape[0]
    n_tiles = m // ROW_TILE
    specs, shapes = _residue_major_specs(batch, seq, _BF16, _prev_tile())
    cast_in, cast_out, cast_shapes = _cast_specs(to_cast, n_tiles)
    table_shape = jax.ShapeDtypeStruct((m, LANES), _F32)
    outs = pl.pallas_call(
        functools.partial(_ffn_kv_kernel, n_tiles, tuple(f for _, _, f in to_cast)),
        grid=(n_tiles + 1,),
        in_specs=[_row_spec(_cur_tile(n_tiles)), pl.BlockSpec((ROW_TILE, 1), lambda i: (jnp.maximum(i - 1, 0), 0))]
        + _ffn_weight_specs() + [_const_spec((1, D_MODEL)), _const_spec((1, LANES)),
                                 _const_spec((D_MODEL, QKV_WIDTH)), _const_spec((D_MODEL, QKV_WIDTH))] + cast_in,
        out_specs=[_row_spec(_cur_tile(n_tiles))] + specs * 2 + [_table_spec()] * 2 + cast_out,
        out_shape=[jax.ShapeDtypeStruct((m, D_MODEL), _F32)] + shapes * 2 + [table_shape] * 2 + cast_shapes,
        scratch_shapes=[_HANDOVER, _SLAB, _FFN_WORK[0]] + _FFN_WORK,
        compiler_params=_params(("arbitrary",)),
        name="ffn_kv",
    )(h, pos, *ffn_w, *kv_w, *(w for w, _, _ in to_cast))
    return outs[0], outs[1:4], outs[4:7], outs[7:9], outs[9:]


def _ffn_q_kernel(n_tiles, interleave, h_ref, cos_ref, sin_ref, *refs):
    n_cast = len(interleave)
    ffn_w, q_w, cast_src, (o_ref,), q_refs, cast_dst, (hand_ref, slab_ref, xn2_ref, *work) = _split(
        refs, 5, 2, n_cast, 1, N_GROUPS, n_cast, 5)
    put, get = _hand_over(hand_ref)
    _cast_blocks(cast_src, cast_dst, interleave)
    _two_stage(n_tiles, _ffn_and_keep(h_ref, o_ref, put, ffn_w, work),
               lambda: _q_steps(get, cos_ref, sin_ref, *q_w, q_refs, slab_ref, xn2_ref))


def _ffn_q(h, tables, ffn_w, q_w, batch, seq, to_cast):
    m = h.shape[0]
    n_tiles = m // ROW_TILE
    specs, shapes = _residue_major_specs(batch, seq, _BF16, _prev_tile())
    cast_in, cast_out, cast_shapes = _cast_specs(to_cast, n_tiles)
    outs = pl.pallas_call(
        functools.partial(_ffn_q_kernel, n_tiles, tuple(f for _, _, f in to_cast)),
        grid=(n_tiles + 1,),
        in_specs=[_row_spec(_cur_tile(n_tiles))] + [_table_spec()] * 2 + _ffn_weight_specs()
        + [_const_spec((1, D_MODEL)), _const_spec((D_MODEL, QKV_WIDTH))] + cast_in,
        out_specs=[_row_spec(_cur_tile(n_tiles))] + specs + cast_out,
        out_shape=[jax.ShapeDtypeStruct((m, D_MODEL), _F32)] + shapes + cast_shapes,
        scratch_shapes=[_HANDOVER, _SLAB, _FFN_WORK[0]] + _FFN_WORK,
        compiler_params=_params(("arbitrary",)),
        name="ffn_q",
    )(h, *tables, *ffn_w, *q_w, *(w for w, _, _ in to_cast))
    return outs[0], outs[1:4], outs[4:]


def _merge_ffn_kernel(n_tiles, h_ref, *refs):
    merge_w, o_refs, l_refs, ffn_w, (out_ref, hand_ref, nat_ref, *work) = (
        refs[:2], refs[2:5], refs[5:8], refs[8:13], refs[13:])
    put, get = _hand_over(hand_ref)

    def write(y):
        out_ref[...] = y

    _two_stage(n_tiles,
               lambda: _merge_steps(lambda: h_ref[...], put, *merge_w, o_refs, l_refs, nat_ref),
               lambda: _ffn_steps(get, write, *ffn_w, *work))


def _merge_ffn(h, merge_w, outs, lses, ffn_w, batch, seq):
    m = h.shape[0]
    n_tiles = m // ROW_TILE
    specs, _ = _residue_major_specs(batch, seq, _F32, _cur_tile(n_tiles))
    return pl.pallas_call(
        functools.partial(_merge_ffn_kernel, n_tiles),
        grid=(n_tiles + 1,),
        in_specs=[_row_spec(_cur_tile(n_tiles)), _const_spec((1, D_MODEL)), _const_spec((GROUP_WIDTH, D_MODEL))]
        + specs * 2 + _ffn_weight_specs(),
        out_specs=_row_spec(_prev_tile()),
        out_shape=jax.ShapeDtypeStruct((m, D_MODEL), _F32),
        scratch_shapes=[_HANDOVER, pltpu.VMEM((4, PAIRS_PER_GROUP, ROW_TILE, LANES), _F32)] + _FFN_WORK,
        compiler_params=_params(("arbitrary",)),
        name="merge_ffn",
    )(h, *merge_w, *outs, *lses, *ffn_w)


def _attn_kernel(q_ref, kp_ref, kc_ref, vp_ref, vc_ref, o_ref, lse_ref):
    step = pl.program_id(1)
    n_blk = q_ref.shape[1] // ATTN_BLK
    win = 2 * ATTN_BLK
    qi = lax.broadcasted_iota(jnp.int32, (win, win), 0) % ATTN_BLK
    kj = lax.broadcasted_iota(jnp.int32, (win, win), 1)
    band = jnp.where((kj >= qi) & (kj <= qi + ATTN_BLK), 0.0, NEG_BIG)
    first_lo = jnp.maximum(qi, jnp.where(step == 0, ATTN_BLK, 0))
    band_first = jnp.where((kj >= first_lo) & (kj <= qi + ATTN_BLK), 0.0, NEG_BIG)
    q_lane = lax.broadcasted_iota(jnp.int32, (ATTN_BLK, LANES), 1)
    q_is_a = (q_lane % HEAD_DIM) < HEAD_DIM // 2
    v_lane = lax.broadcasted_iota(jnp.int32, (win, LANES), 1)
    v_is_a = v_lane < HEAD_DIM
    ones_a = jnp.where(v_is_a, 1.0, 0.0).astype(_BF16)
    ones_b = jnp.where(v_is_a, 0.0, 1.0).astype(_BF16)
    o_is_a = q_lane < HEAD_DIM
    nt = (((1,), (1,)), ((), ()))
    work = [(sq, j, p) for sq in range(q_ref.shape[0]) for j in range(n_blk) for p in range(PAIRS_PER_GROUP)]

    def window(prev_ref, cur_ref, sq, j, p):
        pair = slice(p * LANES, (p + 1) * LANES)
        if j == 0:
            return jnp.concatenate([prev_ref[sq, :, pair], cur_ref[sq, 0:ATTN_BLK, pair]], axis=0)
        return cur_ref[sq, (j - 1) * ATTN_BLK:(j + 1) * ATTN_BLK, pair]

    def scores(sq, j, p):
        q = q_ref[sq, j * ATTN_BLK:(j + 1) * ATTN_BLK, p * LANES:(p + 1) * LANES]
        zero = jnp.zeros_like(q)
        q2 = jnp.concatenate([jnp.where(q_is_a, q, zero), jnp.where(q_is_a, zero, q)], axis=0)
        s = lax.dot_general(q2, window(kp_ref, kc_ref, sq, j, p), nt, preferred_element_type=_F32)
        return s + (band_first if j == 0 else band)

    def finish(sq, j, p, s):
        m = jnp.max(s, axis=-1, keepdims=True)
        prob = jnp.exp(s - m).astype(_BF16)
        prob = jnp.concatenate([prob[:ATTN_BLK], prob[ATTN_BLK:]], axis=1)
        v = window(vp_ref, vc_ref, sq, j, p)
        zero = jnp.zeros_like(v)
        v2 = jnp.concatenate([jnp.concatenate([jnp.where(v_is_a, v, zero), ones_a], axis=1),
                              jnp.concatenate([jnp.where(v_is_a, zero, v), ones_b], axis=1)], axis=0)
        r = _dot(prob, v2)
        den = r[:, LANES:]
        rows = slice(j * ATTN_BLK, (j + 1) * ATTN_BLK)
        cols = slice(p * LANES, (p + 1) * LANES)
        o_ref[sq, rows, cols] = r[:, :LANES] * (1.0 / den)
        lse_ref[sq, rows, cols] =jnp.where(o_is_a, m[:ATTN_BLK], m[ATTN_BLK:]) + jnp.log(den)

    ahead = 1
    pending = [scores(*work[n]) for n in range(ahead)]
    for n, item in enumerate(work):
        s = pending.pop(0)
        if n + ahead < len(work):
            pending.append(scores(*work[n + ahead]))
        finish(*item, s)


def _attn_group(q, k, v):
    batch, d, rows, _ = q.shape
    n_seq = batch * d
    q, k, v = (a.reshape(n_seq, rows, GROUP_WIDTH) for a in (q, k, v))
    q_tile = min(ATTN_Q_TILE, rows)
    seqs = ATTN_Q_TILE // q_tile
    blocks_per_step = q_tile // ATTN_BLK

    def cur(s, i):
        return (s, i, 0)

    def prev(s, i):
        return (s, jnp.maximum(i * blocks_per_step - 1, 0), 0)

    cur_spec = pl.BlockSpec((seqs, q_tile, GROUP_WIDTH), cur)
    prev_spec = pl.BlockSpec((seqs, ATTN_BLK, GROUP_WIDTH), prev)
    out_shape = jax.ShapeDtypeStruct((n_seq, rows, GROUP_WIDTH), _F32)
    o, lse = pl.pallas_call(
        _attn_kernel,
        grid=(n_seq // seqs, rows // q_tile),
        in_specs=[cur_spec, prev_spec, cur_spec, prev_spec, cur_spec],
        out_specs=[cur_spec, cur_spec],
        out_shape=[out_shape, out_shape],
        compiler_params=_params(("parallel", "arbitrary")),
        name=f"window_attn_d{d}",
    )(q, k, k, v, v)
    shape = (batch, d, rows, GROUP_WIDTH)
    return o.reshape(shape), lse.reshape(shape)


def kernel(x, positions, norm_gain, ffn_w_gate, ffn_w_up, ffn_w_down, pool_w_in, pool_w_group, pool_scale,
           pool_w_out, kv_norm_gain, w_k, w_v, attn_w_q, attn_w_o):
    batch, seq, _ = x.shape
    m = batch * seq
    gains = norm_gain.reshape(2, 6, 1, D_MODEL)
    bf = lambda w: w.astype(_BF16)

    def ffn_f32(layer, slot):
        return tuple((w, (layer, slot), False) for w in (ffn_w_gate, ffn_w_up, ffn_w_down))

    def ffn_gains(layer, slot):
        return (gains[layer, 4 * slot], 0.5 * gains[layer, 4 * slot + 1])

    pos = positions.astype(_F32).reshape(m, 1)
    inv_freq = ROPE_THETA ** (-jnp.arange(0, HEAD_DIM, 2, dtype=_F32) / HEAD_DIM)
    invf = jnp.tile(inv_freq, LANES // (HEAD_DIM // 2)).reshape(1, LANES)

    h = x.reshape(m, D_MODEL)
    pool_w = (gains[0, 2], gains[0, 3], bf(pool_w_in[0]), pool_w_group[0],
              pool_scale[0].reshape(1, D_MODEL), pool_w_out[0])
    ffn00 = tuple(bf(w[lead]) for w, lead, _ in ffn_f32(0, 0))
    h, (*ffn01, wk_bf, wv_bf) = _ffn_pool(h, ffn_gains(0, 0) + ffn00, pool_w, seq,
                                          to_cast=ffn_f32(0, 1) + ((w_k, (), True), (w_v, (), False)))
    kv_w = (kv_norm_gain.reshape(1, D_MODEL), invf, wk_bf, wv_bf)
    h, k_sh, v_sh, tables, (*ffn10, wq_bf) = _ffn_kv(h, pos, ffn_gains(0, 1) + tuple(ffn01), kv_w, batch, seq,
                                                     to_cast=ffn_f32(1, 0) + ((attn_w_q, (0,), True),))
    h, q, (*ffn11, wo_bf) = _ffn_q(h, tables, ffn_gains(1, 0) + tuple(ffn10), (gains[1, 2], wq_bf), batch, seq,
                                   to_cast=ffn_f32(1, 1) + ((attn_w_o, (0,), False),))
    outs, lses = [], []
    for g in range(N_GROUPS):
        o, lse = _attn_group(q[g], k_sh[g], v_sh[g])
        outs.append(o)
        lses.append(lse)
    h = _merge_ffn(h, (gains[1, 3], wo_bf), outs, lses, ffn_gains(1, 1) + tuple(ffn11), batch, seq)
    return h.reshape(batch, seq, D_MODEL)
```

```python
import functools

import jax
import jax.numpy as jnp
from jax import lax
from jax.experimental import pallas as pl
from jax.experimental.pallas import tpu as pltpu

D_MODEL = 1024
D_FF = 2816
POOL_WINDOWS = (2, 4, 8, 16)
POOL_GROUP_DIM = D_MODEL // len(POOL_WINDOWS)
POOL_HALO = 16
ATTN_DILATIONS = (1, 4, 16)
N_GROUPS = len(ATTN_DILATIONS)
ATTN_BLK = 128
HEAD_DIM = 64
HEADS_PER_GROUP = 8
GROUP_WIDTH = HEADS_PER_GROUP * HEAD_DIM
QKV_WIDTH = N_GROUPS * GROUP_WIDTH
ROPE_THETA = 10000.0
RMS_EPS = 1e-6
NEG_BIG = -1e30

LANES = 128
PAIRS_PER_GROUP = GROUP_WIDTH // LANES
QKV_TILES = QKV_WIDTH // LANES
ROW_TILE = 512
ATTN_Q_TILE = 2048
FF_CHUNK = 256
VMEM_LIMIT_BYTES = 56 * 1024 * 1024

_F32 = jnp.float32
_BF16 = jnp.bfloat16


def _rms_norm(x, gain):
    ms = jnp.mean(x * x, axis=-1, keepdims=True)
    return x * lax.rsqrt(ms + RMS_EPS) * gain


def _dot(a, b):
    return jnp.dot(a, b, preferred_element_type=_F32)


def _const_spec(shape):
    return pl.BlockSpec(shape, lambda *_: (0,) * len(shape), pipeline_mode=pl.Buffered(1))


def _params(semantics):
    return pltpu.CompilerParams(dimension_semantics=semantics, vmem_limit_bytes=VMEM_LIMIT_BYTES)


def _ffn_steps(read, write, gpre_ref, gpost_ref, wg_ref, wu_ref, wd_ref, xn_ref, acc_ref):
    def start():
        xn_ref[...] = _rms_norm(read(), gpre_ref[...]).astype(_BF16)

    def chunk(c):
        def run():
            hi = min(c + FF_CHUNK, D_FF)
            g = _dot(xn_ref[...], wg_ref[:, c:hi])
            u = _dot(xn_ref[...], wu_ref[:, c:hi])
            a = (g * jax.nn.sigmoid(g) * u).astype(_BF16)
            down = _dot(a, wd_ref[c:hi, :])
            if c == 0:
                acc_ref[...] = down
            else:
                acc_ref[...] += down
        return run

    def end():
        write(read() + _rms_norm(acc_ref[...], gpost_ref[...]))

    return [start] + [chunk(c) for c in range(0, D_FF, FF_CHUNK)] + [end]


def _pool_out_weight(wgrp_ref, scale_ref, wout_ref, weff_ref):
    for g in range(len(POOL_WINDOWS)):
        rows = slice(g * POOL_GROUP_DIM, (g + 1) * POOL_GROUP_DIM)
        weff_ref[rows, :] = jnp.dot(wgrp_ref[g] * scale_ref[:, rows], wout_ref[rows, :], preferred_element_type=_F32,
                                    precision=lax.Precision.HIGHEST).astype(_BF16)


def _pool_steps(read, write, j, gpre_ref, gpost_ref, win_ref, weff_ref, ubuf_ref):
    st = {"p": []}

    def start():
        st["x"] = read()
        hm = _rms_norm(st["x"], gpre_ref[...]).astype(_BF16)
        st["u"] = _dot(hm, win_ref[...])
        ubuf_ref[0:POOL_HALO, :] = jnp.where(j == 0, 0.0, ubuf_ref[0:POOL_HALO, :])
        ubuf_ref[POOL_HALO:, :] = st["u"]

    def group(g, w):
        def run():
            tm = st["x"].shape[0]
            cols = slice(g * POOL_GROUP_DIM, (g + 1) * POOL_GROUP_DIM)
            t = j * tm + lax.broadcasted_iota(jnp.int32, (tm, 1), 0)
            s = ubuf_ref[:, cols]
            k = 1
            while k < w:
                s = s + pltpu.roll(s, k, axis=0)
                k *= 2
            count = jnp.minimum(t + 1, w).astype(_F32)
            p = s[POOL_HALO:] / count - st["u"][:, cols]
            st["p"].append(p.astype(_BF16))
        return run

    def end():
        tm = st["x"].shape[0]
        mix = _dot(jnp.concatenate(st["p"], axis=-1), weff_ref[...])
        ubuf_ref[0:POOL_HALO, :] = st["u"][tm - POOL_HALO:, :]
        write(st["x"] + _rms_norm(mix, gpost_ref[...]))

    return [start] + [group(g, w) for g, w in enumerate(POOL_WINDOWS)] + [end]


def _rope_tables(c, s):
    lane = lax.broadcasted_iota(jnp.int32, c.shape, 1)
    return c, jnp.where(lane < LANES // 2, -s, s)


def _rope_tile(xt, tables):
    c, s_signed = tables
    return xt * c + pltpu.roll(xt, LANES // 2, axis=1) * s_signed


def _pair_interleaved(t):
    half = HEAD_DIM // 2
    block = lax.broadcasted_iota(jnp.int32, t.shape, 1) // half
    return jnp.where(block == 1, pltpu.roll(t, LANES - half, axis=1),
                     jnp.where(block == 2, pltpu.roll(t, half, axis=1), t))


def _projection_steps(st, xn_ref, w_ref, slab_ref, out_refs, rope=False, scale=None):
    def project(g):
        def run():
            y = _dot(xn_ref[...], w_ref[:, g * GROUP_WIDTH:(g + 1) * GROUP_WIDTH])
            for p in range(PAIRS_PER_GROUP):
                yt = y[:, p * LANES:(p + 1) * LANES]
                if rope:
                    yt = _rope_tile(yt, st["tables"])
                if scale is not None:
                    yt = yt * scale
                slab_ref[g * PAIRS_PER_GROUP + p] = yt
        return run

    def scatter(g, d):
        def run():
            n = slab_ref.shape[1] // d
            for r in range(d):
                for p in range(PAIRS_PER_GROUP):
                    rows = slab_ref[g * PAIRS_PER_GROUP + p, pl.ds(r, n, stride=d), :]
                    out_refs[g][0, r, :, p * LANES:(p + 1) * LANES] = rows.astype(_BF16)
        return run

    steps = []
    for g, d in enumerate(ATTN_DILATIONS):
        steps += [project(g), scatter(g, d)]
    return steps


def _kv_steps(read, pos_ref, gain_ref, invf_ref, wk_ref, wv_ref, k_refs, v_refs, cos_ref, sin_ref, slab_ref, xn_ref):
    st = {}

    def start():
        xn_ref[...] = _rms_norm(read(), gain_ref[...]).astype(_BF16)
        ang = pos_ref[...] * invf_ref[...]
        cos_ref[...] = jnp.cos(ang)
        sin_ref[...] = jnp.sin(ang)
        st["tables"] = _rope_tables(cos_ref[...], sin_ref[...])

    return ([start] + _projection_steps(st, xn_ref, wk_ref, slab_ref, k_refs, rope=True)
            + _projection_steps(st, xn_ref, wv_ref, slab_ref, v_refs))


def _q_steps(read, cos_ref, sin_ref, gain_ref, wq_ref, q_refs, slab_ref, xn_ref):
    st = {}

    def start():
        xn_ref[...] = _rms_norm(read(), gain_ref[...]).astype(_BF16)
        st["tables"] = _rope_tables(cos_ref[...], sin_ref[...])

    return [start] + _projection_steps(st, xn_ref, wq_ref, slab_ref, q_refs, rope=True, scale=HEAD_DIM ** -0.5)


def _merge_steps(read, write, gpost_ref, wo_ref, o_refs, l_refs, nat_ref):
    st = {"o": []}
    permuted = ((o_refs[1], ATTN_DILATIONS[1]), (l_refs[1], ATTN_DILATIONS[1]),
                (o_refs[2], ATTN_DILATIONS[2]), (l_refs[2], ATTN_DILATIONS[2]))

    def lane_tile(p):
        def run():
            cols = slice(p * LANES, (p + 1) * LANES)
            for a, (ref, d) in enumerate(permuted):
                n = nat_ref.shape[2] // d
                for r in range(d):
                    nat_ref[a, p, pl.ds(r, n, stride=d), :] = ref[0, r, :, cols]
            o0, l0 = o_refs[0][0, 0, :, cols], l_refs[0][0, 0, :, cols]
            o1, l1, o2, l2 = (nat_ref[a, p] for a in range(4))
            top = jnp.maximum(jnp.maximum(l0, l1), l2)
            e0, e1, e2 = jnp.exp(l0 - top), jnp.exp(l1 - top), jnp.exp(l2 - top)
            st["o"].append(((e0 * o0 + e1 * o1 + e2 * o2) / (e0 + e1 + e2)).astype(_BF16))
        return run

    def end():
        mix = _dot(jnp.concatenate(st["o"], axis=-1), wo_ref[...])
        write(read() + _rms_norm(mix, gpost_ref[...]))

    return [lane_tile(p) for p in range(PAIRS_PER_GROUP)] + [end]


def _run(steps):
    for step in steps:
        step()


def _two_stage(n_tiles, first, second, init=None):
    i = pl.program_id(0)
    if init is not None:
        pl.when(i == 0)(init)

    @pl.when(i > 0)
    def _():
        _run(second())

    @pl.when(i < n_tiles)
    def _():
        _run(first())


def _cur_tile(n_tiles):
    return lambda i: jnp.minimum(i, n_tiles - 1)


def _prev_tile():
    return lambda i: jnp.maximum(i - 1, 0)


def _row_spec(tile_of):
    return pl.BlockSpec((ROW_TILE, D_MODEL), lambda i: (tile_of(i), 0))


def _residue_major_specs(batch, seq, dtype, tile_of):
    tiles_per_seq = seq // ROW_TILE
    specs, shapes = [], []
    for d in ATTN_DILATIONS:
        specs.append(pl.BlockSpec((1, d, ROW_TILE // d, GROUP_WIDTH),
                                  lambda i: (tile_of(i) // tiles_per_seq, 0, tile_of(i) % tiles_per_seq, 0)))
        shapes.append(jax.ShapeDtypeStruct((batch, d, seq // d, GROUP_WIDTH), dtype))
    return specs, shapes


def _ffn_weight_specs():
    return [_const_spec((1, D_MODEL)), _const_spec((1, D_MODEL)),
            _const_spec((D_MODEL, D_FF)), _const_spec((D_MODEL, D_FF)), _const_spec((D_FF, D_MODEL))]


_HANDOVER = pltpu.VMEM((ROW_TILE, D_MODEL), _F32)
_SLAB = pltpu.VMEM((QKV_TILES, ROW_TILE, LANES), _F32)
_FFN_WORK = [pltpu.VMEM((ROW_TILE, D_MODEL), _BF16), pltpu.VMEM((ROW_TILE, D_MODEL), _F32)]
BF16_SUBLANES = 16


def _hand_over(hand_ref):
    def put(y):
        hand_ref[...] = y

    def get():
        return hand_ref[...]

    return put, get


def _split(refs, *counts):
    out, at = [], 0
    for n in counts:
        out.append(refs[at:at + n])
        at += n
    assert at == len(refs)
    return out


def _cast_specs(weights, n_tiles):
    in_specs, out_specs, out_shapes = [], [], []
    for w, lead, _ in weights:
        rows, cols = w.shape[len(lead):]
        n_blocks = max(n for n in range(1, n_tiles + 1) if rows % (n * BF16_SUBLANES) == 0)

        def block(i, n_blocks=n_blocks):
            return (jnp.minimum(i, n_blocks - 1), 0)

        in_specs.append(pl.BlockSpec((None,) * len(lead) + (rows // n_blocks, cols),
                                     lambda i, lead=lead, block=block: lead + block(i)))
        out_specs.append(pl.BlockSpec((rows // n_blocks, cols), block))
        out_shapes.append(jax.ShapeDtypeStruct((rows, cols), _BF16))
    return in_specs, out_specs, out_shapes


def _cast_blocks(src_refs, dst_refs, interleave):
    for src, dst, flag in zip(src_refs, dst_refs, interleave):
        if flag:
            for j in range(src.shape[1] // LANES):
                cols = slice(j * LANES, (j + 1) * LANES)
                dst[:, cols] = _pair_interleaved(src[:, cols]).astype(_BF16)
        else:
            dst[...] = src[...].astype(_BF16)


def _ffn_pool_kernel(n_tiles, tiles_per_seq, interleave, h_ref, *refs):
    n_cast = len(interleave)
    ffn_w, pool_w, cast_src, (o_ref,), cast_dst, (hand_ref, ubuf_ref, weff_ref, *work) = _split(
        refs, 5, 6, n_cast, 1, n_cast, 5)
    gpre_ref, gpost_ref, win_ref, wgrp_ref, scale_ref, wout_ref = pool_w
    put, get = _hand_over(hand_ref)
    j = (pl.program_id(0) - 1) % tiles_per_seq

    def write(y):
        o_ref[...] = y

    def init():
        ubuf_ref[0:POOL_HALO, :] = jnp.zeros((POOL_HALO, D_MODEL), _F32)
        _pool_out_weight(wgrp_ref, scale_ref, wout_ref, weff_ref)

    _cast_blocks(cast_src, cast_dst, interleave)
    _two_stage(n_tiles,
               lambda: _ffn_steps(lambda: h_ref[...], put, *ffn_w, *work),
               lambda: _pool_steps(get, write, j, gpre_ref, gpost_ref, win_ref, weff_ref, ubuf_ref),
               init=init)


def _ffn_pool(h, ffn_w, pool_w, seq, to_cast):
    m = h.shape[0]
    n_tiles = m // ROW_TILE
    n_grp = len(POOL_WINDOWS)
    cast_in, cast_out, cast_shapes = _cast_specs(to_cast, n_tiles)
    outs = pl.pallas_call(
        functools.partial(_ffn_pool_kernel, n_tiles, seq // ROW_TILE, tuple(f for _, _, f in to_cast)),
        grid=(n_tiles + 1,),
        in_specs=[_row_spec(_cur_tile(n_tiles))] + _ffn_weight_specs() + [
            _const_spec((1, D_MODEL)), _const_spec((1, D_MODEL)), _const_spec((D_MODEL, D_MODEL)),
            _const_spec((n_grp, POOL_GROUP_DIM, POOL_GROUP_DIM)), _const_spec((1, D_MODEL)),
            _const_spec((D_MODEL, D_MODEL))] + cast_in,
        out_specs=[_row_spec(_prev_tile())] + cast_out,
        out_shape=[jax.ShapeDtypeStruct((m, D_MODEL), _F32)] + cast_shapes,
        scratch_shapes=[_HANDOVER, pltpu.VMEM((POOL_HALO + ROW_TILE, D_MODEL), _F32),
                        pltpu.VMEM((D_MODEL, D_MODEL), _BF16)] + _FFN_WORK,
        compiler_params=_params(("arbitrary",)),
        name="ffn_pool",
    )(h, *ffn_w, *pool_w, *(w for w, _, _ in to_cast))
    return outs[0], outs[1:]


def _ffn_and_keep(h_ref, o_ref, put, ffn_w, work):
    def write(y):
        o_ref[...] = y
        put(y)

    return lambda: _ffn_steps(lambda: h_ref[...], write, *ffn_w, *work)


def _table_spec():
    return pl.BlockSpec((ROW_TILE, LANES), lambda i: (jnp.maximum(i - 1, 0), 0))


def _ffn_kv_kernel(n_tiles, interleave, h_ref, pos_ref, *refs):
    n_cast = len(interleave)
    (ffn_w, kv_w, cast_src, (o_ref,), k_refs, v_refs, (cos_ref, sin_ref), cast_dst,
     (hand_ref, slab_ref, xn2_ref, *work)) = _split(refs, 5, 4, n_cast, 1, N_GROUPS, N_GROUPS, 2, n_cast, 5)
    put, get = _hand_over(hand_ref)
    _cast_blocks(cast_src, cast_dst, interleave)
    _two_stage(n_tiles, _ffn_and_keep(h_ref, o_ref, put, ffn_w, work),
               lambda: _kv_steps(get, pos_ref, *kv_w, k_refs, v_refs, cos_ref, sin_ref, slab_ref, xn2_ref))


def _ffn_kv(h, pos, ffn_w, kv_w, batch, seq, to_cast):
    m = h.shape[0]
    n_tiles = m // ROW_TILE
    specs, shapes = _residue_major_specs(batch, seq, _BF16, _prev_tile())
    cast_in, cast_out, cast_shapes = _cast_specs(to_cast, n_tiles)
    table_shape = jax.ShapeDtypeStruct((m, LANES), _F32)
    outs = pl.pallas_call(
        functools.partial(_ffn_kv_kernel, n_tiles, tuple(f for _, _, f in to_cast)),
        grid=(n_tiles + 1,),
        in_specs=[_row_spec(_cur_tile(n_tiles)), pl.BlockSpec((ROW_TILE, 1), lambda i: (jnp.maximum(i - 1, 0), 0))]
        + _ffn_weight_specs() + [_const_spec((1, D_MODEL)), _const_spec((1, LANES)),
                                 _const_spec((D_MODEL, QKV_WIDTH)), _const_spec((D_MODEL, QKV_WIDTH))] + cast_in,
        out_specs=[_row_spec(_cur_tile(n_tiles))] + specs * 2 + [_table_spec()] * 2 + cast_out,
        out_shape=[jax.ShapeDtypeStruct((m, D_MODEL), _F32)] + shapes * 2 + [table_shape] * 2 + cast_shapes,
        scratch_shapes=[_HANDOVER, _SLAB, _FFN_WORK[0]] + _FFN_WORK,
        compiler_params=_params(("arbitrary",)),
        name="ffn_kv",
    )(h, pos, *ffn_w, *kv_w, *(w for w, _, _ in to_cast))
    return outs[0], outs[1:4], outs[4:7], outs[7:9], outs[9:]


def _ffn_q_kernel(n_tiles, interleave, h_ref, cos_ref, sin_ref, *refs):
    n_cast = len(interleave)
    ffn_w, q_w, cast_src, (o_ref,), q_refs, cast_dst, (hand_ref, slab_ref, xn2_ref, *work) = _split(
        refs, 5, 2, n_cast, 1, N_GROUPS, n_cast, 5)
    put, get = _hand_over(hand_ref)
    _cast_blocks(cast_src, cast_dst, interleave)
    _two_stage(n_tiles, _ffn_and_keep(h_ref, o_ref, put, ffn_w, work),
               lambda: _q_steps(get, cos_ref, sin_ref, *q_w, q_refs, slab_ref, xn2_ref))


def _ffn_q(h, tables, ffn_w, q_w, batch, seq, to_cast):
    m = h.shape[0]
    n_tiles = m // ROW_TILE
    specs, shapes = _residue_major_specs(batch, seq, _BF16, _prev_tile())
    cast_in, cast_out, cast_shapes = _cast_specs(to_cast, n_tiles)
    outs = pl.pallas_call(
        functools.partial(_ffn_q_kernel, n_tiles, tuple(f for _, _, f in to_cast)),
        grid=(n_tiles + 1,),
        in_specs=[_row_spec(_cur_tile(n_tiles))] + [_table_spec()] * 2 + _ffn_weight_specs()
        + [_const_spec((1, D_MODEL)), _const_spec((D_MODEL, QKV_WIDTH))] + cast_in,
        out_specs=[_row_spec(_cur_tile(n_tiles))] + specs + cast_out,
        out_shape=[jax.ShapeDtypeStruct((m, D_MODEL), _F32)] + shapes + cast_shapes,
        scratch_shapes=[_HANDOVER, _SLAB, _FFN_WORK[0]] + _FFN_WORK,
        compiler_params=_params(("arbitrary",)),
        name="ffn_q",
    )(h, *tables, *ffn_w, *q_w, *(w for w, _, _ in to_cast))
    return outs[0], outs[1:4], outs[4:]


def _merge_ffn_kernel(n_tiles, h_ref, *refs):
    merge_w, o_refs, l_refs, ffn_w, (out_ref, hand_ref, nat_ref, *work) = (
        refs[:2], refs[2:5], refs[5:8], refs[8:13], refs[13:])
    put, get = _hand_over(hand_ref)

    def write(y):
        out_ref[...] = y

    _two_stage(n_tiles,
               lambda: _merge_steps(lambda: h_ref[...], put, *merge_w, o_refs, l_refs, nat_ref),
               lambda: _ffn_steps(get, write, *ffn_w, *work))


def _merge_ffn(h, merge_w, outs, lses, ffn_w, batch, seq):
    m = h.shape[0]
    n_tiles = m // ROW_TILE
    specs, _ = _residue_major_specs(batch, seq, _F32, _cur_tile(n_tiles))
    return pl.pallas_call(
        functools.partial(_merge_ffn_kernel, n_tiles),
        grid=(n_tiles + 1,),
        in_specs=[_row_spec(_cur_tile(n_tiles)), _const_spec((1, D_MODEL)), _const_spec((GROUP_WIDTH, D_MODEL))]
        + specs * 2 + _ffn_weight_specs(),
        out_specs=_row_spec(_prev_tile()),
        out_shape=jax.ShapeDtypeStruct((m, D_MODEL), _F32),
        scratch_shapes=[_HANDOVER, pltpu.VMEM((4, PAIRS_PER_GROUP, ROW_TILE, LANES), _F32)] + _FFN_WORK,
        compiler_params=_params(("arbitrary",)),
        name="merge_ffn",
    )(h, *merge_w, *outs, *lses, *ffn_w)


def _attn_kernel(q_ref, kp_ref, kc_ref, vp_ref, vc_ref, o_ref, lse_ref):
    step = pl.program_id(1)
    n_blk = q_ref.shape[1] // ATTN_BLK
    win = 2 * ATTN_BLK
    qi = lax.broadcasted_iota(jnp.int32, (win, win), 0) % ATTN_BLK
    kj = lax.broadcasted_iota(jnp.int32, (win, win), 1)
    band = jnp.where((kj >= qi) & (kj <= qi + ATTN_BLK), 0.0, NEG_BIG)
    first_lo = jnp.maximum(qi, jnp.where(step == 0, ATTN_BLK, 0))
    band_first = jnp.where((kj >= first_lo) & (kj <= qi + ATTN_BLK), 0.0, NEG_BIG)
    q_lane = lax.broadcasted_iota(jnp.int32, (ATTN_BLK, LANES), 1)
    q_is_a = (q_lane % HEAD_DIM) < HEAD_DIM // 2
    v_lane = lax.broadcasted_iota(jnp.int32, (win, LANES), 1)
    v_is_a = v_lane < HEAD_DIM
    ones_a = jnp.where(v_is_a, 1.0, 0.0).astype(_BF16)
    ones_b = jnp.where(v_is_a, 0.0, 1.0).astype(_BF16)
    o_is_a = q_lane < HEAD_DIM
    nt = (((1,), (1,)), ((), ()))
    work = [(sq, j, p) for sq in range(q_ref.shape[0]) for j in range(n_blk) for p in range(PAIRS_PER_GROUP)]

    def window(prev_ref, cur_ref, sq, j, p):
        pair = slice(p * LANES, (p + 1) * LANES)
        if j == 0:
            return jnp.concatenate([prev_ref[sq, :, pair], cur_ref[sq, 0:ATTN_BLK, pair]], axis=0)
        return cur_ref[sq, (j - 1) * ATTN_BLK:(j + 1) * ATTN_BLK, pair]

    def scores(sq, j, p):
        q = q_ref[sq, j * ATTN_BLK:(j + 1) * ATTN_BLK, p * LANES:(p + 1) * LANES]
        zero = jnp.zeros_like(q)
        q2 = jnp.concatenate([jnp.where(q_is_a, q, zero), jnp.where(q_is_a, zero, q)], axis=0)
        s = lax.dot_general(q2, window(kp_ref, kc_ref, sq, j, p), nt, preferred_element_type=_F32)
        return s + (band_first if j == 0 else band)

    def finish(sq, j, p, s):
        m = jnp.max(s, axis=-1, keepdims=True)
        prob = jnp.exp(s - m).astype(_BF16)
        prob = jnp.concatenate([prob[:ATTN_BLK], prob[ATTN_BLK:]], axis=1)
        v = window(vp_ref, vc_ref, sq, j, p)
        zero = jnp.zeros_like(v)
        v2 = jnp.concatenate([jnp.concatenate([jnp.where(v_is_a, v, zero), ones_a], axis=1),
                              jnp.concatenate([jnp.where(v_is_a, zero, v), ones_b], axis=1)], axis=0)
        r = _dot(prob, v2)
        den = r[:, LANES:]
        rows = slice(j * ATTN_BLK, (j + 1) * ATTN_BLK)
        cols = slice(p * LANES, (p + 1) * LANES)
        o_ref[sq, rows, cols] = r[:, :LANES] * (1.0 / den)
        lse_ref[sq, rows, cols] =jnp.where(o_is_a, m[:ATTN_BLK], m[ATTN_BLK:]) + jnp.log(den)

    ahead = 1
    pending = [scores(*work[n]) for n in range(ahead)]
    for n, item in enumerate(work):
        s = pending.pop(0)
        if n + ahead < len(work):
            pending.append(scores(*work[n + ahead]))
        finish(*item, s)


def _attn_group(q, k, v):
    batch, d, rows, _ = q.shape
    n_seq = batch * d
    q, k, v = (a.reshape(n_seq, rows, GROUP_WIDTH) for a in (q, k, v))
    q_tile = min(ATTN_Q_TILE, rows)
    seqs = ATTN_Q_TILE // q_tile
    blocks_per_step = q_tile // ATTN_BLK

    def cur(s, i):
        return (s, i, 0)

    def prev(s, i):
        return (s, jnp.maximum(i * blocks_per_step - 1, 0), 0)

    cur_spec = pl.BlockSpec((seqs, q_tile, GROUP_WIDTH), cur)
    prev_spec = pl.BlockSpec((seqs, ATTN_BLK, GROUP_WIDTH), prev)
    out_shape = jax.ShapeDtypeStruct((n_seq, rows, GROUP_WIDTH), _F32)
    o, lse = pl.pallas_call(
        _attn_kernel,
        grid=(n_seq // seqs, rows // q_tile),
        in_specs=[cur_spec, prev_spec, cur_spec, prev_spec, cur_spec],
        out_specs=[cur_spec, cur_spec],
        out_shape=[out_shape, out_shape],
        compiler_params=_params(("parallel", "arbitrary")),
        name=f"window_attn_d{d}",
    )(q, k, k, v, v)
    shape = (batch, d, rows, GROUP_WIDTH)
    return o.reshape(shape), lse.reshape(shape)


def kernel(x, positions, norm_gain, ffn_w_gate, ffn_w_up, ffn_w_down, pool_w_in, pool_w_group, pool_scale,
           pool_w_out, kv_norm_gain, w_k, w_v, attn_w_q, attn_w_o):
    batch, seq, _ = x.shape
    m = batch * seq
    gains = norm_gain.reshape(2, 6, 1, D_MODEL)
    bf = lambda w: w.astype(_BF16)

    def ffn_f32(layer, slot):
        return tuple((w, (layer, slot), False) for w in (ffn_w_gate, ffn_w_up, ffn_w_down))

    def ffn_gains(layer, slot):
        return (gains[layer, 4 * slot], 0.5 * gains[layer, 4 * slot + 1])

    pos = positions.astype(_F32).reshape(m, 1)
    inv_freq = ROPE_THETA ** (-jnp.arange(0, HEAD_DIM, 2, dtype=_F32) / HEAD_DIM)
    invf = jnp.tile(inv_freq, LANES // (HEAD_DIM // 2)).reshape(1, LANES)

    h = x.reshape(m, D_MODEL)
    pool_w = (gains[0, 2], gains[0, 3], bf(pool_w_in[0]), pool_w_group[0],
              pool_scale[0].reshape(1, D_MODEL), pool_w_out[0])
    ffn00 = tuple(bf(w[lead]) for w, lead, _ in ffn_f32(0, 0))
    h, (*ffn01, wk_bf, wv_bf) = _ffn_pool(h, ffn_gains(0, 0) + ffn00, pool_w, seq,
                                          to_cast=ffn_f32(0, 1) + ((w_k, (), True), (w_v, (), False)))
    kv_w = (kv_norm_gain.reshape(1, D_MODEL), invf, wk_bf, wv_bf)
    h, k_sh, v_sh, tables, (*ffn10, wq_bf) = _ffn_kv(h, pos, ffn_gains(0, 1) + tuple(ffn01), kv_w, batch, seq,
                                                     to_cast=ffn_f32(1, 0) + ((attn_w_q, (0,), True),))
    h, q, (*ffn11, wo_bf) = _ffn_q(h, tables, ffn_gains(1, 0) + tuple(ffn10), (gains[1, 2], wq_bf), batch, seq,
                                   to_cast=ffn_f32(1, 1) + ((attn_w_o, (0,), False),))
    outs, lses = [], []
    for g in range(N_GROUPS):
        o, lse = _attn_group(q[g], k_sh[g], v_sh[g])
        outs.append(o)
        lses.append(lse)
    h = _merge_ffn(h, (gains[1, 3], wo_bf), outs, lses, ffn_gains(1, 1) + tuple(ffn11), batch, seq)
    return h.reshape(batch, seq, D_MODEL)
```

```python
import functools
import math

import jax
import jax.numpy as jnp
from jax import lax
from jax.experimental import pallas as pl
from jax.experimental.pallas import tpu as pltpu

D_MODEL = 1024
D_FF = 2816
POOL_WINDOWS = (2, 4, 8, 16)
POOL_GROUP_DIM = D_MODEL // len(POOL_WINDOWS)
POOL_HALO = 16
ATTN_DILATIONS = (1, 4, 16)
N_GROUPS = len(ATTN_DILATIONS)
ATTN_BLK = 128
HEAD_DIM = 64
HEADS_PER_GROUP = 8
GROUP_WIDTH = HEADS_PER_GROUP * HEAD_DIM
QKV_WIDTH = N_GROUPS * GROUP_WIDTH
ROPE_THETA = 10000.0
Q_SCALE = HEAD_DIM ** -0.5 * math.log2(math.e)
RMS_EPS = 1e-6
NEG_BIG = -1e30

LANES = 128
PAIRS_PER_GROUP = GROUP_WIDTH // LANES
QKV_TILES = QKV_WIDTH // LANES
ROW_TILE = 512
ATTN_Q_TILE = 2048
FF_CHUNK = 256
VMEM_LIMIT_BYTES = 56 * 1024 * 1024

_F32 = jnp.float32
_BF16 = jnp.bfloat16


def _rms_norm(x, gain):
    ms = jnp.mean(x * x, axis=-1, keepdims=True)
    return x * lax.rsqrt(ms + RMS_EPS) * gain


def _dot(a, b):
    return jnp.dot(a, b, preferred_element_type=_F32)


def _const_spec(shape):
    return pl.BlockSpec(shape, lambda *_: (0,) * len(shape), pipeline_mode=pl.Buffered(1))


def _params(semantics):
    return pltpu.CompilerParams(dimension_semantics=semantics, vmem_limit_bytes=VMEM_LIMIT_BYTES)


def _ffn_steps(read, write, gpre_ref, gpost_ref, wg_ref, wu_ref, wd_ref, xn_ref, acc_ref):
    def start():
        xn_ref[...] = _rms_norm(read(), gpre_ref[...]).astype(_BF16)

    def chunk(c):
        def run():
            hi = min(c + FF_CHUNK, D_FF)
            g = _dot(xn_ref[...], wg_ref[:, c:hi])
            u = _dot(xn_ref[...], wu_ref[:, c:hi])
            a = (g * jax.nn.sigmoid(g) * u).astype(_BF16)
            down = _dot(a, wd_ref[c:hi, :])
            if c == 0:
                acc_ref[...] = down
            else:
                acc_ref[...] += down
        return run

    def end():
        write(read() + _rms_norm(acc_ref[...], gpost_ref[...]))

    return [start] + [chunk(c) for c in range(0, D_FF, FF_CHUNK)] + [end]


def _pool_out_weight(wgrp_ref, scale_ref, wout_ref, weff_ref):
    for g in range(len(POOL_WINDOWS)):
        rows = slice(g * POOL_GROUP_DIM, (g + 1) * POOL_GROUP_DIM)
        weff_ref[rows, :] = jnp.dot(wgrp_ref[g] * scale_ref[:, rows], wout_ref[rows, :], preferred_element_type=_F32,
                                    precision=lax.Precision.HIGHEST).astype(_BF16)


def _pool_steps(read, write, j, gpre_ref, gpost_ref, win_ref, weff_ref, ubuf_ref):
    st = {"p": []}

    def start():
        st["x"] = read()
        hm = _rms_norm(st["x"], gpre_ref[...]).astype(_BF16)
        st["u"] = _dot(hm, win_ref[...])
        ubuf_ref[0:POOL_HALO, :] = jnp.where(j == 0, 0.0, ubuf_ref[0:POOL_HALO, :])
        ubuf_ref[POOL_HALO:, :] = st["u"]

    def group(g, w):
        def run():
            tm = st["x"].shape[0]
            cols = slice(g * POOL_GROUP_DIM, (g + 1) * POOL_GROUP_DIM)
            t = j * tm + lax.broadcasted_iota(jnp.int32, (tm, 1), 0)
            s = ubuf_ref[:, cols]
            k = 1
            while k < w:
                s = s + pltpu.roll(s, k, axis=0)
                k *= 2
            count = jnp.minimum(t + 1, w).astype(_F32)
            p = s[POOL_HALO:] / count - st["u"][:, cols]
            st["p"].append(p.astype(_BF16))
        return run

    def end():
        tm = st["x"].shape[0]
        mix = _dot(jnp.concatenate(st["p"], axis=-1), weff_ref[...])
        ubuf_ref[0:POOL_HALO, :] = st["u"][tm - POOL_HALO:, :]
        write(st["x"] + _rms_norm(mix, gpost_ref[...]))

    return [start] + [group(g, w) for g, w in enumerate(POOL_WINDOWS)] + [end]


def _rope_tables(c, s):
    lane = lax.broadcasted_iota(jnp.int32, c.shape, 1)
    return c, jnp.where(lane < LANES // 2, -s, s)


def _rope_tile(xt, tables):
    c, s_signed = tables
    return xt * c + pltpu.roll(xt, LANES // 2, axis=1) * s_signed


def _pair_interleaved(t):
    half = HEAD_DIM // 2
    block = lax.broadcasted_iota(jnp.int32, t.shape, 1) // half
    return jnp.where(block == 1, pltpu.roll(t, LANES - half, axis=1),
                     jnp.where(block == 2, pltpu.roll(t, half, axis=1), t))


def _projection_steps(st, xn_ref, w_ref, slab_ref, out_refs, rope=False, scale=None):
    def project(g):
        def run():
            y = _dot(xn_ref[...], w_ref[:, g * GROUP_WIDTH:(g + 1) * GROUP_WIDTH])
            for p in range(PAIRS_PER_GROUP):
                yt = y[:, p * LANES:(p + 1) * LANES]
                if rope:
                    yt = _rope_tile(yt, st["tables"])
                if scale is not None:
                    yt = yt * scale
                slab_ref[g * PAIRS_PER_GROUP + p] = yt
        return run

    def scatter(g, d):
        def run():
            n = slab_ref.shape[1] // d
            for r in range(d):
                for p in range(PAIRS_PER_GROUP):
                    rows = slab_ref[g * PAIRS_PER_GROUP + p, pl.ds(r, n, stride=d), :]
                    out_refs[g][0, r, :, p * LANES:(p + 1) * LANES] = rows.astype(_BF16)
        return run

    steps = []
    for g, d in enumerate(ATTN_DILATIONS):
        steps += [project(g), scatter(g, d)]
    return steps


def _kv_steps(read, pos_ref, gain_ref, invf_ref, wk_ref, wv_ref, k_refs, v_refs, cos_ref, sin_ref, slab_ref, xn_ref):
    st = {}

    def start():
        xn_ref[...] = _rms_norm(read(), gain_ref[...]).astype(_BF16)
        ang = pos_ref[...] * invf_ref[...]
        cos_ref[...] = jnp.cos(ang)
        sin_ref[...] = jnp.sin(ang)
        st["tables"] = _rope_tables(cos_ref[...], sin_ref[...])

    return ([start] + _projection_steps(st, xn_ref, wk_ref, slab_ref, k_refs, rope=True)
            + _projection_steps(st, xn_ref, wv_ref, slab_ref, v_refs))


def _q_steps(read, cos_ref, sin_ref, gain_ref, wq_ref, q_refs, slab_ref, xn_ref):
    st = {}

    def start():
        xn_ref[...] = _rms_norm(read(), gain_ref[...]).astype(_BF16)
        st["tables"] = _rope_tables(cos_ref[...], sin_ref[...])

    return [start] + _projection_steps(st, xn_ref, wq_ref, slab_ref, q_refs, rope=True, scale=Q_SCALE)


def _merge_steps(read, write, gpost_ref, wo_ref, o_refs, l_refs, nat_ref):
    st = {"o": []}
    permuted = ((o_refs[1], ATTN_DILATIONS[1]), (l_refs[1], ATTN_DILATIONS[1]),
                (o_refs[2], ATTN_DILATIONS[2]), (l_refs[2], ATTN_DILATIONS[2]))

    def lane_tile(p):
        def run():
            cols = slice(p * LANES, (p + 1) * LANES)
            for a, (ref, d) in enumerate(permuted):
                n = nat_ref.shape[2] // d
                for r in range(d):
                    nat_ref[a, p, pl.ds(r, n, stride=d), :] = ref[0, r, :, cols]
            o0, l0 = o_refs[0][0, 0, :, cols], l_refs[0][0, 0, :, cols]
            o1, l1, o2, l2 = (nat_ref[a, p] for a in range(4))
            top = jnp.maximum(jnp.maximum(l0, l1), l2)
            e0, e1, e2 = jnp.exp2(l0 - top), jnp.exp2(l1 - top), jnp.exp2(l2 - top)
            st["o"].append(((e0 * o0 + e1 * o1 + e2 * o2) / (e0 + e1 + e2)).astype(_BF16))
        return run

    def end():
        mix = _dot(jnp.concatenate(st["o"], axis=-1), wo_ref[...])
        write(read() + _rms_norm(mix, gpost_ref[...]))

    return [lane_tile(p) for p in range(PAIRS_PER_GROUP)] + [end]


def _run(steps):
    for step in steps:
        step()


def _two_stage(n_tiles, first, second, init=None):
    i = pl.program_id(0)
    if init is not None:
        pl.when(i == 0)(init)

    @pl.when(i > 0)
    def _():
        _run(second())

    @pl.when(i < n_tiles)
    def _():
        _run(first())


def _cur_tile(n_tiles):
    return lambda i: jnp.minimum(i, n_tiles - 1)


def _prev_tile():
    return lambda i: jnp.maximum(i - 1, 0)


def _row_spec(tile_of):
    return pl.BlockSpec((ROW_TILE, D_MODEL), lambda i: (tile_of(i), 0))


def _residue_major_specs(batch, seq, dtype, tile_of):
    tiles_per_seq = seq // ROW_TILE
    specs, shapes = [], []
    for d in ATTN_DILATIONS:
        specs.append(pl.BlockSpec((1, d, ROW_TILE // d, GROUP_WIDTH),
                                  lambda i: (tile_of(i) // tiles_per_seq, 0, tile_of(i) % tiles_per_seq, 0)))
        shapes.append(jax.ShapeDtypeStruct((batch, d, seq // d, GROUP_WIDTH), dtype))
    return specs, shapes


def _ffn_weight_specs():
    return [_const_spec((1, D_MODEL)), _const_spec((1, D_MODEL)),
            _const_spec((D_MODEL, D_FF)), _const_spec((D_MODEL, D_FF)), _const_spec((D_FF, D_MODEL))]


_HANDOVER = pltpu.VMEM((ROW_TILE, D_MODEL), _F32)
_SLAB = pltpu.VMEM((QKV_TILES, ROW_TILE, LANES), _F32)
_XN_BUF = pltpu.VMEM((ROW_TILE, D_MODEL), _BF16)
_FFN_WORK = [_XN_BUF, pltpu.VMEM((ROW_TILE, D_MODEL), _F32)]
BF16_SUBLANES = 16


def _hand_over(hand_ref):
    def put(y):
        hand_ref[...] = y

    def get():
        return hand_ref[...]

    return put, get


def _split(refs, *counts):
    out, at = [], 0
    for n in counts:
        out.append(refs[at:at + n])
        at += n
    assert at == len(refs)
    return out


def _cast_specs(weights, n_tiles):
    in_specs, out_specs, out_shapes = [], [], []
    for w, lead, _ in weights:
        rows, cols = w.shape[len(lead):]
        n_blocks = max(n for n in range(1, n_tiles + 1) if rows % (n * BF16_SUBLANES) == 0)

        def block(i, n_blocks=n_blocks):
            return (jnp.minimum(i, n_blocks - 1), 0)

        in_specs.append(pl.BlockSpec((None,) * len(lead) + (rows // n_blocks, cols),
                                     lambda i, lead=lead, block=block: lead + block(i)))
        out_specs.append(pl.BlockSpec((rows // n_blocks, cols), block))
        out_shapes.append(jax.ShapeDtypeStruct((rows, cols), _BF16))
    return in_specs, out_specs, out_shapes


def _cast_blocks(src_refs, dst_refs, interleave):
    for src, dst, flag in zip(src_refs, dst_refs, interleave):
        if flag:
            for j in range(src.shape[1] // LANES):
                cols = slice(j * LANES, (j + 1) * LANES)
                dst[:, cols] = _pair_interleaved(src[:, cols]).astype(_BF16)
        else:
            dst[...] = src[...].astype(_BF16)


def _ffn_pool_kernel(n_tiles, tiles_per_seq, interleave, h_ref, *refs):
    n_cast = len(interleave)
    ffn_w, pool_w, cast_src, (o_ref,), cast_dst, (hand_ref, ubuf_ref, weff_ref, *work) = _split(
        refs, 5, 6, n_cast, 1, n_cast, 5)
    gpre_ref, gpost_ref, win_ref, wgrp_ref, scale_ref, wout_ref = pool_w
    put, get = _hand_over(hand_ref)
    j = (pl.program_id(0) - 1) % tiles_per_seq

    def write(y):
        o_ref[...] = y

    def init():
        ubuf_ref[0:POOL_HALO, :] = jnp.zeros((POOL_HALO, D_MODEL), _F32)
        _pool_out_weight(wgrp_ref, scale_ref, wout_ref, weff_ref)

    _cast_blocks(cast_src, cast_dst, interleave)
    _two_stage(n_tiles,
               lambda: _ffn_steps(lambda: h_ref[...], put, *ffn_w, *work),
               lambda: _pool_steps(get, write, j, gpre_ref, gpost_ref, win_ref, weff_ref, ubuf_ref),
               init=init)


def _ffn_pool(h, ffn_w, pool_w, seq, to_cast):
    m = h.shape[0]
    n_tiles = m // ROW_TILE
    n_grp = len(POOL_WINDOWS)
    cast_in, cast_out, cast_shapes = _cast_specs(to_cast, n_tiles)
    outs = pl.pallas_call(
        functools.partial(_ffn_pool_kernel, n_tiles, seq // ROW_TILE, tuple(f for _, _, f in to_cast)),
        grid=(n_tiles + 1,),
        in_specs=[_row_spec(_cur_tile(n_tiles))] + _ffn_weight_specs() + [
            _const_spec((1, D_MODEL)), _const_spec((1, D_MODEL)), _const_spec((D_MODEL, D_MODEL)),
            _const_spec((n_grp, POOL_GROUP_DIM, POOL_GROUP_DIM)), _const_spec((1, D_MODEL)),
            _const_spec((D_MODEL, D_MODEL))] + cast_in,
        out_specs=[_row_spec(_prev_tile())] + cast_out,
        out_shape=[jax.ShapeDtypeStruct((m, D_MODEL), _F32)] + cast_shapes,
        scratch_shapes=[_HANDOVER, pltpu.VMEM((POOL_HALO + ROW_TILE, D_MODEL), _F32),
                        pltpu.VMEM((D_MODEL, D_MODEL), _BF16)] + _FFN_WORK,
        compiler_params=_params(("arbitrary",)),
        name="ffn_pool",
    )(h, *ffn_w, *pool_w, *(w for w, _, _ in to_cast))
    return outs[0], outs[1:]


def _ffn_and_keep(h_ref, o_ref, put, ffn_w, work):
    def write(y):
        o_ref[...] = y
        put(y)

    return lambda: _ffn_steps(lambda: h_ref[...], write, *ffn_w, *work)


def _table_spec():
    return pl.BlockSpec((ROW_TILE, LANES), lambda i: (jnp.maximum(i - 1, 0), 0))


def _ffn_kv_kernel(n_tiles, interleave, h_ref, pos_ref, *refs):
    n_cast = len(interleave)
    (ffn_w, kv_w, cast_src, (o_ref,), k_refs, v_refs, (cos_ref, sin_ref), cast_dst,
     (hand_ref, slab_ref, xn2_ref, *work)) = _split(refs, 5, 4, n_cast, 1, N_GROUPS, N_GROUPS, 2, n_cast, 5)
    put, get = _hand_over(hand_ref)
    _cast_blocks(cast_src, cast_dst, interleave)
    _two_stage(n_tiles, _ffn_and_keep(h_ref, o_ref, put, ffn_w, work),
               lambda: _kv_steps(get, pos_ref, *kv_w, k_refs, v_refs, cos_ref, sin_ref, slab_ref, xn2_ref))


def _ffn_kv(h, pos, ffn_w, kv_w, batch, seq, to_cast):
    m = h.shape[0]
    n_tiles = m // ROW_TILE
    specs, shapes = _residue_major_specs(batch, seq, _BF16, _prev_tile())
    cast_in, cast_out, cast_shapes = _cast_specs(to_cast, n_tiles)
    table_shape = jax.ShapeDtypeStruct((m, LANES), _F32)
    outs = pl.pallas_call(
        functools.partial(_ffn_kv_kernel, n_tiles, tuple(f for _, _, f in to_cast)),
        grid=(n_tiles + 1,),
        in_specs=[_row_spec(_cur_tile(n_tiles)), pl.BlockSpec((ROW_TILE, 1), lambda i: (jnp.maximum(i - 1, 0), 0))]
        + _ffn_weight_specs() + [_const_spec((1, D_MODEL)), _const_spec((1, LANES)),
                                 _const_spec((D_MODEL, QKV_WIDTH)), _const_spec((D_MODEL, QKV_WIDTH))] + cast_in,
        out_specs=[_row_spec(_cur_tile(n_tiles))] + specs * 2 + [_table_spec()] * 2 + cast_out,
        out_shape=[jax.ShapeDtypeStruct((m, D_MODEL), _F32)] + shapes * 2 + [table_shape] * 2 + cast_shapes,
        scratch_shapes=[_HANDOVER, _SLAB, _XN_BUF] + _FFN_WORK,
        compiler_params=_params(("arbitrary",)),
        name="ffn_kv",
    )(h, pos, *ffn_w, *kv_w, *(w for w, _, _ in to_cast))
    return outs[0], outs[1:4], outs[4:7], outs[7:9], outs[9:]


def _ffn_q_kernel(n_tiles, interleave, h_ref, cos_ref, sin_ref, *refs):
    n_cast = len(interleave)
    ffn_w, q_w, cast_src, (o_ref,), q_refs, cast_dst, (hand_ref, slab_ref, xn2_ref, *work) = _split(
        refs, 5, 2, n_cast, 1, N_GROUPS, n_cast, 5)
    put, get = _hand_over(hand_ref)
    _cast_blocks(cast_src, cast_dst, interleave)
    _two_stage(n_tiles, _ffn_and_keep(h_ref, o_ref, put, ffn_w, work),
               lambda: _q_steps(get, cos_ref, sin_ref, *q_w, q_refs, slab_ref, xn2_ref))


def _ffn_q(h, tables, ffn_w, q_w, batch, seq, to_cast):
    m = h.shape[0]
    n_tiles = m // ROW_TILE
    specs, shapes = _residue_major_specs(batch, seq, _BF16, _prev_tile())
    cast_in, cast_out, cast_shapes = _cast_specs(to_cast, n_tiles)
    outs = pl.pallas_call(
        functools.partial(_ffn_q_kernel, n_tiles, tuple(f for _, _, f in to_cast)),
        grid=(n_tiles + 1,),
        in_specs=[_row_spec(_cur_tile(n_tiles))] + [_table_spec()] * 2 + _ffn_weight_specs()
        + [_const_spec((1, D_MODEL)), _const_spec((D_MODEL, QKV_WIDTH))] + cast_in,
        out_specs=[_row_spec(_cur_tile(n_tiles))] + specs + cast_out,
        out_shape=[jax.ShapeDtypeStruct((m, D_MODEL), _F32)] + shapes + cast_shapes,
        scratch_shapes=[_HANDOVER, _SLAB, _XN_BUF] + _FFN_WORK,
        compiler_params=_params(("arbitrary",)),
        name="ffn_q",
    )(h, *tables, *ffn_w, *q_w, *(w for w, _, _ in to_cast))
    return outs[0], outs[1:4], outs[4:]


def _merge_ffn_kernel(n_tiles, h_ref, *refs):
    merge_w, o_refs, l_refs, ffn_w, (out_ref, hand_ref, nat_ref, *work) = (
        refs[:2], refs[2:5], refs[5:8], refs[8:13], refs[13:])
    put, get = _hand_over(hand_ref)

    def write(y):
        out_ref[...] = y

    _two_stage(n_tiles,
               lambda: _merge_steps(lambda: h_ref[...], put, *merge_w, o_refs, l_refs, nat_ref),
               lambda: _ffn_steps(get, write, *ffn_w, *work))


def _merge_ffn(h, merge_w, outs, lses, ffn_w, batch, seq):
    m = h.shape[0]
    n_tiles = m // ROW_TILE
    specs, _ = _residue_major_specs(batch, seq, _F32, _cur_tile(n_tiles))
    return pl.pallas_call(
        functools.partial(_merge_ffn_kernel, n_tiles),
        grid=(n_tiles + 1,),
        in_specs=[_row_spec(_cur_tile(n_tiles)), _const_spec((1, D_MODEL)), _const_spec((GROUP_WIDTH, D_MODEL))]
        + specs * 2 + _ffn_weight_specs(),
        out_specs=_row_spec(_prev_tile()),
        out_shape=jax.ShapeDtypeStruct((m, D_MODEL), _F32),
        scratch_shapes=[_HANDOVER, pltpu.VMEM((4, PAIRS_PER_GROUP, ROW_TILE, LANES), _F32)] + _FFN_WORK,
        compiler_params=_params(("arbitrary",)),
        name="merge_ffn",
    )(h, *merge_w, *outs, *lses, *ffn_w)


def _attn_kernel(q_ref, kp_ref, kc_ref, vp_ref, vc_ref, o_ref, lse_ref):
    step = pl.program_id(1)
    n_blk = q_ref.shape[1] // ATTN_BLK
    win = 2 * ATTN_BLK
    qi = lax.broadcasted_iota(jnp.int32, (win, win), 0) % ATTN_BLK
    kj = lax.broadcasted_iota(jnp.int32, (win, win), 1)
    band = jnp.where((kj >= qi) & (kj <= qi + ATTN_BLK), 0.0, NEG_BIG)
    first_lo = jnp.maximum(qi, jnp.where(step == 0, ATTN_BLK, 0))
    band_first = jnp.where((kj >= first_lo) & (kj <= qi + ATTN_BLK), 0.0, NEG_BIG)
    q_lane = lax.broadcasted_iota(jnp.int32, (ATTN_BLK, LANES), 1)
    q_is_a = (q_lane % HEAD_DIM) < HEAD_DIM // 2
    v_lane = lax.broadcasted_iota(jnp.int32, (win, LANES), 1)
    v_is_a = v_lane < HEAD_DIM
    ones_a = jnp.where(v_is_a, 1.0, 0.0).astype(_BF16)
    ones_b = jnp.where(v_is_a, 0.0, 1.0).astype(_BF16)
    o_is_a = q_lane < HEAD_DIM
    nt = (((1,), (1,)), ((), ()))
    work = [(sq, j, p) for sq in range(q_ref.shape[0]) for j in range(n_blk) for p in range(PAIRS_PER_GROUP)]

    def window(prev_ref, cur_ref, sq, j, p):
        pair = slice(p * LANES, (p + 1) * LANES)
        if j == 0:
            return jnp.concatenate([prev_ref[sq, :, pair], cur_ref[sq, 0:ATTN_BLK, pair]], axis=0)
        return cur_ref[sq, (j - 1) * ATTN_BLK:(j + 1) * ATTN_BLK, pair]

    def scores(sq, j, p):
        q = q_ref[sq, j * ATTN_BLK:(j + 1) * ATTN_BLK, p * LANES:(p + 1) * LANES]
        zero = jnp.zeros_like(q)
        q2 = jnp.concatenate([jnp.where(q_is_a, q, zero), jnp.where(q_is_a, zero, q)], axis=0)
        s = lax.dot_general(q2, window(kp_ref, kc_ref, sq, j, p), nt, preferred_element_type=_F32)
        return s + (band_first if j == 0 else band)

    def finish(sq, j, p, s):
        m = jnp.max(s, axis=-1, keepdims=True)
        prob = jnp.exp2(s - m).astype(_BF16)
        prob = jnp.concatenate([prob[:ATTN_BLK], prob[ATTN_BLK:]], axis=1)
        v = window(vp_ref, vc_ref, sq, j, p)
        zero = jnp.zeros_like(v)
        v2 = jnp.concatenate([jnp.concatenate([jnp.where(v_is_a, v, zero), ones_a], axis=1),
                              jnp.concatenate([jnp.where(v_is_a, zero, v), ones_b], axis=1)], axis=0)
        r = _dot(prob, v2)
        den = r[:, LANES:]
        rows = slice(j * ATTN_BLK, (j + 1) * ATTN_BLK)
        cols = slice(p * LANES, (p + 1) * LANES)
        o_ref[sq, rows, cols] = r[:, :LANES] * (1.0 / den)
        lse_ref[sq, rows, cols] = jnp.where(o_is_a, m[:ATTN_BLK], m[ATTN_BLK:]) + jnp.log2(den)

    ahead = 1
    pending = [scores(*work[n]) for n in range(ahead)]
    for n, item in enumerate(work):
        s = pending.pop(0)
        if n + ahead < len(work):
            pending.append(scores(*work[n + ahead]))
        finish(*item, s)


def _attn_group(q, k, v):
    batch, d, rows, _ = q.shape
    n_seq = batch * d
    q, k, v = (a.reshape(n_seq, rows, GROUP_WIDTH) for a in (q, k, v))
    q_tile = min(ATTN_Q_TILE, rows)
    seqs = ATTN_Q_TILE // q_tile
    blocks_per_step = q_tile // ATTN_BLK

    def cur(s, i):
        return (s, i, 0)

    def prev(s, i):
        return (s, jnp.maximum(i * blocks_per_step - 1, 0), 0)

    cur_spec = pl.BlockSpec((seqs, q_tile, GROUP_WIDTH), cur)
    prev_spec = pl.BlockSpec((seqs, ATTN_BLK, GROUP_WIDTH), prev)
    out_shape = jax.ShapeDtypeStruct((n_seq, rows, GROUP_WIDTH), _F32)
    o, lse = pl.pallas_call(
        _attn_kernel,
        grid=(n_seq // seqs, rows // q_tile),
        in_specs=[cur_spec, prev_spec, cur_spec, prev_spec, cur_spec],
        out_specs=[cur_spec, cur_spec],
        out_shape=[out_shape, out_shape],
        compiler_params=_params(("parallel", "arbitrary")),
        name=f"window_attn_d{d}",
    )(q, k, k, v, v)
    shape = (batch, d, rows, GROUP_WIDTH)
    return o.reshape(shape), lse.reshape(shape)


def kernel(x, positions, norm_gain, ffn_w_gate, ffn_w_up, ffn_w_down, pool_w_in, pool_w_group, pool_scale,
           pool_w_out, kv_norm_gain, w_k, w_v, attn_w_q, attn_w_o):
    batch, seq, _ = x.shape
    m = batch * seq
    gains = norm_gain.reshape(2, 6, 1, D_MODEL)
    bf = lambda w: w.astype(_BF16)

    def ffn_f32(layer, slot):
        return tuple((w, (layer, slot), False) for w in (ffn_w_gate, ffn_w_up, ffn_w_down))

    def ffn_gains(layer, slot):
        return (gains[layer, 4 * slot], 0.5 * gains[layer, 4 * slot + 1])

    pos = positions.astype(_F32).reshape(m, 1)
    inv_freq = ROPE_THETA ** (-jnp.arange(0, HEAD_DIM, 2, dtype=_F32) / HEAD_DIM)
    invf = jnp.tile(inv_freq, LANES // (HEAD_DIM // 2)).reshape(1, LANES)

    h = x.reshape(m, D_MODEL)
    pool_w = (gains[0, 2], gains[0, 3], bf(pool_w_in[0]), pool_w_group[0],
              pool_scale[0].reshape(1, D_MODEL), pool_w_out[0])
    ffn00 = tuple(bf(w[lead]) for w, lead, _ in ffn_f32(0, 0))
    h, (*ffn01, wk_bf, wv_bf) = _ffn_pool(h, ffn_gains(0, 0) + ffn00, pool_w, seq,
                                          to_cast=ffn_f32(0, 1) + ((w_k, (), True), (w_v, (), False)))
    kv_w = (kv_norm_gain.reshape(1, D_MODEL), invf, wk_bf, wv_bf)
    h, k_sh, v_sh, tables, (*ffn10, wq_bf) = _ffn_kv(h, pos, ffn_gains(0, 1) + tuple(ffn01), kv_w, batch, seq,
                                                     to_cast=ffn_f32(1, 0) + ((attn_w_q, (0,), True),))
    h, q, (*ffn11, wo_bf) = _ffn_q(h, tables, ffn_gains(1, 0) + tuple(ffn10), (gains[1, 2], wq_bf), batch, seq,
                                   to_cast=ffn_f32(1, 1) + ((attn_w_o, (0,), False),))
    outs, lses = [], []
    for g in range(N_GROUPS):
        o, lse = _attn_group(q[g], k_sh[g], v_sh[g])
        outs.append(o)
        lses.append(lse)
    h = _merge_ffn(h, (gains[1, 3], wo_bf), outs, lses, ffn_gains(1, 1) + tuple(ffn11), batch, seq)
    return h.reshape(batch, seq, D_MODEL)
```

```python
import functools
import math

import jax
import jax.numpy as jnp
from jax import lax
from jax.experimental import pallas as pl
from jax.experimental.pallas import tpu as pltpu

D_MODEL = 1024
D_FF = 2816
POOL_WINDOWS = (2, 4, 8, 16)
POOL_GROUP_DIM = D_MODEL // len(POOL_WINDOWS)
POOL_HALO = 16
ATTN_DILATIONS = (1, 4, 16)
N_GROUPS = len(ATTN_DILATIONS)
ATTN_BLK = 128
HEAD_DIM = 64
HEADS_PER_GROUP = 8
GROUP_WIDTH = HEADS_PER_GROUP * HEAD_DIM
QKV_WIDTH = N_GROUPS * GROUP_WIDTH
ROPE_THETA = 10000.0
Q_SCALE = HEAD_DIM ** -0.5 * math.log2(math.e)
RMS_EPS = 1e-6
NEG_BIG = -1e30

LANES = 128
PAIRS_PER_GROUP = GROUP_WIDTH // LANES
QKV_TILES = QKV_WIDTH // LANES
ROW_TILE = 512
ATTN_Q_TILE = 2048
FF_CHUNK = 256
VMEM_LIMIT_BYTES = 56 * 1024 * 1024

_F32 = jnp.float32
_BF16 = jnp.bfloat16


def _rms_norm(x, gain=None):
    y = x * lax.rsqrt(jnp.mean(x * x, axis=-1, keepdims=True) + RMS_EPS)
    return y if gain is None else y * gain


def _dot(a, b):
    return jnp.dot(a, b, preferred_element_type=_F32)


def _const_spec(shape):
    return pl.BlockSpec(shape, lambda *_: (0,) * len(shape), pipeline_mode=pl.Buffered(1))


def _params(semantics):
    return pltpu.CompilerParams(dimension_semantics=semantics, vmem_limit_bytes=VMEM_LIMIT_BYTES)


def _ffn_steps(read, write, gpost_ref, wg_ref, wu_ref, wd_ref, xn_ref, acc_ref):
    def start():
        xn_ref[...] = _rms_norm(read()).astype(_BF16)

    def chunk(c):
        def run():
            hi = min(c + FF_CHUNK, D_FF)
            g = _dot(xn_ref[...], wg_ref[:, c:hi])
            u = _dot(xn_ref[...], wu_ref[:, c:hi])
            a = (g * jax.nn.sigmoid(g) * u).astype(_BF16)
            down = _dot(a, wd_ref[c:hi, :])
            if c == 0:
                acc_ref[...] = down
            else:
                acc_ref[...] += down
        return run

    def end():
        write(read() + _rms_norm(acc_ref[...], gpost_ref[...]))

    return [start] + [chunk(c) for c in range(0, D_FF, FF_CHUNK)] + [end]


def _pool_out_weight(wgrp_ref, scale_ref, wout_ref, weff_ref):
    for g in range(len(POOL_WINDOWS)):
        rows = slice(g * POOL_GROUP_DIM, (g + 1) * POOL_GROUP_DIM)
        weff_ref[rows, :] = jnp.dot(wgrp_ref[g] * scale_ref[:, rows], wout_ref[rows, :], preferred_element_type=_F32,
                                    precision=lax.Precision.HIGHEST).astype(_BF16)


def _pool_steps(read, write, j, gpost_ref, win_ref, weff_ref, ubuf_ref):
    st = {"p": []}

    def start():
        st["x"] = read()
        hm = _rms_norm(st["x"]).astype(_BF16)
        st["u"] = _dot(hm, win_ref[...])
        ubuf_ref[0:POOL_HALO, :] = jnp.where(j == 0, 0.0, ubuf_ref[0:POOL_HALO, :])
        ubuf_ref[POOL_HALO:, :] = st["u"]

    def group(g, w):
        def run():
            tm = st["x"].shape[0]
            cols = slice(g * POOL_GROUP_DIM, (g + 1) * POOL_GROUP_DIM)
            t = j * tm + lax.broadcasted_iota(jnp.int32, (tm, 1), 0)
            s = ubuf_ref[:, cols]
            k = 1
            while k < w:
                s = s + pltpu.roll(s, k, axis=0)
                k *= 2
            count = jnp.minimum(t + 1, w).astype(_F32)
            p = s[POOL_HALO:] / count - st["u"][:, cols]
            st["p"].append(p.astype(_BF16))
        return run

    def end():
        tm = st["x"].shape[0]
        mix = _dot(jnp.concatenate(st["p"], axis=-1), weff_ref[...])
        ubuf_ref[0:POOL_HALO, :] = st["u"][tm - POOL_HALO:, :]
        write(st["x"] + _rms_norm(mix, gpost_ref[...]))

    return [start] + [group(g, w) for g, w in enumerate(POOL_WINDOWS)] + [end]


def _rope_tables(c, s):
    lane = lax.broadcasted_iota(jnp.int32, c.shape, 1)
    return c, jnp.where(lane < LANES // 2, -s, s)


def _rope_tile(xt, tables):
    c, s_signed = tables
    return xt * c + pltpu.roll(xt, LANES // 2, axis=1) * s_signed


def _pair_interleaved(t):
    half = HEAD_DIM // 2
    block = lax.broadcasted_iota(jnp.int32, t.shape, 1) // half
    return jnp.where(block == 1, pltpu.roll(t, LANES - half, axis=1),
                     jnp.where(block == 2, pltpu.roll(t, half, axis=1), t))


def _projection_steps(st, xn_ref, w_ref, slab_ref, out_refs, rope=False, scale=None):
    def project(g):
        def run():
            y = _dot(xn_ref[...], w_ref[:, g * GROUP_WIDTH:(g + 1) * GROUP_WIDTH])
            for p in range(PAIRS_PER_GROUP):
                yt = y[:, p * LANES:(p + 1) * LANES]
                if rope:
                    yt = _rope_tile(yt, st["tables"])
                if scale is not None:
                    yt = yt * scale
                slab_ref[g * PAIRS_PER_GROUP + p] = yt
        return run

    def scatter(g, d):
        def run():
            n = slab_ref.shape[1] // d
            for r in range(d):
                for p in range(PAIRS_PER_GROUP):
                    rows = slab_ref[g * PAIRS_PER_GROUP + p, pl.ds(r, n, stride=d), :]
                    out_refs[g][0, r, :, p * LANES:(p + 1) * LANES] = rows.astype(_BF16)
        return run

    steps = []
    for g, d in enumerate(ATTN_DILATIONS):
        steps += [project(g), scatter(g, d)]
    return steps


def _kv_steps(read, pos_ref, invf_ref, wk_ref, wv_ref, k_refs, v_refs, cos_ref, sin_ref, slab_ref, xn_ref):
    st = {}

    def start():
        xn_ref[...] = _rms_norm(read()).astype(_BF16)
        ang = pos_ref[...] * invf_ref[...]
        cos_ref[...] = jnp.cos(ang)
        sin_ref[...] = jnp.sin(ang)
        st["tables"] = _rope_tables(cos_ref[...], sin_ref[...])

    return ([start] + _projection_steps(st, xn_ref, wk_ref, slab_ref, k_refs, rope=True)
            + _projection_steps(st, xn_ref, wv_ref, slab_ref, v_refs))


def _q_steps(read, cos_ref, sin_ref, wq_ref, q_refs, slab_ref, xn_ref):
    st = {}

    def start():
        xn_ref[...] = _rms_norm(read()).astype(_BF16)
        st["tables"] = _rope_tables(cos_ref[...], sin_ref[...])

    return [start] + _projection_steps(st, xn_ref, wq_ref, slab_ref, q_refs, rope=True, scale=Q_SCALE)


def _merge_steps(read, write, gpost_ref, wo_ref, o_refs, l_refs, nat_ref):
    st = {"o": []}
    permuted = ((o_refs[1], ATTN_DILATIONS[1]), (l_refs[1], ATTN_DILATIONS[1]),
                (o_refs[2], ATTN_DILATIONS[2]), (l_refs[2], ATTN_DILATIONS[2]))

    def lane_tile(p):
        def run():
            cols = slice(p * LANES, (p + 1) * LANES)
            for a, (ref, d) in enumerate(permuted):
                n = nat_ref.shape[2] // d
                for r in range(d):
                    nat_ref[a, p, pl.ds(r, n, stride=d), :] = ref[0, r, :, cols]
            o0, l0 = o_refs[0][0, 0, :, cols], l_refs[0][0, 0, :, cols]
            o1, l1, o2, l2 = (nat_ref[a, p] for a in range(4))
            top = jnp.maximum(jnp.maximum(l0, l1), l2)
            e0, e1, e2 = jnp.exp2(l0 - top), jnp.exp2(l1 - top), jnp.exp2(l2 - top)
            st["o"].append(((e0 * o0 + e1 * o1 + e2 * o2) / (e0 + e1 + e2)).astype(_BF16))
        return run

    def end():
        mix = _dot(jnp.concatenate(st["o"], axis=-1), wo_ref[...])
        write(read() + _rms_norm(mix, gpost_ref[...]))

    return [lane_tile(p) for p in range(PAIRS_PER_GROUP)] + [end]


def _run(steps):
    for step in steps:
        step()


def _two_stage(n_tiles, first, second, init=None):
    i = pl.program_id(0)
    if init is not None:
        pl.when(i == 0)(init)

    @pl.when(i > 0)
    def _():
        _run(second())

    @pl.when(i < n_tiles)
    def _():
        _run(first())


def _cur_tile(n_tiles):
    return lambda i: jnp.minimum(i, n_tiles - 1)


def _prev_tile():
    return lambda i: jnp.maximum(i - 1, 0)


def _row_spec(tile_of):
    return pl.BlockSpec((ROW_TILE, D_MODEL), lambda i: (tile_of(i), 0))


def _residue_major_specs(batch, seq, dtype, tile_of):
    tiles_per_seq = seq // ROW_TILE
    specs, shapes = [], []
    for d in ATTN_DILATIONS:
        specs.append(pl.BlockSpec((1, d, ROW_TILE // d, GROUP_WIDTH),
                                  lambda i: (tile_of(i) // tiles_per_seq, 0, tile_of(i) % tiles_per_seq, 0)))
        shapes.append(jax.ShapeDtypeStruct((batch, d, seq // d, GROUP_WIDTH), dtype))
    return specs, shapes


def _ffn_weight_specs():
    return [_const_spec((1, D_MODEL)),
            _const_spec((D_MODEL, D_FF)), _const_spec((D_MODEL, D_FF)), _const_spec((D_FF, D_MODEL))]


_HANDOVER = pltpu.VMEM((ROW_TILE, D_MODEL), _F32)
_SLAB = pltpu.VMEM((QKV_TILES, ROW_TILE, LANES), _F32)
_XN_BUF = pltpu.VMEM((ROW_TILE, D_MODEL), _BF16)
_FFN_WORK = [_XN_BUF, pltpu.VMEM((ROW_TILE, D_MODEL), _F32)]
BF16_SUBLANES = 16


def _hand_over(hand_ref):
    def put(y):
        hand_ref[...] = y

    def get():
        return hand_ref[...]

    return put, get


def _split(refs, *counts):
    out, at = [], 0
    for n in counts:
        out.append(refs[at:at + n])
        at += n
    assert at == len(refs)
    return out


def _cast_specs(weights, n_tiles):
    in_specs, out_specs, out_shapes = [], [], []
    for w, lead, _, _ in weights:
        rows, cols = w.shape[len(lead):]
        n_blocks = max(n for n in range(1, n_tiles + 1) if rows % (n * BF16_SUBLANES) == 0)

        def block(i, n_blocks=n_blocks):
            return (jnp.minimum(i, n_blocks - 1), 0)

        in_specs.append(pl.BlockSpec((None,) * len(lead) + (rows // n_blocks, cols),
                                     lambda i, lead=lead, block=block: lead + block(i)))
        out_specs.append(pl.BlockSpec((rows // n_blocks, cols), block))
        out_shapes.append(jax.ShapeDtypeStruct((rows, cols), _BF16))
    gains = [jnp.broadcast_to(g.reshape(-1, 1), (g.shape[0], LANES)) for _, _, _, g in weights if g is not None]
    in_specs += [_const_spec(g.shape) for g in gains]
    plan = tuple((interleave, g is not None) for _, _, interleave, g in weights)
    operands = [w for w, _, _, _ in weights] + gains
    return plan, operands, in_specs, out_specs, out_shapes


def _cast_blocks(plan, src_refs, gain_refs, dst_refs):
    gain_refs = list(gain_refs)
    for (interleave, has_gain), src, dst in zip(plan, src_refs, dst_refs):
        if not (interleave or has_gain):
            dst[...] = src[...].astype(_BF16)
            continue
        if has_gain:
            gain_ref = gain_refs.pop(0)
            rows = src.shape[0]
            block = jnp.minimum(pl.program_id(0), gain_ref.shape[0] // rows - 1)
            gain = gain_ref[pl.ds(pl.multiple_of(block * rows, rows), rows), :]
        for j in range(src.shape[1] // LANES):
            cols = slice(j * LANES, (j + 1) * LANES)
            tile = src[:, cols] * gain if has_gain else src[:, cols]
            dst[:, cols] = (_pair_interleaved(tile) if interleave else tile).astype(_BF16)


def _ffn_pool_kernel(n_tiles, tiles_per_seq, plan, h_ref, *refs):
    n_cast, n_gain = len(plan), sum(has_gain for _, has_gain in plan)
    ffn_w, pool_w, cast_src, cast_gain, (o_ref,), cast_dst, (hand_ref, ubuf_ref, weff_ref, *work) = _split(
        refs, 4, 5, n_cast, n_gain, 1, n_cast, 5)
    gpost_ref, win_ref, wgrp_ref, scale_ref, wout_ref = pool_w
    put, get = _hand_over(hand_ref)
    j = (pl.program_id(0) - 1) % tiles_per_seq

    def write(y):
        o_ref[...] = y

    def init():
        ubuf_ref[0:POOL_HALO, :] = jnp.zeros((POOL_HALO, D_MODEL), _F32)
        _pool_out_weight(wgrp_ref, scale_ref, wout_ref, weff_ref)

    _cast_blocks(plan, cast_src, cast_gain, cast_dst)
    _two_stage(n_tiles,
               lambda: _ffn_steps(lambda: h_ref[...], put, *ffn_w, *work),
               lambda: _pool_steps(get, write, j, gpost_ref, win_ref, weff_ref, ubuf_ref),
               init=init)


def _ffn_pool(h, ffn_w, pool_w, seq, to_cast):
    m = h.shape[0]
    n_tiles = m // ROW_TILE
    n_grp = len(POOL_WINDOWS)
    plan, cast_operands, cast_in, cast_out, cast_shapes = _cast_specs(to_cast, n_tiles)
    outs = pl.pallas_call(
        functools.partial(_ffn_pool_kernel, n_tiles, seq // ROW_TILE, plan),
        grid=(n_tiles + 1,),
        in_specs=[_row_spec(_cur_tile(n_tiles))] + _ffn_weight_specs() + [
            _const_spec((1, D_MODEL)), _const_spec((D_MODEL, D_MODEL)),
            _const_spec((n_grp, POOL_GROUP_DIM, POOL_GROUP_DIM)), _const_spec((1, D_MODEL)),
            _const_spec((D_MODEL, D_MODEL))] + cast_in,
        out_specs=[_row_spec(_prev_tile())] + cast_out,
        out_shape=[jax.ShapeDtypeStruct((m, D_MODEL), _F32)] + cast_shapes,
        scratch_shapes=[_HANDOVER, pltpu.VMEM((POOL_HALO + ROW_TILE, D_MODEL), _F32),
                        pltpu.VMEM((D_MODEL, D_MODEL), _BF16)] + _FFN_WORK,
        compiler_params=_params(("arbitrary",)),
        name="ffn_pool",
    )(h, *ffn_w, *pool_w, *cast_operands)
    return outs[0], outs[1:]


def _ffn_and_keep(h_ref, o_ref, put, ffn_w, work):
    def write(y):
        o_ref[...] = y
        put(y)

    return lambda: _ffn_steps(lambda: h_ref[...], write, *ffn_w, *work)


def _table_spec():
    return pl.BlockSpec((ROW_TILE, LANES), lambda i: (jnp.maximum(i - 1, 0), 0))


def _ffn_kv_kernel(n_tiles, plan, h_ref, pos_ref, *refs):
    n_cast, n_gain = len(plan), sum(has_gain for _, has_gain in plan)
    (ffn_w, kv_w, cast_src, cast_gain, (o_ref,), k_refs, v_refs, (cos_ref, sin_ref), cast_dst,
     (hand_ref, slab_ref, xn2_ref, *work)) = _split(refs, 4, 3, n_cast, n_gain, 1, N_GROUPS, N_GROUPS, 2, n_cast, 5)
    put, get = _hand_over(hand_ref)
    _cast_blocks(plan, cast_src, cast_gain, cast_dst)
    _two_stage(n_tiles, _ffn_and_keep(h_ref, o_ref, put, ffn_w, work),
               lambda: _kv_steps(get, pos_ref, *kv_w, k_refs, v_refs, cos_ref, sin_ref, slab_ref, xn2_ref))


def _ffn_kv(h, pos, ffn_w, kv_w, batch, seq, to_cast):
    m = h.shape[0]
    n_tiles = m // ROW_TILE
    specs, shapes = _residue_major_specs(batch, seq, _BF16, _prev_tile())
    plan, cast_operands, cast_in, cast_out, cast_shapes = _cast_specs(to_cast, n_tiles)
    table_shape = jax.ShapeDtypeStruct((m, LANES), _F32)
    outs = pl.pallas_call(
        functools.partial(_ffn_kv_kernel, n_tiles, plan),
        grid=(n_tiles + 1,),
        in_specs=[_row_spec(_cur_tile(n_tiles)), pl.BlockSpec((ROW_TILE, 1), lambda i: (jnp.maximum(i - 1, 0), 0))]
        + _ffn_weight_specs() + [_const_spec((1, LANES)),
                                 _const_spec((D_MODEL, QKV_WIDTH)), _const_spec((D_MODEL, QKV_WIDTH))] + cast_in,
        out_specs=[_row_spec(_cur_tile(n_tiles))] + specs * 2 + [_table_spec()] * 2 + cast_out,
        out_shape=[jax.ShapeDtypeStruct((m, D_MODEL), _F32)] + shapes * 2 + [table_shape] * 2 + cast_shapes,
        scratch_shapes=[_HANDOVER, _SLAB, _XN_BUF] + _FFN_WORK,
        compiler_params=_params(("arbitrary",)),
        name="ffn_kv",
    )(h, pos, *ffn_w, *kv_w, *cast_operands)
    return outs[0], outs[1:4], outs[4:7], outs[7:9], outs[9:]


def _ffn_q_kernel(n_tiles, plan, h_ref, cos_ref, sin_ref, *refs):
    n_cast, n_gain = len(plan), sum(has_gain for _, has_gain in plan)
    ffn_w, q_w, cast_src, cast_gain, (o_ref,), q_refs, cast_dst, (hand_ref, slab_ref, xn2_ref, *work) = _split(
        refs, 4, 1, n_cast, n_gain, 1, N_GROUPS, n_cast, 5)
    put, get = _hand_over(hand_ref)
    _cast_blocks(plan, cast_src, cast_gain, cast_dst)
    _two_stage(n_tiles, _ffn_and_keep(h_ref, o_ref, put, ffn_w, work),
               lambda: _q_steps(get, cos_ref, sin_ref, *q_w, q_refs, slab_ref, xn2_ref))


def _ffn_q(h, tables, ffn_w, q_w, batch, seq, to_cast):
    m = h.shape[0]
    n_tiles = m // ROW_TILE
    specs, shapes = _residue_major_specs(batch, seq, _BF16, _prev_tile())
    plan, cast_operands, cast_in, cast_out, cast_shapes = _cast_specs(to_cast, n_tiles)
    outs = pl.pallas_call(
        functools.partial(_ffn_q_kernel, n_tiles, plan),
        grid=(n_tiles + 1,),
        in_specs=[_row_spec(_cur_tile(n_tiles))] + [_table_spec()] * 2 + _ffn_weight_specs()
        + [_const_spec((D_MODEL, QKV_WIDTH))] + cast_in,
        out_specs=[_row_spec(_cur_tile(n_tiles))] + specs + cast_out,
        out_shape=[jax.ShapeDtypeStruct((m, D_MODEL), _F32)] + shapes + cast_shapes,
        scratch_shapes=[_HANDOVER, _SLAB, _XN_BUF] + _FFN_WORK,
        compiler_params=_params(("arbitrary",)),
        name="ffn_q",
    )(h, *tables, *ffn_w, *q_w, *cast_operands)
    return outs[0], outs[1:4], outs[4:]


def _merge_ffn_kernel(n_tiles, h_ref, *refs):
    merge_w, o_refs, l_refs, ffn_w, (out_ref, hand_ref, nat_ref, *work) = (
        refs[:2], refs[2:5], refs[5:8], refs[8:12], refs[12:])
    put, get = _hand_over(hand_ref)

    def write(y):
        out_ref[...] = y

    _two_stage(n_tiles,
               lambda: _merge_steps(lambda: h_ref[...], put, *merge_w, o_refs, l_refs, nat_ref),
               lambda: _ffn_steps(get, write, *ffn_w, *work))


def _merge_ffn(h, merge_w, outs, lses, ffn_w, batch, seq):
    m = h.shape[0]
    n_tiles = m // ROW_TILE
    specs, _ = _residue_major_specs(batch, seq, _F32, _cur_tile(n_tiles))
    return pl.pallas_call(
        functools.partial(_merge_ffn_kernel, n_tiles),
        grid=(n_tiles + 1,),
        in_specs=[_row_spec(_cur_tile(n_tiles)), _const_spec((1, D_MODEL)), _const_spec((GROUP_WIDTH, D_MODEL))]
        + specs * 2 + _ffn_weight_specs(),
        out_specs=_row_spec(_prev_tile()),
        out_shape=jax.ShapeDtypeStruct((m, D_MODEL), _F32),
        scratch_shapes=[_HANDOVER, pltpu.VMEM((4, PAIRS_PER_GROUP, ROW_TILE, LANES), _F32)] + _FFN_WORK,
        compiler_params=_params(("arbitrary",)),
        name="merge_ffn",
    )(h, *merge_w, *outs, *lses, *ffn_w)


def _attn_kernel(q_ref, kp_ref, kc_ref, vp_ref, vc_ref, o_ref, lse_ref):
    step = pl.program_id(1)
    n_blk = q_ref.shape[1] // ATTN_BLK
    win = 2 * ATTN_BLK
    qi = lax.broadcasted_iota(jnp.int32, (win, win), 0) % ATTN_BLK
    kj = lax.broadcasted_iota(jnp.int32, (win, win), 1)
    band = jnp.where((kj >= qi) & (kj <= qi + ATTN_BLK), 0.0, NEG_BIG)
    first_lo = jnp.maximum(qi, jnp.where(step == 0, ATTN_BLK, 0))
    band_first = jnp.where((kj >= first_lo) & (kj <= qi + ATTN_BLK), 0.0, NEG_BIG)
    q_lane = lax.broadcasted_iota(jnp.int32, (ATTN_BLK, LANES), 1)
    q_is_a = (q_lane % HEAD_DIM) < HEAD_DIM // 2
    v_lane = lax.broadcasted_iota(jnp.int32, (win, LANES), 1)
    v_is_a = v_lane < HEAD_DIM
    ones_a = jnp.where(v_is_a, 1.0, 0.0).astype(_BF16)
    ones_b = jnp.where(v_is_a, 0.0, 1.0).astype(_BF16)
    o_is_a = q_lane < HEAD_DIM
    nt = (((1,), (1,)), ((), ()))
    work = [(sq, j, p) for sq in range(q_ref.shape[0]) for j in range(n_blk) for p in range(PAIRS_PER_GROUP)]

    def window(prev_ref, cur_ref, sq, j, p):
        pair = slice(p * LANES, (p + 1) * LANES)
        if j == 0:
            return jnp.concatenate([prev_ref[sq, :, pair], cur_ref[sq, 0:ATTN_BLK, pair]], axis=0)
        return cur_ref[sq, (j - 1) * ATTN_BLK:(j + 1) * ATTN_BLK, pair]

    def scores(sq, j, p):
        q = q_ref[sq, j * ATTN_BLK:(j + 1) * ATTN_BLK, p * LANES:(p + 1) * LANES]
        zero = jnp.zeros_like(q)
        q2 = jnp.concatenate([jnp.where(q_is_a, q, zero), jnp.where(q_is_a, zero, q)], axis=0)
        s = lax.dot_general(q2, window(kp_ref, kc_ref, sq, j, p), nt, preferred_element_type=_F32)
        return s + (band_first if j == 0 else band)

    def finish(sq, j, p, s):
        m = jnp.max(s, axis=-1, keepdims=True)
        prob = jnp.exp2(s - m).astype(_BF16)
        prob = jnp.concatenate([prob[:ATTN_BLK], prob[ATTN_BLK:]], axis=1)
        v = window(vp_ref, vc_ref, sq, j, p)
        zero = jnp.zeros_like(v)
        v2 = jnp.concatenate([jnp.concatenate([jnp.where(v_is_a, v, zero), ones_a], axis=1),
                              jnp.concatenate([jnp.where(v_is_a, zero, v), ones_b], axis=1)], axis=0)
        r = _dot(prob, v2)
        den = r[:, LANES:]
        rows = slice(j * ATTN_BLK, (j + 1) * ATTN_BLK)
        cols = slice(p * LANES, (p + 1) * LANES)
        o_ref[sq, rows, cols] = r[:, :LANES] * (1.0 / den)
        lse_ref[sq, rows, cols] = jnp.where(o_is_a, m[:ATTN_BLK], m[ATTN_BLK:]) + jnp.log2(den)

    ahead = 1
    pending = [scores(*work[n]) for n in range(ahead)]
    for n, item in enumerate(work):
        s = pending.pop(0)
        if n + ahead < len(work):
            pending.append(scores(*work[n + ahead]))
        finish(*item, s)


def _attn_group(q, k, v):
    batch, d, rows, _ = q.shape
    n_seq = batch * d
    q, k, v = (a.reshape(n_seq, rows, GROUP_WIDTH) for a in (q, k, v))
    q_tile = min(ATTN_Q_TILE, rows)
    seqs = ATTN_Q_TILE // q_tile
    blocks_per_step = q_tile // ATTN_BLK

    def cur(s, i):
        return (s, i, 0)

    def prev(s, i):
        return (s, jnp.maximum(i * blocks_per_step - 1, 0), 0)

    cur_spec = pl.BlockSpec((seqs, q_tile, GROUP_WIDTH), cur)
    prev_spec = pl.BlockSpec((seqs, ATTN_BLK, GROUP_WIDTH), prev)
    out_shape = jax.ShapeDtypeStruct((n_seq, rows, GROUP_WIDTH), _F32)
    o, lse = pl.pallas_call(
        _attn_kernel,
        grid=(n_seq // seqs, rows // q_tile),
        in_specs=[cur_spec, prev_spec, cur_spec, prev_spec, cur_spec],
        out_specs=[cur_spec, cur_spec],
        out_shape=[out_shape, out_shape],
        compiler_params=_params(("parallel", "arbitrary")),
        name=f"window_attn_d{d}",
    )(q, k, k, v, v)
    shape = (batch, d, rows, GROUP_WIDTH)
    return o.reshape(shape), lse.reshape(shape)


def kernel(x, positions, norm_gain, ffn_w_gate, ffn_w_up, ffn_w_down, pool_w_in, pool_w_group, pool_scale,
           pool_w_out, kv_norm_gain, w_k, w_v, attn_w_q, attn_w_o):
    batch, seq, _ = x.shape
    m = batch * seq
    gains = norm_gain.reshape(2, 6, 1, D_MODEL)

    def ffn_f32(layer, slot):
        pre_gain = norm_gain[layer, 4 * slot]
        return ((ffn_w_gate, (layer, slot), False, pre_gain), (ffn_w_up, (layer, slot), False, pre_gain),
                (ffn_w_down, (layer, slot), False, None))

    def ffn_post_gain(layer, slot):
        return (0.5 * gains[layer, 4 * slot + 1],)

    def rounded(w, gain=None):
        return (w if gain is None else w * gain[:, None]).astype(_BF16)

    pos = positions.astype(_F32).reshape(m, 1)
    inv_freq = ROPE_THETA ** (-jnp.arange(0, HEAD_DIM, 2, dtype=_F32) / HEAD_DIM)
    invf = jnp.tile(inv_freq, LANES // (HEAD_DIM // 2)).reshape(1, LANES)

    h = x.reshape(m, D_MODEL)
    pool_w = (gains[0, 3], rounded(pool_w_in[0], norm_gain[0, 2]), pool_w_group[0],
              pool_scale[0].reshape(1, D_MODEL), pool_w_out[0])
    ffn00 = tuple(rounded(w[lead], g) for w, lead, _, g in ffn_f32(0, 0))
    h, (*ffn01, wk_bf, wv_bf) = _ffn_pool(h, ffn_post_gain(0, 0) + ffn00, pool_w, seq,
                                          to_cast=ffn_f32(0, 1) + ((w_k, (), True, kv_norm_gain),
                                                                   (w_v, (), False, kv_norm_gain)))
    h, k_sh, v_sh, tables, (*ffn10, wq_bf) = _ffn_kv(h, pos, ffn_post_gain(0, 1) + tuple(ffn01),
                                                     (invf, wk_bf, wv_bf), batch, seq,
                                                     to_cast=ffn_f32(1, 0) + ((attn_w_q, (0,), True, norm_gain[1, 2]),))
    h, q, (*ffn11, wo_bf) = _ffn_q(h, tables, ffn_post_gain(1, 0) + tuple(ffn10), (wq_bf,), batch, seq,
                                   to_cast=ffn_f32(1, 1) + ((attn_w_o, (0,), False, None),))
    outs, lses = [], []
    for g in range(N_GROUPS):
        o, lse = _attn_group(q[g], k_sh[g], v_sh[g])
        outs.append(o)
        lses.append(lse)
    h = _merge_ffn(h, (gains[1, 3], wo_bf), outs, lses, ffn_post_gain(1, 1) + tuple(ffn11), batch, seq)
    return h.reshape(batch, seq, D_MODEL)
```

```python
import functools
import math

import jax
import jax.numpy as jnp
from jax import lax
from jax.experimental import pallas as pl
from jax.experimental.pallas import tpu as pltpu

D_MODEL = 1024
D_FF = 2816
POOL_WINDOWS = (2, 4, 8, 16)
POOL_GROUP_DIM = D_MODEL // len(POOL_WINDOWS)
POOL_HALO = 16
ATTN_DILATIONS = (1, 4, 16)
N_GROUPS = len(ATTN_DILATIONS)
ATTN_BLK = 128
HEAD_DIM = 64
HEADS_PER_GROUP = 8
GROUP_WIDTH = HEADS_PER_GROUP * HEAD_DIM
QKV_WIDTH = N_GROUPS * GROUP_WIDTH
ROPE_THETA = 10000.0
Q_SCALE = HEAD_DIM ** -0.5 * math.log2(math.e)
RMS_EPS = 1e-6
NEG_BIG = -1e30

LANES = 128
PAIRS_PER_GROUP = GROUP_WIDTH // LANES
QKV_TILES = QKV_WIDTH // LANES
ROW_TILE = 512
ATTN_Q_TILE = 2048
SPLIT_STRIDE = 4
FF_CHUNK = 256
VMEM_LIMIT_BYTES = 56 * 1024 * 1024

_F32 = jnp.float32
_BF16 = jnp.bfloat16


def _rms_norm(x, gain):
    ms = jnp.mean(x * x, axis=-1, keepdims=True)
    return x * lax.rsqrt(ms + RMS_EPS) * gain


def _dot(a, b):
    return jnp.dot(a, b, preferred_element_type=_F32)


def _const_spec(shape):
    return pl.BlockSpec(shape, lambda *_: (0,) * len(shape), pipeline_mode=pl.Buffered(1))


def _params(semantics):
    return pltpu.CompilerParams(dimension_semantics=semantics, vmem_limit_bytes=VMEM_LIMIT_BYTES)


def _ffn_steps(read, write, gpre_ref, gpost_ref, wg_ref, wu_ref, wd_ref, xn_ref, acc_ref):
    def start():
        xn_ref[...] = _rms_norm(read(), gpre_ref[...]).astype(_BF16)

    def chunk(c):
        def run():
            hi = min(c + FF_CHUNK, D_FF)
            g = _dot(xn_ref[...], wg_ref[:, c:hi])
            u = _dot(xn_ref[...], wu_ref[:, c:hi])
            a = (g * jax.nn.sigmoid(g) * u).astype(_BF16)
            down = _dot(a, wd_ref[c:hi, :])
            if c == 0:
                acc_ref[...] = down
            else:
                acc_ref[...] += down
        return run

    def end():
        write(read() + _rms_norm(acc_ref[...], gpost_ref[...]))

    return [start] + [chunk(c) for c in range(0, D_FF, FF_CHUNK)] + [end]


def _pool_out_weight(wgrp_ref, scale_ref, wout_ref, weff_ref):
    for g in range(len(POOL_WINDOWS)):
        rows = slice(g * POOL_GROUP_DIM, (g + 1) * POOL_GROUP_DIM)
        weff_ref[rows, :] = jnp.dot(wgrp_ref[g] * scale_ref[:, rows], wout_ref[rows, :], preferred_element_type=_F32,
                                    precision=lax.Precision.HIGHEST).astype(_BF16)


def _pool_steps(read, write, j, gpre_ref, gpost_ref, win_ref, weff_ref, ubuf_ref):
    st = {"p": []}

    def start():
        st["x"] = read()
        hm = _rms_norm(st["x"], gpre_ref[...]).astype(_BF16)
        st["u"] = _dot(hm, win_ref[...])
        ubuf_ref[0:POOL_HALO, :] = jnp.where(j == 0, 0.0, ubuf_ref[0:POOL_HALO, :])
        ubuf_ref[POOL_HALO:, :] = st["u"]

    def group(g, w):
        def run():
            tm = st["x"].shape[0]
            cols = slice(g * POOL_GROUP_DIM, (g + 1) * POOL_GROUP_DIM)
            t = j * tm + lax.broadcasted_iota(jnp.int32, (tm, 1), 0)
            s = ubuf_ref[:, cols]
            k = 1
            while k < w:
                s = s + pltpu.roll(s, k, axis=0)
                k *= 2
            count = jnp.minimum(t + 1, w).astype(_F32)
            p = s[POOL_HALO:] / count - st["u"][:, cols]
            st["p"].append(p.astype(_BF16))
        return run

    def end():
        tm = st["x"].shape[0]
        mix = _dot(jnp.concatenate(st["p"], axis=-1), weff_ref[...])
        ubuf_ref[0:POOL_HALO, :] = st["u"][tm - POOL_HALO:, :]
        write(st["x"] + _rms_norm(mix, gpost_ref[...]))

    return [start] + [group(g, w) for g, w in enumerate(POOL_WINDOWS)] + [end]


def _rope_tables(c, s):
    lane = lax.broadcasted_iota(jnp.int32, c.shape, 1)
    return c, jnp.where(lane < LANES // 2, -s, s)


def _rope_tile(xt, tables):
    c, s_signed = tables
    return xt * c + pltpu.roll(xt, LANES // 2, axis=1) * s_signed


def _pair_interleaved(t):
    half = HEAD_DIM // 2
    block = lax.broadcasted_iota(jnp.int32, t.shape, 1) // half
    return jnp.where(block == 1, pltpu.roll(t, LANES - half, axis=1),
                     jnp.where(block == 2, pltpu.roll(t, half, axis=1), t))


def _projection_steps(st, xn_ref, w_ref, slab_ref, out_refs, rope=False, scale=None):
    def project(g):
        def run():
            y = _dot(xn_ref[...], w_ref[:, g * GROUP_WIDTH:(g + 1) * GROUP_WIDTH])
            for p in range(PAIRS_PER_GROUP):
                yt = y[:, p * LANES:(p + 1) * LANES]
                if rope:
                    yt = _rope_tile(yt, st["tables"])
                if scale is not None:
                    yt = yt * scale
                slab_ref[g * PAIRS_PER_GROUP + p] = yt
        return run

    def scatter(g, d):
        def run():
            tm = slab_ref.shape[1]
            n = tm // d
            for p in range(PAIRS_PER_GROUP):
                src, first, stride = g * PAIRS_PER_GROUP + p, (lambda r: r), d
                if d > SPLIT_STRIDE:
                    part = tm // SPLIT_STRIDE
                    for r_lo in range(SPLIT_STRIDE):
                        slab_ref[QKV_TILES + p, r_lo * part:(r_lo + 1) * part, :] = (
                            slab_ref[src, pl.ds(r_lo, part, stride=SPLIT_STRIDE), :])
                    src, stride = QKV_TILES + p, d // SPLIT_STRIDE
                    first = lambda r: (r % SPLIT_STRIDE) * part + r // SPLIT_STRIDE
                for r in range(d):
                    rows = slab_ref[src, pl.ds(first(r), n, stride=stride), :]
                    out_refs[g][0, r, :, p * LANES:(p + 1) * LANES] = rows.astype(_BF16)
        return run

    steps = []
    for g, d in enumerate(ATTN_DILATIONS):
        steps += [project(g), scatter(g, d)]
    return steps


def _kv_steps(read, pos_ref, gain_ref, invf_ref, wk_ref, wv_ref, k_refs, v_refs, cos_ref, sin_ref, slab_ref, xn_ref):
    st = {}

    def start():
        xn_ref[...] = _rms_norm(read(), gain_ref[...]).astype(_BF16)
        ang = pos_ref[...] * invf_ref[...]
        cos_ref[...] = jnp.cos(ang)
        sin_ref[...] = jnp.sin(ang)
        st["tables"] = _rope_tables(cos_ref[...], sin_ref[...])

    return ([start] + _projection_steps(st, xn_ref, wk_ref, slab_ref, k_refs, rope=True)
            + _projection_steps(st, xn_ref, wv_ref, slab_ref, v_refs))


def _q_steps(read, cos_ref, sin_ref, gain_ref, wq_ref, q_refs, slab_ref, xn_ref):
    st = {}

    def start():
        xn_ref[...] = _rms_norm(read(), gain_ref[...]).astype(_BF16)
        st["tables"] = _rope_tables(cos_ref[...], sin_ref[...])

    return [start] + _projection_steps(st, xn_ref, wq_ref, slab_ref, q_refs, rope=True, scale=Q_SCALE)


def _merge_steps(read, write, gpost_ref, wo_ref, o_refs, l_refs, nat_ref):
    st = {"o": []}
    permuted = ((o_refs[1], ATTN_DILATIONS[1]), (l_refs[1], ATTN_DILATIONS[1]),
                (o_refs[2], ATTN_DILATIONS[2]), (l_refs[2], ATTN_DILATIONS[2]))

    def lane_tile(p):
        def run():
            cols = slice(p * LANES, (p + 1) * LANES)
            for a, (ref, d) in enumerate(permuted):
                n = nat_ref.shape[2] // d
                for r in range(d):
                    nat_ref[a, p, pl.ds(r, n, stride=d), :] = ref[0, r, :, cols]
            o0, l0 = o_refs[0][0, 0, :, cols], l_refs[0][0, 0, :, cols]
            o1, l1, o2, l2 = (nat_ref[a, p] for a in range(4))
            top = jnp.maximum(jnp.maximum(l0, l1), l2)
            e0, e1, e2 = jnp.exp2(l0 - top), jnp.exp2(l1 - top), jnp.exp2(l2 - top)
            st["o"].append(((e0 * o0 + e1 * o1 + e2 * o2) / (e0 + e1 + e2)).astype(_BF16))
        return run

    def end():
        mix = _dot(jnp.concatenate(st["o"], axis=-1), wo_ref[...])
        write(read() + _rms_norm(mix, gpost_ref[...]))

    return [lane_tile(p) for p in range(PAIRS_PER_GROUP)] + [end]


def _run(steps):
    for step in steps:
        step()


def _two_stage(n_tiles, first, second, init=None):
    i = pl.program_id(0)
    if init is not None:
        pl.when(i == 0)(init)

    @pl.when(i > 0)
    def _():
        _run(second())

    @pl.when(i < n_tiles)
    def _():
        _run(first())


def _cur_tile(n_tiles):
    return lambda i: jnp.minimum(i, n_tiles - 1)


def _prev_tile():
    return lambda i: jnp.maximum(i - 1, 0)


def _row_spec(tile_of):
    return pl.BlockSpec((ROW_TILE, D_MODEL), lambda i: (tile_of(i), 0))


def _residue_major_specs(batch, seq, dtype, tile_of):
    tiles_per_seq = seq // ROW_TILE
    specs, shapes = [], []
    for d in ATTN_DILATIONS:
        specs.append(pl.BlockSpec((1, d, ROW_TILE // d, GROUP_WIDTH),
                                  lambda i: (tile_of(i) // tiles_per_seq, 0, tile_of(i) % tiles_per_seq, 0)))
        shapes.append(jax.ShapeDtypeStruct((batch, d, seq // d, GROUP_WIDTH), dtype))
    return specs, shapes


def _ffn_weight_specs():
    return [_const_spec((1, D_MODEL)), _const_spec((1, D_MODEL)),
            _const_spec((D_MODEL, D_FF)), _const_spec((D_MODEL, D_FF)), _const_spec((D_FF, D_MODEL))]


_HANDOVER = pltpu.VMEM((ROW_TILE, D_MODEL), _F32)
_SLAB = pltpu.VMEM((QKV_TILES + PAIRS_PER_GROUP, ROW_TILE, LANES), _F32)
_XN_BUF = pltpu.VMEM((ROW_TILE, D_MODEL), _BF16)
_FFN_WORK = [_XN_BUF, pltpu.VMEM((ROW_TILE, D_MODEL), _F32)]
BF16_SUBLANES = 16


def _hand_over(hand_ref):
    def put(y):
        hand_ref[...] = y

    def get():
        return hand_ref[...]

    return put, get


def _split(refs, *counts):
    out, at = [], 0
    for n in counts:
        out.append(refs[at:at + n])
        at += n
    assert at == len(refs)
    return out


def _cast_specs(weights, n_tiles):
    in_specs, out_specs, out_shapes = [], [], []
    for w, lead, _ in weights:
        rows, cols = w.shape[len(lead):]
        n_blocks = max(n for n in range(1, n_tiles + 1) if rows % (n * BF16_SUBLANES) == 0)

        def block(i, n_blocks=n_blocks):
            return (jnp.minimum(i, n_blocks - 1), 0)

        in_specs.append(pl.BlockSpec((None,) * len(lead) + (rows // n_blocks, cols),
                                     lambda i, lead=lead, block=block: lead + block(i)))
        out_specs.append(pl.BlockSpec((rows // n_blocks, cols), block))
        out_shapes.append(jax.ShapeDtypeStruct((rows, cols), _BF16))
    return in_specs, out_specs, out_shapes


def _cast_blocks(src_refs, dst_refs, interleave):
    for src, dst, flag in zip(src_refs, dst_refs, interleave):
        if flag:
            for j in range(src.shape[1] // LANES):
                cols = slice(j * LANES, (j + 1) * LANES)
                dst[:, cols] = _pair_interleaved(src[:, cols]).astype(_BF16)
        else:
            dst[...] = src[...].astype(_BF16)


def _ffn_pool_kernel(n_tiles, tiles_per_seq, interleave, h_ref, *refs):
    n_cast = len(interleave)
    ffn_w, pool_w, cast_src, (o_ref,), cast_dst, (hand_ref, ubuf_ref, weff_ref, *work) = _split(
        refs, 5, 6, n_cast, 1, n_cast, 5)
    gpre_ref, gpost_ref, win_ref, wgrp_ref, scale_ref, wout_ref = pool_w
    put, get = _hand_over(hand_ref)
    j = (pl.program_id(0) - 1) % tiles_per_seq

    def write(y):
        o_ref[...] = y

    def init():
        ubuf_ref[0:POOL_HALO, :] = jnp.zeros((POOL_HALO, D_MODEL), _F32)
        _pool_out_weight(wgrp_ref, scale_ref, wout_ref, weff_ref)

    _cast_blocks(cast_src, cast_dst, interleave)
    _two_stage(n_tiles,
               lambda: _ffn_steps(lambda: h_ref[...], put, *ffn_w, *work),
               lambda: _pool_steps(get, write, j, gpre_ref, gpost_ref, win_ref, weff_ref, ubuf_ref),
               init=init)


def _ffn_pool(h, ffn_w, pool_w, seq, to_cast):
    m = h.shape[0]
    n_tiles = m // ROW_TILE
    n_grp = len(POOL_WINDOWS)
    cast_in, cast_out, cast_shapes = _cast_specs(to_cast, n_tiles)
    outs = pl.pallas_call(
        functools.partial(_ffn_pool_kernel, n_tiles, seq // ROW_TILE, tuple(f for _, _, f in to_cast)),
        grid=(n_tiles + 1,),
        in_specs=[_row_spec(_cur_tile(n_tiles))] + _ffn_weight_specs() + [
            _const_spec((1, D_MODEL)), _const_spec((1, D_MODEL)), _const_spec((D_MODEL, D_MODEL)),
            _const_spec((n_grp, POOL_GROUP_DIM, POOL_GROUP_DIM)), _const_spec((1, D_MODEL)),
            _const_spec((D_MODEL, D_MODEL))] + cast_in,
        out_specs=[_row_spec(_prev_tile())] + cast_out,
        out_shape=[jax.ShapeDtypeStruct((m, D_MODEL), _F32)] + cast_shapes,
        scratch_shapes=[_HANDOVER, pltpu.VMEM((POOL_HALO + ROW_TILE, D_MODEL), _F32),
                        pltpu.VMEM((D_MODEL, D_MODEL), _BF16)] + _FFN_WORK,
        compiler_params=_params(("arbitrary",)),
        name="ffn_pool",
    )(h, *ffn_w, *pool_w, *(w for w, _, _ in to_cast))
    return outs[0], outs[1:]


def _ffn_and_keep(h_ref, o_ref, put, ffn_w, work):
    def write(y):
        o_ref[...] = y
        put(y)

    return lambda: _ffn_steps(lambda: h_ref[...], write, *ffn_w, *work)


def _table_spec():
    return pl.BlockSpec((ROW_TILE, LANES), lambda i: (jnp.maximum(i - 1, 0), 0))


def _ffn_kv_kernel(n_tiles, interleave, h_ref, pos_ref, *refs):
    n_cast = len(interleave)
    (ffn_w, kv_w, cast_src, (o_ref,), k_refs, v_refs, (cos_ref, sin_ref), cast_dst,
     (hand_ref, slab_ref, xn2_ref, *work)) = _split(refs, 5, 4, n_cast, 1, N_GROUPS, N_GROUPS, 2, n_cast, 5)
    put, get = _hand_over(hand_ref)
    _cast_blocks(cast_src, cast_dst, interleave)
    _two_stage(n_tiles, _ffn_and_keep(h_ref, o_ref, put, ffn_w, work),
               lambda: _kv_steps(get, pos_ref, *kv_w, k_refs, v_refs, cos_ref, sin_ref, slab_ref, xn2_ref))


def _ffn_kv(h, pos, ffn_w, kv_w, batch, seq, to_cast):
    m = h.shape[0]
    n_tiles = m // ROW_TILE
    specs, shapes = _residue_major_specs(batch, seq, _BF16, _prev_tile())
    cast_in, cast_out, cast_shapes = _cast_specs(to_cast, n_tiles)
    table_shape = jax.ShapeDtypeStruct((m, LANES), _F32)
    outs = pl.pallas_call(
        functools.partial(_ffn_kv_kernel, n_tiles, tuple(f for _, _, f in to_cast)),
        grid=(n_tiles + 1,),
        in_specs=[_row_spec(_cur_tile(n_tiles)), pl.BlockSpec((ROW_TILE, 1), lambda i: (jnp.maximum(i - 1, 0), 0))]
        + _ffn_weight_specs() + [_const_spec((1, D_MODEL)), _const_spec((1, LANES)),
                                 _const_spec((D_MODEL, QKV_WIDTH)), _const_spec((D_MODEL, QKV_WIDTH))] + cast_in,
        out_specs=[_row_spec(_cur_tile(n_tiles))] + specs * 2 + [_table_spec()] * 2 + cast_out,
        out_shape=[jax.ShapeDtypeStruct((m, D_MODEL), _F32)] + shapes * 2 + [table_shape] * 2 + cast_shapes,
        scratch_shapes=[_HANDOVER, _SLAB, _XN_BUF] + _FFN_WORK,
        compiler_params=_params(("arbitrary",)),
        name="ffn_kv",
    )(h, pos, *ffn_w, *kv_w, *(w for w, _, _ in to_cast))
    return outs[0], outs[1:4], outs[4:7], outs[7:9], outs[9:]


def _ffn_q_kernel(n_tiles, interleave, h_ref, cos_ref, sin_ref, *refs):
    n_cast = len(interleave)
    ffn_w, q_w, cast_src, (o_ref,), q_refs, cast_dst, (hand_ref, slab_ref, xn2_ref, *work) = _split(
        refs, 5, 2, n_cast, 1, N_GROUPS, n_cast, 5)
    put, get = _hand_over(hand_ref)
    _cast_blocks(cast_src, cast_dst, interleave)
    _two_stage(n_tiles, _ffn_and_keep(h_ref, o_ref, put, ffn_w, work),
               lambda: _q_steps(get, cos_ref, sin_ref, *q_w, q_refs, slab_ref, xn2_ref))


def _ffn_q(h, tables, ffn_w, q_w, batch, seq, to_cast):
    m = h.shape[0]
    n_tiles = m // ROW_TILE
    specs, shapes = _residue_major_specs(batch, seq, _BF16, _prev_tile())
    cast_in, cast_out, cast_shapes = _cast_specs(to_cast, n_tiles)
    outs = pl.pallas_call(
        functools.partial(_ffn_q_kernel, n_tiles, tuple(f for _, _, f in to_cast)),
        grid=(n_tiles + 1,),
        in_specs=[_row_spec(_cur_tile(n_tiles))] + [_table_spec()] * 2 + _ffn_weight_specs()
        + [_const_spec((1, D_MODEL)), _const_spec((D_MODEL, QKV_WIDTH))] + cast_in,
        out_specs=[_row_spec(_cur_tile(n_tiles))] + specs + cast_out,
        out_shape=[jax.ShapeDtypeStruct((m, D_MODEL), _F32)] + shapes + cast_shapes,
        scratch_shapes=[_HANDOVER, _SLAB, _XN_BUF] + _FFN_WORK,
        compiler_params=_params(("arbitrary",)),
        name="ffn_q",
    )(h, *tables, *ffn_w, *q_w, *(w for w, _, _ in to_cast))
    return outs[0], outs[1:4], outs[4:]


def _merge_ffn_kernel(n_tiles, h_ref, *refs):
    merge_w, o_refs, l_refs, ffn_w, (out_ref, hand_ref, nat_ref, *work) = (
        refs[:2], refs[2:5], refs[5:8], refs[8:13], refs[13:])
    put, get = _hand_over(hand_ref)

    def write(y):
        out_ref[...] = y

    _two_stage(n_tiles,
               lambda: _merge_steps(lambda: h_ref[...], put, *merge_w, o_refs, l_refs, nat_ref),
               lambda: _ffn_steps(get, write, *ffn_w, *work))


def _merge_ffn(h, merge_w, outs, lses, ffn_w, batch, seq):
    m = h.shape[0]
    n_tiles = m // ROW_TILE
    specs, _ = _residue_major_specs(batch, seq, _F32, _cur_tile(n_tiles))
    return pl.pallas_call(
        functools.partial(_merge_ffn_kernel, n_tiles),
        grid=(n_tiles + 1,),
        in_specs=[_row_spec(_cur_tile(n_tiles)), _const_spec((1, D_MODEL)), _const_spec((GROUP_WIDTH, D_MODEL))]
        + specs * 2 + _ffn_weight_specs(),
        out_specs=_row_spec(_prev_tile()),
        out_shape=jax.ShapeDtypeStruct((m, D_MODEL), _F32),
        scratch_shapes=[_HANDOVER, pltpu.VMEM((4, PAIRS_PER_GROUP, ROW_TILE, LANES), _F32)] + _FFN_WORK,
        compiler_params=_params(("arbitrary",)),
        name="merge_ffn",
    )(h, *merge_w, *outs, *lses, *ffn_w)


def _attn_kernel(q_ref, kp_ref, kc_ref, vp_ref, vc_ref, o_ref, lse_ref):
    step = pl.program_id(1)
    n_blk = q_ref.shape[1] // ATTN_BLK
    win = 2 * ATTN_BLK
    qi = lax.broadcasted_iota(jnp.int32, (win, win), 0) % ATTN_BLK
    kj = lax.broadcasted_iota(jnp.int32, (win, win), 1)
    band = jnp.where((kj >= qi) & (kj <= qi + ATTN_BLK), 0.0, NEG_BIG)
    first_lo = jnp.maximum(qi, jnp.where(step == 0, ATTN_BLK, 0))
    band_first = jnp.where((kj >= first_lo) & (kj <= qi + ATTN_BLK), 0.0, NEG_BIG)
    q_lane = lax.broadcasted_iota(jnp.int32, (ATTN_BLK, LANES), 1)
    q_is_a = (q_lane % HEAD_DIM) < HEAD_DIM // 2
    v_lane = lax.broadcasted_iota(jnp.int32, (win, LANES), 1)
    v_is_a = v_lane < HEAD_DIM
    ones_a = jnp.where(v_is_a, 1.0, 0.0).astype(_BF16)
    ones_b = jnp.where(v_is_a, 0.0, 1.0).astype(_BF16)
    o_is_a = q_lane < HEAD_DIM
    nt = (((1,), (1,)), ((), ()))
    work = [(sq, j, p) for sq in range(q_ref.shape[0]) for j in range(n_blk) for p in range(PAIRS_PER_GROUP)]

    def window(prev_ref, cur_ref, sq, j, p):
        pair = slice(p * LANES, (p + 1) * LANES)
        if j == 0:
            return jnp.concatenate([prev_ref[sq, :, pair], cur_ref[sq, 0:ATTN_BLK, pair]], axis=0)
        return cur_ref[sq, (j - 1) * ATTN_BLK:(j + 1) * ATTN_BLK, pair]

    def scores(sq, j, p):
        q = q_ref[sq, j * ATTN_BLK:(j + 1) * ATTN_BLK, p * LANES:(p + 1) * LANES]
        zero = jnp.zeros_like(q)
        q2 = jnp.concatenate([jnp.where(q_is_a, q, zero), jnp.where(q_is_a, zero, q)], axis=0)
        s = lax.dot_general(q2, window(kp_ref, kc_ref, sq, j, p), nt, preferred_element_type=_F32)
        return s + (band_first if j == 0 else band)

    def finish(sq, j, p, s):
        m = jnp.max(s, axis=-1, keepdims=True)
        prob = jnp.exp2(s - m).astype(_BF16)
        prob = jnp.concatenate([prob[:ATTN_BLK], prob[ATTN_BLK:]], axis=1)
        v = window(vp_ref, vc_ref, sq, j, p)
        zero = jnp.zeros_like(v)
        v2 = jnp.concatenate([jnp.concatenate([jnp.where(v_is_a, v, zero), ones_a], axis=1),
                              jnp.concatenate([jnp.where(v_is_a, zero, v), ones_b], axis=1)], axis=0)
        r = _dot(prob, v2)
        den = r[:, LANES:]
        rows = slice(j * ATTN_BLK, (j + 1) * ATTN_BLK)
        cols = slice(p * LANES, (p + 1) * LANES)
        o_ref[sq, rows, cols] = r[:, :LANES] * (1.0 / den)
        lse_ref[sq, rows, cols] = jnp.where(o_is_a, m[:ATTN_BLK], m[ATTN_BLK:]) + jnp.log2(den)

    ahead = 1
    pending = [scores(*work[n]) for n in range(ahead)]
    for n, item in enumerate(work):
        s = pending.pop(0)
        if n + ahead < len(work):
            pending.append(scores(*work[n + ahead]))
        finish(*item, s)


def _attn_group(q, k, v):
    batch, d, rows, _ = q.shape
    n_seq = batch * d
    q, k, v = (a.reshape(n_seq, rows, GROUP_WIDTH) for a in (q, k, v))
    q_tile = min(ATTN_Q_TILE, rows)
    seqs = ATTN_Q_TILE // q_tile
    blocks_per_step = q_tile // ATTN_BLK

    def cur(s, i):
        return (s, i, 0)

    def prev(s, i):
        return (s, jnp.maximum(i * blocks_per_step - 1, 0), 0)

    cur_spec = pl.BlockSpec((seqs, q_tile, GROUP_WIDTH), cur)
    prev_spec = pl.BlockSpec((seqs, ATTN_BLK, GROUP_WIDTH), prev)
    out_shape = jax.ShapeDtypeStruct((n_seq, rows, GROUP_WIDTH), _F32)
    o, lse = pl.pallas_call(
        _attn_kernel,
        grid=(n_seq // seqs, rows // q_tile),
        in_specs=[cur_spec, prev_spec, cur_spec, prev_spec, cur_spec],
        out_specs=[cur_spec, cur_spec],
        out_shape=[out_shape, out_shape],
        compiler_params=_params(("parallel", "arbitrary")),
        name=f"window_attn_d{d}",
    )(q, k, k, v, v)
    shape = (batch, d, rows, GROUP_WIDTH)
    return o.reshape(shape), lse.reshape(shape)


def kernel(x, positions, norm_gain, ffn_w_gate, ffn_w_up, ffn_w_down, pool_w_in, pool_w_group, pool_scale,
           pool_w_out, kv_norm_gain, w_k, w_v, attn_w_q, attn_w_o):
    batch, seq, _ = x.shape
    m = batch * seq
    gains = norm_gain.reshape(2, 6, 1, D_MODEL)
    bf = lambda w: w.astype(_BF16)

    def ffn_f32(layer, slot):
        return tuple((w, (layer, slot), False) for w in (ffn_w_gate, ffn_w_up, ffn_w_down))

    def ffn_gains(layer, slot):
        return (gains[layer, 4 * slot], 0.5 * gains[layer, 4 * slot + 1])

    pos = positions.astype(_F32).reshape(m, 1)
    inv_freq = ROPE_THETA ** (-jnp.arange(0, HEAD_DIM, 2, dtype=_F32) / HEAD_DIM)
    invf = jnp.tile(inv_freq, LANES // (HEAD_DIM // 2)).reshape(1, LANES)

    h = x.reshape(m, D_MODEL)
    pool_w = (gains[0, 2], gains[0, 3], bf(pool_w_in[0]), pool_w_group[0],
              pool_scale[0].reshape(1, D_MODEL), pool_w_out[0])
    ffn00 = tuple(bf(w[lead]) for w, lead, _ in ffn_f32(0, 0))
    h, (*ffn01, wk_bf, wv_bf) = _ffn_pool(h, ffn_gains(0, 0) + ffn00, pool_w, seq,
                                          to_cast=ffn_f32(0, 1) + ((w_k, (), True), (w_v, (), False)))
    kv_w = (kv_norm_gain.reshape(1, D_MODEL), invf, wk_bf, wv_bf)
    h, k_sh, v_sh, tables, (*ffn10, wq_bf) = _ffn_kv(h, pos, ffn_gains(0, 1) + tuple(ffn01), kv_w, batch, seq,
                                                     to_cast=ffn_f32(1, 0) + ((attn_w_q, (0,), True),))
    h, q, (*ffn11, wo_bf) = _ffn_q(h, tables, ffn_gains(1, 0) + tuple(ffn10), (gains[1, 2], wq_bf), batch, seq,
                                   to_cast=ffn_f32(1, 1) + ((attn_w_o, (0,), False),))
    outs, lses = [], []
    for g in range(N_GROUPS):
        o, lse = _attn_group(q[g], k_sh[g], v_sh[g])
        outs.append(o)
        lses.append(lse)
    h = _merge_ffn(h, (gains[1, 3], wo_bf), outs, lses, ffn_gains(1, 1) + tuple(ffn11), batch, seq)
    return h.reshape(batch, seq, D_MODEL)
```

```python
import functools
import math

import jax
import jax.numpy as jnp
from jax import lax
from jax.experimental import pallas as pl
from jax.experimental.pallas import tpu as pltpu

D_MODEL = 1024
D_FF = 2816
POOL_WINDOWS = (2, 4, 8, 16)
POOL_GROUP_DIM = D_MODEL // len(POOL_WINDOWS)
POOL_HALO = 16
ATTN_DILATIONS = (1, 4, 16)
N_GROUPS = len(ATTN_DILATIONS)
ATTN_BLK = 128
HEAD_DIM = 64
HEADS_PER_GROUP = 8
GROUP_WIDTH = HEADS_PER_GROUP * HEAD_DIM
QKV_WIDTH = N_GROUPS * GROUP_WIDTH
ROPE_THETA = 10000.0
Q_SCALE = HEAD_DIM ** -0.5 * math.log2(math.e)
RMS_EPS = 1e-6
NEG_BIG = -1e30

LANES = 128
PAIRS_PER_GROUP = GROUP_WIDTH // LANES
QKV_TILES = QKV_WIDTH // LANES
ROW_TILE = 512
ATTN_Q_TILE = 2048
SPLIT_STRIDE = 4
FF_CHUNK = 256
VMEM_LIMIT_BYTES = 56 * 1024 * 1024

_F32 = jnp.float32
_BF16 = jnp.bfloat16


def _rms_norm(x, gain):
    ms = jnp.mean(x * x, axis=-1, keepdims=True)
    return x * lax.rsqrt(ms + RMS_EPS) * gain


def _dot(a, b):
    return jnp.dot(a, b, preferred_element_type=_F32)


def _const_spec(shape):
    return pl.BlockSpec(shape, lambda *_: (0,) * len(shape), pipeline_mode=pl.Buffered(1))


def _params(semantics):
    return pltpu.CompilerParams(dimension_semantics=semantics, vmem_limit_bytes=VMEM_LIMIT_BYTES)


def _ffn_steps(read, write, gpre_ref, gpost_ref, wg_ref, wu_ref, wd_ref, xn_ref, acc_ref):
    def start():
        xn_ref[...] = _rms_norm(read(), gpre_ref[...]).astype(_BF16)

    def chunk(c):
        def run():
            hi = min(c + FF_CHUNK, D_FF)
            g = _dot(xn_ref[...], wg_ref[:, c:hi])
            u = _dot(xn_ref[...], wu_ref[:, c:hi])
            a = ((0.5 * g) * (jnp.tanh(0.5 * g) + 1.0) * u).astype(_BF16)
            down = _dot(a, wd_ref[c:hi, :])
            if c == 0:
                acc_ref[...] = down
            else:
                acc_ref[...] += down
        return run

    def end():
        write(read() + _rms_norm(acc_ref[...], gpost_ref[...]))

    return [start] + [chunk(c) for c in range(0, D_FF, FF_CHUNK)] + [end]


def _pool_out_weight(wgrp_ref, scale_ref, wout_ref, weff_ref):
    for g in range(len(POOL_WINDOWS)):
        rows = slice(g * POOL_GROUP_DIM, (g + 1) * POOL_GROUP_DIM)
        weff_ref[rows, :] = jnp.dot(wgrp_ref[g] * scale_ref[:, rows], wout_ref[rows, :], preferred_element_type=_F32,
                                    precision=lax.Precision.HIGHEST).astype(_BF16)


def _pool_steps(read, write, j, gpre_ref, gpost_ref, win_ref, weff_ref, ubuf_ref):
    st = {"p": []}

    def start():
        st["x"] = read()
        hm = _rms_norm(st["x"], gpre_ref[...]).astype(_BF16)
        st["u"] = _dot(hm, win_ref[...])
        ubuf_ref[0:POOL_HALO, :] = jnp.where(j == 0, 0.0, ubuf_ref[0:POOL_HALO, :])
        ubuf_ref[POOL_HALO:, :] = st["u"]

    def group(g, w):
        def run():
            tm = st["x"].shape[0]
            cols = slice(g * POOL_GROUP_DIM, (g + 1) * POOL_GROUP_DIM)
            t = j * tm + lax.broadcasted_iota(jnp.int32, (tm, 1), 0)
            s = ubuf_ref[:, cols]
            k = 1
            while k < w:
                s = s + pltpu.roll(s, k, axis=0)
                k *= 2
            count = jnp.minimum(t + 1, w).astype(_F32)
            p = s[POOL_HALO:] / count - st["u"][:, cols]
            st["p"].append(p.astype(_BF16))
        return run

    def end():
        tm = st["x"].shape[0]
        mix = _dot(jnp.concatenate(st["p"], axis=-1), weff_ref[...])
        ubuf_ref[0:POOL_HALO, :] = st["u"][tm - POOL_HALO:, :]
        write(st["x"] + _rms_norm(mix, gpost_ref[...]))

    return [start] + [group(g, w) for g, w in enumerate(POOL_WINDOWS)] + [end]


def _rope_tables(c, s):
    lane = lax.broadcasted_iota(jnp.int32, c.shape, 1)
    return c, jnp.where(lane < LANES // 2, -s, s)


def _rope_tile(xt, tables):
    c, s_signed = tables
    return xt * c + pltpu.roll(xt, LANES // 2, axis=1) * s_signed


def _pair_interleaved(t):
    half = HEAD_DIM // 2
    block = lax.broadcasted_iota(jnp.int32, t.shape, 1) // half
    return jnp.where(block == 1, pltpu.roll(t, LANES - half, axis=1),
                     jnp.where(block == 2, pltpu.roll(t, half, axis=1), t))


def _projection_steps(st, xn_ref, w_ref, slab_ref, out_refs, rope=False, scale=None):
    def project(g):
        def run():
            y = _dot(xn_ref[...], w_ref[:, g * GROUP_WIDTH:(g + 1) * GROUP_WIDTH])
            for p in range(PAIRS_PER_GROUP):
                yt = y[:, p * LANES:(p + 1) * LANES]
                if rope:
                    yt = _rope_tile(yt, st["tables"])
                if scale is not None:
                    yt = yt * scale
                slab_ref[g * PAIRS_PER_GROUP + p] = yt
        return run

    def scatter(g, d):
        def run():
            tm = slab_ref.shape[1]
            n = tm // d
            for p in range(PAIRS_PER_GROUP):
                src, first, stride = g * PAIRS_PER_GROUP + p, (lambda r: r), d
                if d > SPLIT_STRIDE:
                    part = tm // SPLIT_STRIDE
                    for r_lo in range(SPLIT_STRIDE):
                        slab_ref[QKV_TILES + p, r_lo * part:(r_lo + 1) * part, :] = (
                            slab_ref[src, pl.ds(r_lo, part, stride=SPLIT_STRIDE), :])
                    src, stride = QKV_TILES + p, d // SPLIT_STRIDE
                    first = lambda r: (r % SPLIT_STRIDE) * part + r // SPLIT_STRIDE
                for r in range(d):
                    rows = slab_ref[src, pl.ds(first(r), n, stride=stride), :]
                    out_refs[g][0, r, :, p * LANES:(p + 1) * LANES] = rows.astype(_BF16)
        return run

    steps = []
    for g, d in enumerate(ATTN_DILATIONS):
        steps += [project(g), scatter(g, d)]
    return steps


def _kv_steps(read, pos_ref, gain_ref, invf_ref, wk_ref, wv_ref, k_refs, v_refs, cos_ref, sin_ref, slab_ref, xn_ref):
    st = {}

    def start():
        xn_ref[...] = _rms_norm(read(), gain_ref[...]).astype(_BF16)
        ang = pos_ref[...] * invf_ref[...]
        cos_ref[...] = jnp.cos(ang)
        sin_ref[...] = jnp.sin(ang)
        st["tables"] = _rope_tables(cos_ref[...], sin_ref[...])

    return ([start] + _projection_steps(st, xn_ref, wk_ref, slab_ref, k_refs, rope=True)
            + _projection_steps(st, xn_ref, wv_ref, slab_ref, v_refs))


def _q_steps(read, cos_ref, sin_ref, gain_ref, wq_ref, q_refs, slab_ref, xn_ref):
    st = {}

    def start():
        xn_ref[...] = _rms_norm(read(), gain_ref[...]).astype(_BF16)
        st["tables"] = _rope_tables(cos_ref[...], sin_ref[...])

    return [start] + _projection_steps(st, xn_ref, wq_ref, slab_ref, q_refs, rope=True, scale=Q_SCALE)


def _merge_steps(read, write, gpost_ref, wo_ref, o_refs, l_refs, nat_ref):
    st = {"o": []}
    permuted = ((o_refs[1], ATTN_DILATIONS[1]), (l_refs[1], ATTN_DILATIONS[1]),
                (o_refs[2], ATTN_DILATIONS[2]), (l_refs[2], ATTN_DILATIONS[2]))

    def lane_tile(p):
        def run():
            cols = slice(p * LANES, (p + 1) * LANES)
            for a, (ref, d) in enumerate(permuted):
                tm = nat_ref.shape[2]
                n = tm // d
                if d <= SPLIT_STRIDE:
                    for r in range(d):
                        nat_ref[a, p, pl.ds(r, n, stride=d), :] = ref[0, r, :, cols]
                    continue
                part = tm // SPLIT_STRIDE
                for r in range(d):
                    start = (r % SPLIT_STRIDE) * part + r // SPLIT_STRIDE
                    nat_ref[len(permuted), p, pl.ds(start, n, stride=d // SPLIT_STRIDE), :] = ref[0, r, :, cols]
                for r_lo in range(SPLIT_STRIDE):
                    nat_ref[a, p, pl.ds(r_lo, part, stride=SPLIT_STRIDE), :] = (
                        nat_ref[len(permuted), p, r_lo * part:(r_lo + 1) * part, :])
            o0, l0 = o_refs[0][0, 0, :, cols], l_refs[0][0, 0, :, cols]
            o1, l1, o2, l2 = (nat_ref[a, p] for a in range(4))
            top = jnp.maximum(jnp.maximum(l0, l1), l2)
            e0, e1, e2 = jnp.exp2(l0 - top), jnp.exp2(l1 - top), jnp.exp2(l2 - top)
            st["o"].append(((e0 * o0 + e1 * o1 + e2 * o2) / (e0 + e1 + e2)).astype(_BF16))
        return run

    def end():
        mix = _dot(jnp.concatenate(st["o"], axis=-1), wo_ref[...])
        write(read() + _rms_norm(mix, gpost_ref[...]))

    return [lane_tile(p) for p in range(PAIRS_PER_GROUP)] + [end]


def _run(steps):
    for step in steps:
        step()


def _two_stage(n_tiles, first, second, init=None):
    i = pl.program_id(0)
    if init is not None:
        pl.when(i == 0)(init)

    @pl.when(i > 0)
    def _():
        _run(second())

    @pl.when(i < n_tiles)
    def _():
        _run(first())


def _cur_tile(n_tiles):
    return lambda i: jnp.minimum(i, n_tiles - 1)


def _prev_tile():
    return lambda i: jnp.maximum(i - 1, 0)


def _row_spec(tile_of):
    return pl.BlockSpec((ROW_TILE, D_MODEL), lambda i: (tile_of(i), 0))


def _residue_major_specs(batch, seq, dtype, tile_of):
    tiles_per_seq = seq // ROW_TILE
    specs, shapes = [], []
    for d in ATTN_DILATIONS:
        specs.append(pl.BlockSpec((1, d, ROW_TILE // d, GROUP_WIDTH),
                                  lambda i: (tile_of(i) // tiles_per_seq, 0, tile_of(i) % tiles_per_seq, 0)))
        shapes.append(jax.ShapeDtypeStruct((batch, d, seq // d, GROUP_WIDTH), dtype))
    return specs, shapes


def _ffn_weight_specs():
    return [_const_spec((1, D_MODEL)), _const_spec((1, D_MODEL)),
            _const_spec((D_MODEL, D_FF)), _const_spec((D_MODEL, D_FF)), _const_spec((D_FF, D_MODEL))]


_HANDOVER = pltpu.VMEM((ROW_TILE, D_MODEL), _F32)
_SLAB = pltpu.VMEM((QKV_TILES + PAIRS_PER_GROUP, ROW_TILE, LANES), _F32)
_XN_BUF = pltpu.VMEM((ROW_TILE, D_MODEL), _BF16)
_FFN_WORK = [_XN_BUF, pltpu.VMEM((ROW_TILE, D_MODEL), _F32)]
BF16_SUBLANES = 16


def _hand_over(hand_ref):
    def put(y):
        hand_ref[...] = y

    def get():
        return hand_ref[...]

    return put, get


def _split(refs, *counts):
    out, at = [], 0
    for n in counts:
        out.append(refs[at:at + n])
        at += n
    assert at == len(refs)
    return out


def _cast_specs(weights, n_tiles):
    in_specs, out_specs, out_shapes = [], [], []
    for w, lead, _ in weights:
        rows, cols = w.shape[len(lead):]
        n_blocks = max(n for n in range(1, n_tiles + 1) if rows % (n * BF16_SUBLANES) == 0)

        def block(i, n_blocks=n_blocks):
            return (jnp.minimum(i, n_blocks - 1), 0)

        in_specs.append(pl.BlockSpec((None,) * len(lead) + (rows // n_blocks, cols),
                                     lambda i, lead=lead, block=block: lead + block(i)))
        out_specs.append(pl.BlockSpec((rows // n_blocks, cols), block))
        out_shapes.append(jax.ShapeDtypeStruct((rows, cols), _BF16))
    return in_specs, out_specs, out_shapes


def _cast_blocks(src_refs, dst_refs, interleave):
    for src, dst, flag in zip(src_refs, dst_refs, interleave):
        if flag:
            for j in range(src.shape[1] // LANES):
                cols = slice(j * LANES, (j + 1) * LANES)
                dst[:, cols] = _pair_interleaved(src[:, cols]).astype(_BF16)
        else:
            dst[...] = src[...].astype(_BF16)


def _ffn_pool_kernel(n_tiles, tiles_per_seq, interleave, h_ref, *refs):
    n_cast = len(interleave)
    ffn_w, pool_w, cast_src, (o_ref,), cast_dst, (hand_ref, ubuf_ref, weff_ref, *work) = _split(
        refs, 5, 6, n_cast, 1, n_cast, 5)
    gpre_ref, gpost_ref, win_ref, wgrp_ref, scale_ref, wout_ref = pool_w
    put, get = _hand_over(hand_ref)
    j = (pl.program_id(0) - 1) % tiles_per_seq

    def write(y):
        o_ref[...] = y

    def init():
        ubuf_ref[0:POOL_HALO, :] = jnp.zeros((POOL_HALO, D_MODEL), _F32)
        _pool_out_weight(wgrp_ref, scale_ref, wout_ref, weff_ref)

    _cast_blocks(cast_src, cast_dst, interleave)
    _two_stage(n_tiles,
               lambda: _ffn_steps(lambda: h_ref[...], put, *ffn_w, *work),
               lambda: _pool_steps(get, write, j, gpre_ref, gpost_ref, win_ref, weff_ref, ubuf_ref),
               init=init)


def _ffn_pool(h, ffn_w, pool_w, seq, to_cast):
    m = h.shape[0]
    n_tiles = m // ROW_TILE
    n_grp = len(POOL_WINDOWS)
    cast_in, cast_out, cast_shapes = _cast_specs(to_cast, n_tiles)
    outs = pl.pallas_call(
        functools.partial(_ffn_pool_kernel, n_tiles, seq // ROW_TILE, tuple(f for _, _, f in to_cast)),
        grid=(n_tiles + 1,),
        in_specs=[_row_spec(_cur_tile(n_tiles))] + _ffn_weight_specs() + [
            _const_spec((1, D_MODEL)), _const_spec((1, D_MODEL)), _const_spec((D_MODEL, D_MODEL)),
            _const_spec((n_grp, POOL_GROUP_DIM, POOL_GROUP_DIM)), _const_spec((1, D_MODEL)),
            _const_spec((D_MODEL, D_MODEL))] + cast_in,
        out_specs=[_row_spec(_prev_tile())] + cast_out,
        out_shape=[jax.ShapeDtypeStruct((m, D_MODEL), _F32)] + cast_shapes,
        scratch_shapes=[_HANDOVER, pltpu.VMEM((POOL_HALO + ROW_TILE, D_MODEL), _F32),
                        pltpu.VMEM((D_MODEL, D_MODEL), _BF16)] + _FFN_WORK,
        compiler_params=_params(("arbitrary",)),
        name="ffn_pool",
    )(h, *ffn_w, *pool_w, *(w for w, _, _ in to_cast))
    return outs[0], outs[1:]


def _ffn_and_keep(h_ref, o_ref, put, ffn_w, work):
    def write(y):
        o_ref[...] = y
        put(y)

    return lambda: _ffn_steps(lambda: h_ref[...], write, *ffn_w, *work)


def _table_spec():
    return pl.BlockSpec((ROW_TILE, LANES), lambda i: (jnp.maximum(i - 1, 0), 0))


def _ffn_kv_kernel(n_tiles, interleave, h_ref, pos_ref, *refs):
    n_cast = len(interleave)
    (ffn_w, kv_w, cast_src, (o_ref,), k_refs, v_refs, (cos_ref, sin_ref), cast_dst,
     (hand_ref, slab_ref, xn2_ref, *work)) = _split(refs, 5, 4, n_cast, 1, N_GROUPS, N_GROUPS, 2, n_cast, 5)
    put, get = _hand_over(hand_ref)
    _cast_blocks(cast_src, cast_dst, interleave)
    _two_stage(n_tiles, _ffn_and_keep(h_ref, o_ref, put, ffn_w, work),
               lambda: _kv_steps(get, pos_ref, *kv_w, k_refs, v_refs, cos_ref, sin_ref, slab_ref, xn2_ref))


def _ffn_kv(h, pos, ffn_w, kv_w, batch, seq, to_cast):
    m = h.shape[0]
    n_tiles = m // ROW_TILE
    specs, shapes = _residue_major_specs(batch, seq, _BF16, _prev_tile())
    cast_in, cast_out, cast_shapes = _cast_specs(to_cast, n_tiles)
    table_shape = jax.ShapeDtypeStruct((m, LANES), _F32)
    outs = pl.pallas_call(
        functools.partial(_ffn_kv_kernel, n_tiles, tuple(f for _, _, f in to_cast)),
        grid=(n_tiles + 1,),
        in_specs=[_row_spec(_cur_tile(n_tiles)), pl.BlockSpec((ROW_TILE, 1), lambda i: (jnp.maximum(i - 1, 0), 0))]
        + _ffn_weight_specs() + [_const_spec((1, D_MODEL)), _const_spec((1, LANES)),
                                 _const_spec((D_MODEL, QKV_WIDTH)), _const_spec((D_MODEL, QKV_WIDTH))] + cast_in,
        out_specs=[_row_spec(_cur_tile(n_tiles))] + specs * 2 + [_table_spec()] * 2 + cast_out,
        out_shape=[jax.ShapeDtypeStruct((m, D_MODEL), _F32)] + shapes * 2 + [table_shape] * 2 + cast_shapes,
        scratch_shapes=[_HANDOVER, _SLAB, _XN_BUF] + _FFN_WORK,
        compiler_params=_params(("arbitrary",)),
        name="ffn_kv",
    )(h, pos, *ffn_w, *kv_w, *(w for w, _, _ in to_cast))
    return outs[0], outs[1:4], outs[4:7], outs[7:9], outs[9:]


def _ffn_q_kernel(n_tiles, interleave, h_ref, cos_ref, sin_ref, *refs):
    n_cast = len(interleave)
    ffn_w, q_w, cast_src, (o_ref,), q_refs, cast_dst, (hand_ref, slab_ref, xn2_ref, *work) = _split(
        refs, 5, 2, n_cast, 1, N_GROUPS, n_cast, 5)
    put, get = _hand_over(hand_ref)
    _cast_blocks(cast_src, cast_dst, interleave)
    _two_stage(n_tiles, _ffn_and_keep(h_ref, o_ref, put, ffn_w, work),
               lambda: _q_steps(get, cos_ref, sin_ref, *q_w, q_refs, slab_ref, xn2_ref))


def _ffn_q(h, tables, ffn_w, q_w, batch, seq, to_cast):
    m = h.shape[0]
    n_tiles = m // ROW_TILE
    specs, shapes = _residue_major_specs(batch, seq, _BF16, _prev_tile())
    cast_in, cast_out, cast_shapes = _cast_specs(to_cast, n_tiles)
    outs = pl.pallas_call(
        functools.partial(_ffn_q_kernel, n_tiles, tuple(f for _, _, f in to_cast)),
        grid=(n_tiles + 1,),
        in_specs=[_row_spec(_cur_tile(n_tiles))] + [_table_spec()] * 2 + _ffn_weight_specs()
        + [_const_spec((1, D_MODEL)), _const_spec((D_MODEL, QKV_WIDTH))] + cast_in,
        out_specs=[_row_spec(_cur_tile(n_tiles))] + specs + cast_out,
        out_shape=[jax.ShapeDtypeStruct((m, D_MODEL), _F32)] + shapes + cast_shapes,
        scratch_shapes=[_HANDOVER, _SLAB, _XN_BUF] + _FFN_WORK,
        compiler_params=_params(("arbitrary",)),
        name="ffn_q",
    )(h, *tables, *ffn_w, *q_w, *(w for w, _, _ in to_cast))
    return outs[0], outs[1:4], outs[4:]


def _merge_ffn_kernel(n_tiles, h_ref, *refs):
    merge_w, o_refs, l_refs, ffn_w, (out_ref, hand_ref, nat_ref, *work) = (
        refs[:2], refs[2:5], refs[5:8], refs[8:13], refs[13:])
    put, get = _hand_over(hand_ref)

    def write(y):
        out_ref[...] = y

    _two_stage(n_tiles,
               lambda: _merge_steps(lambda: h_ref[...], put, *merge_w, o_refs, l_refs, nat_ref),
               lambda: _ffn_steps(get, write, *ffn_w, *work))


def _merge_ffn(h, merge_w, outs, lses, ffn_w, batch, seq):
    m = h.shape[0]
    n_tiles = m // ROW_TILE
    specs, _ = _residue_major_specs(batch, seq, _F32, _cur_tile(n_tiles))
    return pl.pallas_call(
        functools.partial(_merge_ffn_kernel, n_tiles),
        grid=(n_tiles + 1,),
        in_specs=[_row_spec(_cur_tile(n_tiles)), _const_spec((1, D_MODEL)), _const_spec((GROUP_WIDTH, D_MODEL))]
        + specs * 2 + _ffn_weight_specs(),
        out_specs=_row_spec(_prev_tile()),
        out_shape=jax.ShapeDtypeStruct((m, D_MODEL), _F32),
        scratch_shapes=[_HANDOVER, pltpu.VMEM((5, PAIRS_PER_GROUP, ROW_TILE, LANES), _F32)] + _FFN_WORK,
        compiler_params=_params(("arbitrary",)),
        name="merge_ffn",
    )(h, *merge_w, *outs, *lses, *ffn_w)


def _attn_kernel(q_ref, kp_ref, kc_ref, vp_ref, vc_ref, o_ref, lse_ref):
    step = pl.program_id(1)
    n_blk = q_ref.shape[1] // ATTN_BLK
    win = 2 * ATTN_BLK
    qi = lax.broadcasted_iota(jnp.int32, (win, win), 0) % ATTN_BLK
    kj = lax.broadcasted_iota(jnp.int32, (win, win), 1)
    band = jnp.where((kj >= qi) & (kj <= qi + ATTN_BLK), 0.0, NEG_BIG)
    first_lo = jnp.maximum(qi, jnp.where(step == 0, ATTN_BLK, 0))
    band_first = jnp.where((kj >= first_lo) & (kj <= qi + ATTN_BLK), 0.0, NEG_BIG)
    q_lane = lax.broadcasted_iota(jnp.int32, (ATTN_BLK, LANES), 1)
    q_is_a = (q_lane % HEAD_DIM) < HEAD_DIM // 2
    v_lane = lax.broadcasted_iota(jnp.int32, (win, LANES), 1)
    v_is_a = v_lane < HEAD_DIM
    ones_a = jnp.where(v_is_a, 1.0, 0.0).astype(_BF16)
    ones_b = jnp.where(v_is_a, 0.0, 1.0).astype(_BF16)
    o_is_a = q_lane < HEAD_DIM
    nt = (((1,), (1,)), ((), ()))
    work = [(sq, j, p) for sq in range(q_ref.shape[0]) for j in range(n_blk) for p in range(PAIRS_PER_GROUP)]

    def window(prev_ref, cur_ref, sq, j, p):
        pair = slice(p * LANES, (p + 1) * LANES)
        if j == 0:
            return jnp.concatenate([prev_ref[sq, :, pair], cur_ref[sq, 0:ATTN_BLK, pair]], axis=0)
        return cur_ref[sq, (j - 1) * ATTN_BLK:(j + 1) * ATTN_BLK, pair]

    def scores(sq, j, p):
        q = q_ref[sq, j * ATTN_BLK:(j + 1) * ATTN_BLK, p * LANES:(p + 1) * LANES]
        zero = jnp.zeros_like(q)
        q2 = jnp.concatenate([jnp.where(q_is_a, q, zero), jnp.where(q_is_a, zero, q)], axis=0)
        s = lax.dot_general(q2, window(kp_ref, kc_ref, sq, j, p), nt, preferred_element_type=_F32)
        return s + (band_first if j == 0 else band)

    def finish(sq, j, p, s):
        m = jnp.max(s, axis=-1, keepdims=True)
        prob = jnp.exp2(s - m).astype(_BF16)
        prob = jnp.concatenate([prob[:ATTN_BLK], prob[ATTN_BLK:]], axis=1)
        v = window(vp_ref, vc_ref, sq, j, p)
        zero = jnp.zeros_like(v)
        v2 = jnp.concatenate([jnp.concatenate([jnp.where(v_is_a, v, zero), ones_a], axis=1),
                              jnp.concatenate([jnp.where(v_is_a, zero, v), ones_b], axis=1)], axis=0)
        r = _dot(prob, v2)
        den = r[:, LANES:]
        rows = slice(j * ATTN_BLK, (j + 1) * ATTN_BLK)
        cols = slice(p * LANES, (p + 1) * LANES)
        o_ref[sq, rows, cols] = r[:, :LANES] * (1.0 / den)
        lse_ref[sq, rows, cols] = jnp.where(o_is_a, m[:ATTN_BLK], m[ATTN_BLK:]) + jnp.log2(den)

    ahead = 1
    pending = [scores(*work[n]) for n in range(ahead)]
    for n, item in enumerate(work):
        s = pending.pop(0)
        if n + ahead < len(work):
            pending.append(scores(*work[n + ahead]))
        finish(*item, s)


def _attn_group(q, k, v):
    batch, d, rows, _ = q.shape
    n_seq = batch * d
    q, k, v = (a.reshape(n_seq, rows, GROUP_WIDTH) for a in (q, k, v))
    q_tile = min(ATTN_Q_TILE, rows)
    seqs = ATTN_Q_TILE // q_tile
    blocks_per_step = q_tile // ATTN_BLK

    def cur(s, i):
        return (s, i, 0)

    def prev(s, i):
        return (s, jnp.maximum(i * blocks_per_step - 1, 0), 0)

    cur_spec = pl.BlockSpec((seqs, q_tile, GROUP_WIDTH), cur)
    prev_spec = pl.BlockSpec((seqs, ATTN_BLK, GROUP_WIDTH), prev)
    out_shape = jax.ShapeDtypeStruct((n_seq, rows, GROUP_WIDTH), _F32)
    o, lse = pl.pallas_call(
        _attn_kernel,
        grid=(n_seq // seqs, rows // q_tile),
        in_specs=[cur_spec, prev_spec, cur_spec, prev_spec, cur_spec],
        out_specs=[cur_spec, cur_spec],
        out_shape=[out_shape, out_shape],
        compiler_params=_params(("parallel", "arbitrary")),
        name=f"window_attn_d{d}",
    )(q, k, k, v, v)
    shape = (batch, d, rows, GROUP_WIDTH)
    return o.reshape(shape), lse.reshape(shape)


def kernel(x, positions, norm_gain, ffn_w_gate, ffn_w_up, ffn_w_down, pool_w_in, pool_w_group, pool_scale,
           pool_w_out, kv_norm_gain, w_k, w_v, attn_w_q, attn_w_o):
    batch, seq, _ = x.shape
    m = batch * seq
    gains = norm_gain.reshape(2, 6, 1, D_MODEL)
    bf = lambda w: w.astype(_BF16)

    def ffn_f32(layer, slot):
        return tuple((w, (layer, slot), False) for w in (ffn_w_gate, ffn_w_up, ffn_w_down))

    def ffn_gains(layer, slot):
        return (gains[layer, 4 * slot], 0.5 * gains[layer, 4 * slot + 1])

    pos = positions.astype(_F32).reshape(m, 1)
    inv_freq = ROPE_THETA ** (-jnp.arange(0, HEAD_DIM, 2, dtype=_F32) / HEAD_DIM)
    invf = jnp.tile(inv_freq, LANES // (HEAD_DIM // 2)).reshape(1, LANES)

    h = x.reshape(m, D_MODEL)
    pool_w = (gains[0, 2], gains[0, 3], bf(pool_w_in[0]), pool_w_group[0],
              pool_scale[0].reshape(1, D_MODEL), pool_w_out[0])
    ffn00 = tuple(bf(w[lead]) for w, lead, _ in ffn_f32(0, 0))
    h, (*ffn01, wk_bf, wv_bf) = _ffn_pool(h, ffn_gains(0, 0) + ffn00, pool_w, seq,
                                          to_cast=ffn_f32(0, 1) + ((w_k, (), True), (w_v, (), False)))
    kv_w = (kv_norm_gain.reshape(1, D_MODEL), invf, wk_bf, wv_bf)
    h, k_sh, v_sh, tables, (*ffn10, wq_bf) = _ffn_kv(h, pos, ffn_gains(0, 1) + tuple(ffn01), kv_w, batch, seq,
                                                     to_cast=ffn_f32(1, 0) + ((attn_w_q, (0,), True),))
    h, q, (*ffn11, wo_bf) = _ffn_q(h, tables, ffn_gains(1, 0) + tuple(ffn10), (gains[1, 2], wq_bf), batch, seq,
                                   to_cast=ffn_f32(1, 1) + ((attn_w_o, (0,), False),))
    outs, lses = [], []
    for g in range(N_GROUPS):
        o, lse = _attn_group(q[g], k_sh[g], v_sh[g])
        outs.append(o)
        lses.append(lse)
    h = _merge_ffn(h, (gains[1, 3], wo_bf), outs, lses, ffn_gains(1, 1) + tuple(ffn11), batch, seq)
    return h.reshape(batch, seq, D_MODEL)
```
